```python
import jax, jax.numpy as jnp
from jax import lax
import numpy as np

D_MODEL = 1024
BATCH = 8
SEQ = 16384
DEPTH = 4

N_MIXERS = 2
N_LAYERS_A = (DEPTH + N_MIXERS - 1) // N_MIXERS
N_LAYERS_B = DEPTH // N_MIXERS
D_FF = 2816
FFN_RESIDUAL_SCALE = 0.5
N_SUBLAYER_NORMS = 6
SGU_DIM = 2 * D_MODEL
SGU_GROUPS = 8
SGU_GROUP_DIM = SGU_DIM // SGU_GROUPS
CHUNK = 128
CONV_DIM = D_MODEL
CONV_WIDTH = 31
EPS = 1e-6

kernel_name = "hybrid_sgu_conformer_conv_macaron"


def rms_norm(x, g):
    xf = x.astype(jnp.float32)
    y = xf * lax.rsqrt(jnp.mean(xf * xf, axis=-1, keepdims=True) + EPS)
    return (y * g.astype(jnp.float32)).astype(x.dtype)


def layer_norm(x, g, b):
    xf = x.astype(jnp.float32)
    mu = jnp.mean(xf, axis=-1, keepdims=True)
    xc = xf - mu
    var = jnp.mean(xc * xc, axis=-1, keepdims=True)
    y = xc * lax.rsqrt(var + EPS) * g.astype(jnp.float32) + b.astype(jnp.float32)
    return y.astype(x.dtype)


def swiglu(h, w_gate, w_up, w_down):
    return (jax.nn.silu(h @ w_gate) * (h @ w_up)) @ w_down


def sgu_mixer(h, w_in, ln_g, ln_b, w_spatial, b_spatial, w_out):
    bsz, seq, _ = h.shape
    z = jax.nn.gelu(h @ w_in)
    u, v = jnp.split(z, 2, axis=-1)
    v = layer_norm(v, ln_g, ln_b)
    vc = v.reshape(bsz, seq // CHUNK, CHUNK, SGU_GROUPS, SGU_GROUP_DIM)
    causal = jnp.tril(jnp.ones((CHUNK, CHUNK), dtype=bool))
    ws = jnp.where(causal[None], w_spatial, 0)
    mixed = jnp.einsum('gts,bcsgd->bctgd', ws, vc)
    mixed = mixed + b_spatial.T[None, None, :, :, None]
    gated = u * mixed.reshape(bsz, seq, SGU_DIM)
    return gated @ w_out


def conv_mixer(h, w_pw1, w_dw, b_dw, ln_g, ln_b, w_pw2):
    a, gate = jnp.split(h @ w_pw1, 2, axis=-1)
    y = a * jax.nn.sigmoid(gate)
    y = lax.conv_general_dilated(
        y, w_dw[:, None, :].astype(y.dtype),
        window_strides=(1,),
        padding=[(CONV_WIDTH - 1, 0)],
        dimension_numbers=('NWC', 'WIO', 'NWC'),
        feature_group_count=CONV_DIM) + b_dw
    y = jax.nn.silu(layer_norm(y, ln_g, ln_b))
    return y @ w_pw2


def _fwd_setup_inputs(seed: int = 0) -> dict:
    key = jax.random.key(seed)
    ks = jax.random.split(key, 18)
    f32 = jnp.float32
    nrm = lambda k, shape, scale: (jax.random.normal(k, shape, f32) * scale).astype(f32)
    x = jax.random.normal(ks[0], (BATCH, SEQ, D_MODEL), f32)
    norm_g = 1.0 + nrm(ks[1], (DEPTH, N_SUBLAYER_NORMS, D_MODEL), 0.05)
    ff_w_gate = nrm(ks[2], (DEPTH, 2, D_MODEL, D_FF), D_MODEL ** -0.5)
    ff_w_up = nrm(ks[3], (DEPTH, 2, D_MODEL, D_FF), D_MODEL ** -0.5)
    ff_w_down = nrm(ks[4], (DEPTH, 2, D_FF, D_MODEL), D_FF ** -0.5)
    sgu_w_in = nrm(ks[5], (N_LAYERS_A, D_MODEL, 2 * SGU_DIM), D_MODEL ** -0.5)
    sgu_ln_g = 1.0 + nrm(ks[6], (N_LAYERS_A, SGU_DIM), 0.05)
    sgu_ln_b = nrm(ks[7], (N_LAYERS_A, SGU_DIM), 0.02)
    sgu_w_spatial = nrm(ks[8], (N_LAYERS_A, SGU_GROUPS, CHUNK, CHUNK), CHUNK ** -0.5)
    sgu_b_spatial = 1.0 + nrm(ks[9], (N_LAYERS_A, SGU_GROUPS, CHUNK), 0.1)
    sgu_w_out = nrm(ks[10], (N_LAYERS_A, SGU_DIM, D_MODEL), SGU_DIM ** -0.5)
    conv_w_pw1 = nrm(ks[11], (N_LAYERS_B, D_MODEL, 2 * CONV_DIM), D_MODEL ** -0.5)
    conv_w_dw = nrm(ks[12], (N_LAYERS_B, CONV_WIDTH, CONV_DIM), CONV_WIDTH ** -0.5)
    conv_b_dw = nrm(ks[13], (N_LAYERS_B, CONV_DIM), 0.02)
    conv_ln_g = 1.0 + nrm(ks[14], (N_LAYERS_B, CONV_DIM), 0.05)
    conv_ln_b = nrm(ks[15], (N_LAYERS_B, CONV_DIM), 0.02)
    conv_w_pw2 = nrm(ks[16], (N_LAYERS_B, CONV_DIM, D_MODEL), CONV_DIM ** -0.5)
    return {
        "x": x, "norm_g": norm_g,
        "ff_w_gate": ff_w_gate, "ff_w_up": ff_w_up, "ff_w_down": ff_w_down,
        "sgu_w_in": sgu_w_in, "sgu_ln_g": sgu_ln_g, "sgu_ln_b": sgu_ln_b,
        "sgu_w_spatial": sgu_w_spatial, "sgu_b_spatial": sgu_b_spatial, "sgu_w_out": sgu_w_out,
        "conv_w_pw1": conv_w_pw1, "conv_w_dw": conv_w_dw, "conv_b_dw": conv_b_dw,
        "conv_ln_g": conv_ln_g, "conv_ln_b": conv_ln_b, "conv_w_pw2": conv_w_pw2,
    }


def _fwd_reference(x, norm_g, ff_w_gate, ff_w_up, ff_w_down,
              sgu_w_in, sgu_ln_g, sgu_ln_b, sgu_w_spatial, sgu_b_spatial, sgu_w_out,
              conv_w_pw1, conv_w_dw, conv_b_dw, conv_ln_g, conv_ln_b, conv_w_pw2):
    for i in range(DEPTH):
        g = norm_g[i]
        h = swiglu(rms_norm(x, g[0]), ff_w_gate[i, 0], ff_w_up[i, 0], ff_w_down[i, 0])
        x = x + FFN_RESIDUAL_SCALE * rms_norm(h, g[1])
        hn = rms_norm(x, g[2])
        j = i // N_MIXERS
        if i % N_MIXERS == 0:
            m = sgu_mixer(hn, sgu_w_in[j], sgu_ln_g[j], sgu_ln_b[j],
                          sgu_w_spatial[j], sgu_b_spatial[j], sgu_w_out[j])
        else:
            m = conv_mixer(hn, conv_w_pw1[j], conv_w_dw[j], conv_b_dw[j],
                           conv_ln_g[j], conv_ln_b[j], conv_w_pw2[j])
        x = x + rms_norm(m, g[3])
        h = swiglu(rms_norm(x, g[4]), ff_w_gate[i, 1], ff_w_up[i, 1], ff_w_down[i, 1])
        x = x + FFN_RESIDUAL_SCALE * rms_norm(h, g[5])
    return x


import jax as _jax
import jax.numpy as _jnp

TWIN_FORMAT = 'train_step'
FWD_PARAMS = ['x', 'norm_g', 'ff_w_gate', 'ff_w_up', 'ff_w_down', 'sgu_w_in', 'sgu_ln_g', 'sgu_ln_b', 'sgu_w_spatial', 'sgu_b_spatial', 'sgu_w_out', 'conv_w_pw1', 'conv_w_dw', 'conv_b_dw', 'conv_ln_g', 'conv_ln_b', 'conv_w_pw2']
TWIN_WEIGHTS = ['norm_g', 'ff_w_gate', 'ff_w_up', 'ff_w_down', 'sgu_w_in', 'sgu_ln_g', 'sgu_ln_b', 'sgu_w_spatial', 'sgu_b_spatial', 'sgu_w_out', 'conv_w_pw1', 'conv_w_dw', 'conv_b_dw', 'conv_ln_g', 'conv_ln_b', 'conv_w_pw2']
TWIN_DIFF_INPUT = 'x'
TWIN_INPUTS = ['x', 'norm_g', 'ff_w_gate', 'ff_w_up', 'ff_w_down', 'sgu_w_in', 'sgu_ln_g', 'sgu_ln_b', 'sgu_w_spatial', 'sgu_b_spatial', 'sgu_w_out', 'conv_w_pw1', 'conv_w_dw', 'conv_b_dw', 'conv_ln_g', 'conv_ln_b', 'conv_w_pw2', 'loss_target', 'm_norm_g', 'm_ff_w_gate', 'm_ff_w_up', 'm_ff_w_down', 'm_sgu_w_in', 'm_sgu_ln_g', 'm_sgu_ln_b', 'm_sgu_w_spatial', 'm_sgu_b_spatial', 'm_sgu_w_out', 'm_conv_w_pw1', 'm_conv_w_dw', 'm_conv_b_dw', 'm_conv_ln_g', 'm_conv_ln_b', 'm_conv_w_pw2', 'v_norm_g', 'v_ff_w_gate', 'v_ff_w_up', 'v_ff_w_down', 'v_sgu_w_in', 'v_sgu_ln_g', 'v_sgu_ln_b', 'v_sgu_w_spatial', 'v_sgu_b_spatial', 'v_sgu_w_out', 'v_conv_w_pw1', 'v_conv_w_dw', 'v_conv_b_dw', 'v_conv_ln_g', 'v_conv_ln_b', 'v_conv_w_pw2']
TWIN_OUTPUTS = ['loss', 'grad_x', 'grad_norm_g', 'grad_ff_w_gate', 'grad_ff_w_up', 'grad_ff_w_down', 'grad_sgu_w_in', 'grad_sgu_ln_g', 'grad_sgu_ln_b', 'grad_sgu_w_spatial', 'grad_sgu_b_spatial', 'grad_sgu_w_out', 'grad_conv_w_pw1', 'grad_conv_w_dw', 'grad_conv_b_dw', 'grad_conv_ln_g', 'grad_conv_ln_b', 'grad_conv_w_pw2', 'delta_norm_g', 'delta_ff_w_gate', 'delta_ff_w_up', 'delta_ff_w_down', 'delta_sgu_w_in', 'delta_sgu_ln_g', 'delta_sgu_ln_b', 'delta_sgu_w_spatial', 'delta_sgu_b_spatial', 'delta_sgu_w_out', 'delta_conv_w_pw1', 'delta_conv_w_dw', 'delta_conv_b_dw', 'delta_conv_ln_g', 'delta_conv_ln_b', 'delta_conv_w_pw2', 'new_m_norm_g', 'new_m_ff_w_gate', 'new_m_ff_w_up', 'new_m_ff_w_down', 'new_m_sgu_w_in', 'new_m_sgu_ln_g', 'new_m_sgu_ln_b', 'new_m_sgu_w_spatial', 'new_m_sgu_b_spatial', 'new_m_sgu_w_out', 'new_m_conv_w_pw1', 'new_m_conv_w_dw', 'new_m_conv_b_dw', 'new_m_conv_ln_g', 'new_m_conv_ln_b', 'new_m_conv_w_pw2', 'new_v_norm_g', 'new_v_ff_w_gate', 'new_v_ff_w_up', 'new_v_ff_w_down', 'new_v_sgu_w_in', 'new_v_sgu_ln_g', 'new_v_sgu_ln_b', 'new_v_sgu_w_spatial', 'new_v_sgu_b_spatial', 'new_v_sgu_w_out', 'new_v_conv_w_pw1', 'new_v_conv_w_dw', 'new_v_conv_b_dw', 'new_v_conv_ln_g', 'new_v_conv_ln_b', 'new_v_conv_w_pw2']
TWIN_LEAF_KINDS = {'loss': 'loss', 'grad_x': 'grad_x', 'grad_norm_g': 'grad_w', 'grad_ff_w_gate': 'grad_w', 'grad_ff_w_up': 'grad_w', 'grad_ff_w_down': 'grad_w', 'grad_sgu_w_in': 'grad_w', 'grad_sgu_ln_g': 'grad_w', 'grad_sgu_ln_b': 'grad_w', 'grad_sgu_w_spatial': 'grad_w', 'grad_sgu_b_spatial': 'grad_w', 'grad_sgu_w_out': 'grad_w', 'grad_conv_w_pw1': 'grad_w', 'grad_conv_w_dw': 'grad_w', 'grad_conv_b_dw': 'grad_w', 'grad_conv_ln_g': 'grad_w', 'grad_conv_ln_b': 'grad_w', 'grad_conv_w_pw2': 'grad_w', 'delta_norm_g': 'delta_w', 'delta_ff_w_gate': 'delta_w', 'delta_ff_w_up': 'delta_w', 'delta_ff_w_down': 'delta_w', 'delta_sgu_w_in': 'delta_w', 'delta_sgu_ln_g': 'delta_w', 'delta_sgu_ln_b': 'delta_w', 'delta_sgu_w_spatial': 'delta_w', 'delta_sgu_b_spatial': 'delta_w', 'delta_sgu_w_out': 'delta_w', 'delta_conv_w_pw1': 'delta_w', 'delta_conv_w_dw': 'delta_w', 'delta_conv_b_dw': 'delta_w', 'delta_conv_ln_g': 'delta_w', 'delta_conv_ln_b': 'delta_w', 'delta_conv_w_pw2': 'delta_w', 'new_m_norm_g': 'new_m', 'new_m_ff_w_gate': 'new_m', 'new_m_ff_w_up': 'new_m', 'new_m_ff_w_down': 'new_m', 'new_m_sgu_w_in': 'new_m', 'new_m_sgu_ln_g': 'new_m', 'new_m_sgu_ln_b': 'new_m', 'new_m_sgu_w_spatial': 'new_m', 'new_m_sgu_b_spatial': 'new_m', 'new_m_sgu_w_out': 'new_m', 'new_m_conv_w_pw1': 'new_m', 'new_m_conv_w_dw': 'new_m', 'new_m_conv_b_dw': 'new_m', 'new_m_conv_ln_g': 'new_m', 'new_m_conv_ln_b': 'new_m', 'new_m_conv_w_pw2': 'new_m', 'new_v_norm_g': 'new_v', 'new_v_ff_w_gate': 'new_v', 'new_v_ff_w_up': 'new_v', 'new_v_ff_w_down': 'new_v', 'new_v_sgu_w_in': 'new_v', 'new_v_sgu_ln_g': 'new_v', 'new_v_sgu_ln_b': 'new_v', 'new_v_sgu_w_spatial': 'new_v', 'new_v_sgu_b_spatial': 'new_v', 'new_v_sgu_w_out': 'new_v', 'new_v_conv_w_pw1': 'new_v', 'new_v_conv_w_dw': 'new_v', 'new_v_conv_b_dw': 'new_v', 'new_v_conv_ln_g': 'new_v', 'new_v_conv_ln_b': 'new_v', 'new_v_conv_w_pw2': 'new_v'}


def _forward(args):
    return _fwd_reference(*[args[k] for k in FWD_PARAMS])


def _output_shape():
    def fwd():
        inp = _fwd_setup_inputs(0)
        return _fwd_reference(*[inp[k] for k in FWD_PARAMS])
    out = _jax.eval_shape(fwd)
    return out.shape, out.dtype

N_MICROBATCH = 1
ADAM_LR = 0.001
ADAM_B1 = 0.9
ADAM_B2 = 0.999
ADAM_EPS = 1e-08
ADAM_WD = 0.01
ADAM_STEP = 10
PER_EXAMPLE_BATCH_AXIS = {'x': 0, 'loss_target': 0}
SHARED_INPUTS = []
_WEIGHT_DTYPES = {'norm_g': _jnp.float32, 'ff_w_gate': _jnp.float32, 'ff_w_up': _jnp.float32, 'ff_w_down': _jnp.float32, 'sgu_w_in': _jnp.float32, 'sgu_ln_g': _jnp.float32, 'sgu_ln_b': _jnp.float32, 'sgu_w_spatial': _jnp.float32, 'sgu_b_spatial': _jnp.float32, 'sgu_w_out': _jnp.float32, 'conv_w_pw1': _jnp.float32, 'conv_w_dw': _jnp.float32, 'conv_b_dw': _jnp.float32, 'conv_ln_g': _jnp.float32, 'conv_ln_b': _jnp.float32, 'conv_w_pw2': _jnp.float32}
MOMENT_SCALE = {'norm_g': 5.851875e+01, 'ff_w_gate': 1.459611e+00, 'ff_w_up': 2.717076e+00, 'ff_w_down': 4.523064e+00, 'sgu_w_in': 4.074183e+00, 'sgu_ln_g': 7.787218e-01, 'sgu_ln_b': 9.181933e-01, 'sgu_w_spatial': 1.067764e+00, 'sgu_b_spatial': 2.324669e+00, 'sgu_w_out': 3.873029e+01, 'conv_w_pw1': 1.270341e+01, 'conv_w_dw': 1.979896e+01, 'conv_b_dw': 1.310405e+02, 'conv_ln_g': 5.804267e+01, 'conv_ln_b': 8.060577e+01, 'conv_w_pw2': 3.818326e+01}


def _to_microbatches(a, axis):
    t = _jnp.moveaxis(a, axis, 0)
    t = t.reshape((N_MICROBATCH, t.shape[0] // N_MICROBATCH) + t.shape[1:])
    return _jnp.moveaxis(t, 1, axis + 1)


def setup_inputs(seed: int = 0) -> dict:
    inp = _fwd_setup_inputs(seed)
    key = _jax.random.fold_in(_jax.random.key(seed), 7919)
    shape, _ = _output_shape()
    out = dict(inp)
    out["loss_target"] = _jax.random.normal(_jax.random.fold_in(key, 0), shape, _jnp.float32)
    for i, name in enumerate(TWIN_WEIGHTS):
        w = inp[name].astype(_jnp.float32)
        if MOMENT_SCALE is None:
            s = _jnp.sqrt(_jnp.mean(_jnp.square(w)) + 1e-30)
        else:
            s = MOMENT_SCALE[name]
        km, kv = _jax.random.split(_jax.random.fold_in(key, i + 1))
        out[name] = w
        out["m_" + name] = s * _jax.random.normal(km, w.shape, _jnp.float32)
        out["v_" + name] = (s * s) * _jax.random.uniform(kv, w.shape, _jnp.float32, 0.5, 1.5)
    if N_MICROBATCH > 1:
        for name, axis in PER_EXAMPLE_BATCH_AXIS.items():
            out[name] = _to_microbatches(out[name], axis)
    return {'x': out['x'], 'norm_g': out['norm_g'], 'ff_w_gate': out['ff_w_gate'], 'ff_w_up': out['ff_w_up'], 'ff_w_down': out['ff_w_down'], 'sgu_w_in': out['sgu_w_in'], 'sgu_ln_g': out['sgu_ln_g'], 'sgu_ln_b': out['sgu_ln_b'], 'sgu_w_spatial': out['sgu_w_spatial'], 'sgu_b_spatial': out['sgu_b_spatial'], 'sgu_w_out': out['sgu_w_out'], 'conv_w_pw1': out['conv_w_pw1'], 'conv_w_dw': out['conv_w_dw'], 'conv_b_dw': out['conv_b_dw'], 'conv_ln_g': out['conv_ln_g'], 'conv_ln_b': out['conv_ln_b'], 'conv_w_pw2': out['conv_w_pw2'], 'loss_target': out['loss_target'], 'm_norm_g': out['m_norm_g'], 'm_ff_w_gate': out['m_ff_w_gate'], 'm_ff_w_up': out['m_ff_w_up'], 'm_ff_w_down': out['m_ff_w_down'], 'm_sgu_w_in': out['m_sgu_w_in'], 'm_sgu_ln_g': out['m_sgu_ln_g'], 'm_sgu_ln_b': out['m_sgu_ln_b'], 'm_sgu_w_spatial': out['m_sgu_w_spatial'], 'm_sgu_b_spatial': out['m_sgu_b_spatial'], 'm_sgu_w_out': out['m_sgu_w_out'], 'm_conv_w_pw1': out['m_conv_w_pw1'], 'm_conv_w_dw': out['m_conv_w_dw'], 'm_conv_b_dw': out['m_conv_b_dw'], 'm_conv_ln_g': out['m_conv_ln_g'], 'm_conv_ln_b': out['m_conv_ln_b'], 'm_conv_w_pw2': out['m_conv_w_pw2'], 'v_norm_g': out['v_norm_g'], 'v_ff_w_gate': out['v_ff_w_gate'], 'v_ff_w_up': out['v_ff_w_up'], 'v_ff_w_down': out['v_ff_w_down'], 'v_sgu_w_in': out['v_sgu_w_in'], 'v_sgu_ln_g': out['v_sgu_ln_g'], 'v_sgu_ln_b': out['v_sgu_ln_b'], 'v_sgu_w_spatial': out['v_sgu_w_spatial'], 'v_sgu_b_spatial': out['v_sgu_b_spatial'], 'v_sgu_w_out': out['v_sgu_w_out'], 'v_conv_w_pw1': out['v_conv_w_pw1'], 'v_conv_w_dw': out['v_conv_w_dw'], 'v_conv_b_dw': out['v_conv_b_dw'], 'v_conv_ln_g': out['v_conv_ln_g'], 'v_conv_ln_b': out['v_conv_ln_b'], 'v_conv_w_pw2': out['v_conv_w_pw2']}


def _loss(weights, diff, rest, loss_target):
    with _jax.named_scope("forward"):
        args = {**rest, TWIN_DIFF_INPUT: diff, **{k: w.astype(_WEIGHT_DTYPES[k]) for k, w in weights.items()}}
        y = _forward(args)
    with _jax.named_scope("loss_head"):
        err = _jnp.square(y.astype(_jnp.float32) - loss_target)
        return 0.5 * _jnp.sum(_jnp.mean(err, axis=-1)) if err.ndim else 0.5 * err


def _adamw(w, g, m, v):
    m = ADAM_B1 * m + (1.0 - ADAM_B1) * g
    v = ADAM_B2 * v + (1.0 - ADAM_B2) * _jnp.square(g)
    m_hat = m / (1.0 - ADAM_B1 ** ADAM_STEP)
    v_hat = v / (1.0 - ADAM_B2 ** ADAM_STEP)
    delta = -ADAM_LR * (m_hat / (_jnp.sqrt(v_hat) + ADAM_EPS) + ADAM_WD * w)
    return delta, m, v


def reference(x, norm_g, ff_w_gate, ff_w_up, ff_w_down, sgu_w_in, sgu_ln_g, sgu_ln_b, sgu_w_spatial, sgu_b_spatial, sgu_w_out, conv_w_pw1, conv_w_dw, conv_b_dw, conv_ln_g, conv_ln_b, conv_w_pw2, loss_target, m_norm_g, m_ff_w_gate, m_ff_w_up, m_ff_w_down, m_sgu_w_in, m_sgu_ln_g, m_sgu_ln_b, m_sgu_w_spatial, m_sgu_b_spatial, m_sgu_w_out, m_conv_w_pw1, m_conv_w_dw, m_conv_b_dw, m_conv_ln_g, m_conv_ln_b, m_conv_w_pw2, v_norm_g, v_ff_w_gate, v_ff_w_up, v_ff_w_down, v_sgu_w_in, v_sgu_ln_g, v_sgu_ln_b, v_sgu_w_spatial, v_sgu_b_spatial, v_sgu_w_out, v_conv_w_pw1, v_conv_w_dw, v_conv_b_dw, v_conv_ln_g, v_conv_ln_b, v_conv_w_pw2):
    given = dict(x=x, norm_g=norm_g, ff_w_gate=ff_w_gate, ff_w_up=ff_w_up, ff_w_down=ff_w_down, sgu_w_in=sgu_w_in, sgu_ln_g=sgu_ln_g, sgu_ln_b=sgu_ln_b, sgu_w_spatial=sgu_w_spatial, sgu_b_spatial=sgu_b_spatial, sgu_w_out=sgu_w_out, conv_w_pw1=conv_w_pw1, conv_w_dw=conv_w_dw, conv_b_dw=conv_b_dw, conv_ln_g=conv_ln_g, conv_ln_b=conv_ln_b, conv_w_pw2=conv_w_pw2, loss_target=loss_target, m_norm_g=m_norm_g, m_ff_w_gate=m_ff_w_gate, m_ff_w_up=m_ff_w_up, m_ff_w_down=m_ff_w_down, m_sgu_w_in=m_sgu_w_in, m_sgu_ln_g=m_sgu_ln_g, m_sgu_ln_b=m_sgu_ln_b, m_sgu_w_spatial=m_sgu_w_spatial, m_sgu_b_spatial=m_sgu_b_spatial, m_sgu_w_out=m_sgu_w_out, m_conv_w_pw1=m_conv_w_pw1, m_conv_w_dw=m_conv_w_dw, m_conv_b_dw=m_conv_b_dw, m_conv_ln_g=m_conv_ln_g, m_conv_ln_b=m_conv_ln_b, m_conv_w_pw2=m_conv_w_pw2, v_norm_g=v_norm_g, v_ff_w_gate=v_ff_w_gate, v_ff_w_up=v_ff_w_up, v_ff_w_down=v_ff_w_down, v_sgu_w_in=v_sgu_w_in, v_sgu_ln_g=v_sgu_ln_g, v_sgu_ln_b=v_sgu_ln_b, v_sgu_w_spatial=v_sgu_w_spatial, v_sgu_b_spatial=v_sgu_b_spatial, v_sgu_w_out=v_sgu_w_out, v_conv_w_pw1=v_conv_w_pw1, v_conv_w_dw=v_conv_w_dw, v_conv_b_dw=v_conv_b_dw, v_conv_ln_g=v_conv_ln_g, v_conv_ln_b=v_conv_ln_b, v_conv_w_pw2=v_conv_w_pw2)
    weights = {n: given[n] for n in TWIN_WEIGHTS}
    shared = {n: given[n] for n in SHARED_INPUTS}
    per_example = {n: given[n] for n in ['x']}
    grad_fn = _jax.value_and_grad(_loss, argnums=(0, 1))

    def one_microbatch(ex, loss_target):
        ex = dict(ex)
        diff = ex.pop(TWIN_DIFF_INPUT)
        return grad_fn(weights, diff, {**shared, **ex}, loss_target)

    if N_MICROBATCH == 1:
        loss, (grad_w, grad_x) = one_microbatch(per_example, given["loss_target"])
    else:
        def body(carry, xs):
            loss_sum, grad_sum = carry
            l_k, (gw_k, gx_k) = one_microbatch(xs[0], xs[1])
            with _jax.named_scope("update"):
                return (loss_sum + l_k, _jax.tree.map(_jnp.add, grad_sum, gw_k)), gx_k

        init = (_jnp.zeros((), _jnp.float32), _jax.tree.map(_jnp.zeros_like, weights))
        (loss, grad_w), grad_x = _jax.lax.scan(body, init, (per_example, given["loss_target"]))
    with _jax.named_scope("update"):
        delta_w, new_m, new_v = {}, {}, {}
        for n in TWIN_WEIGHTS:
            delta_w[n], new_m[n], new_v[n] = _adamw(weights[n], grad_w[n], given["m_" + n], given["v_" + n])
    return (loss, grad_x, *[grad_w[n] for n in TWIN_WEIGHTS], *[delta_w[n] for n in TWIN_WEIGHTS],
            *[new_m[n] for n in TWIN_WEIGHTS], *[new_v[n] for n in TWIN_WEIGHTS])
```

```python
import jax
import jax.numpy as jnp
from jax import lax
from jax.experimental import pallas as pl
from jax.experimental.pallas import tpu as pltpu

F32 = jnp.float32
BF16 = jnp.bfloat16

EPS = 1e-6
FFN_SCALE = 0.5
N_MIXERS = 2
CHUNK = 128
GROUPS = 8
CONV_W = 31
HALO = 32
GELU_C0 = 0.7978845608028654
GELU_C1 = 0.044715
ADAM_LR, ADAM_B1, ADAM_B2, ADAM_EPS, ADAM_WD, ADAM_STEP = 0.001, 0.9, 0.999, 1e-08, 0.01, 10

MESH_AXES = ("x", "y", "c")
N_DEV = 8
VMEM_LIMIT_V7X = 56 * 1024 * 1024
WGRAD_ACC_BYTES = 16 * 1024 * 1024

NT = (((1,), (1,)), ((), ()))
TN = (((0,), (0,)), ((), ()))


def _pcall(body, **kw):
    return pl.pallas_call(body, **kw)


def _cparams(n_axes):
    return pltpu.CompilerParams(dimension_semantics=("arbitrary",) * n_axes, vmem_limit_bytes=VMEM_LIMIT_V7X)


def _tile(n, pref, align):
    if n <= pref:
        return n
    t = (pref // align) * align
    while t > align and n % t:
        t -= align
    assert n % t == 0, (n, pref, align)
    return t


def _rstd(x):
    return lax.rsqrt(jnp.mean(x * x, axis=-1, keepdims=True) + EPS)


def _sum8(v):
    t, d = v.shape
    return v.reshape(t // 8, 8, d).sum(axis=0)


def _sigmoid(x):
    return jax.nn.sigmoid(x)


def _gelu_tanh(z):
    return jnp.tanh(GELU_C0 * (z + GELU_C1 * z * z * z))


def _gelu_grad(z, t):
    return 0.5 * (1.0 + t) + 0.5 * z * (1.0 - t * t) * (GELU_C0 * (1.0 + 3.0 * GELU_C1 * z * z))


def _first_step(n_axes):
    first = pl.program_id(0) == 0
    for a in range(1, n_axes):
        first = jnp.logical_and(first, pl.program_id(a) == 0)
    return first


def rms_matmul_nt(x, g, wts, tt, nc, name):
    T, D = x.shape
    N = wts[0].shape[0]
    nw = len(wts)

    def body(*refs):
        x_ref, g_ref = refs[0], refs[1]
        w_refs, o_refs, h_ref = refs[2 : 2 + nw], refs[2 + nw : 2 + 2 * nw], refs[-1]

        @pl.when(pl.program_id(1) == 0)
        def _():
            xv = x_ref[...]
            h_ref[...] = (xv * _rstd(xv) * g_ref[...]).astype(BF16)

        h = h_ref[...]
        for w_ref, o_ref in zip(w_refs, o_refs):
            o_ref[...] = lax.dot_general(h, w_ref[...], NT, preferred_element_type=F32).astype(BF16)

    return _pcall(
        body,
        name=name,
        grid=(T // tt, N // nc),
        in_specs=[pl.BlockSpec((tt, D), lambda i, j: (i, 0)), pl.BlockSpec((1, D), lambda i, j: (0, 0))]
        + [pl.BlockSpec((nc, D), lambda i, j: (j, 0))] * nw,
        out_specs=[pl.BlockSpec((tt, nc), lambda i, j: (i, j))] * nw,
        out_shape=[jax.ShapeDtypeStruct((T, N), BF16)] * nw,
        scratch_shapes=[pltpu.VMEM((tt, D), BF16)],
        compiler_params=_cparams(2),
    )(x, g, *wts)


def matmul_rms_res(x, a_in, w, g, scale, tt, kc, name):
    T, D = x.shape
    K = w.shape[0]
    nk = K // kc
    act = len(a_in) == 2

    def body(*refs):
        if act:
            x_ref, gate_ref, up_ref, w_ref, g_ref, xo_ref, m_ref, a_out_ref, acc_ref = refs
        else:
            x_ref, a_ref, w_ref, g_ref, xo_ref, m_ref, acc_ref = refs
        j = pl.program_id(1)

        @pl.when(j == 0)
        def _():
            acc_ref[...] = jnp.zeros_like(acc_ref)

        if act:
            gate = gate_ref[...].astype(F32)
            a = (gate * _sigmoid(gate) * up_ref[...].astype(F32)).astype(BF16)
            a_out_ref[...] = a
        else:
            a = a_ref[...]
        acc_ref[...] += jnp.dot(a, w_ref[...], preferred_element_type=F32)

        @pl.when(j == nk - 1)
        def _():
            m = acc_ref[...]
            m_ref[...] = m
            xo_ref[...] = x_ref[...] + scale * (m * _rstd(m) * g_ref[...])

    row = pl.BlockSpec((tt, D), lambda i, j: (i, 0))
    col = pl.BlockSpec((tt, kc), lambda i, j: (i, j))
    out_specs = [row, row] + ([col] if act else [])
    out_shape = [jax.ShapeDtypeStruct((T, D), F32)] * 2 + ([jax.ShapeDtypeStruct((T, K), BF16)] if act else [])
    return _pcall(
        body,
        name=name,
        grid=(T // tt, nk),
        in_specs=[row] + [col] * len(a_in) + [pl.BlockSpec((kc, D), lambda i, j: (j, 0)), pl.BlockSpec((1, D), lambda i, j: (0, 0))],
        out_specs=out_specs,
        out_shape=out_shape,
        scratch_shapes=[pltpu.VMEM((tt, D), F32)],
        compiler_params=_cparams(2),
    )(x, *a_in, w, g)


def rms_bwd_matmul_nt(dxo, m, g, w, scale, gate_up, tt, kc, name):
    T, D = dxo.shape
    K = w.shape[0]
    act = gate_up is not None

    def body(*refs):
        if act:
            dxo_ref, m_ref, g_ref, w_ref, gate_ref, up_ref, dm_ref, dg_ref, dgate_ref, dup_ref, dms_ref = refs
        else:
            dxo_ref, m_ref, g_ref, w_ref, dm_ref, dg_ref, da_ref, dms_ref = refs

        @pl.when(_first_step(2))
        def _():
            dg_ref[...] = jnp.zeros_like(dg_ref)

        @pl.when(pl.program_id(1) == 0)
        def _():
            dy = scale * dxo_ref[...]
            mv = m_ref[...]
            r = _rstd(mv)
            mh = mv * r
            dg_ref[...] += _sum8(dy * mh)
            dmh = dy * g_ref[...]
            dm = (r * (dmh - mh * jnp.mean(dmh * mh, axis=-1, keepdims=True))).astype(BF16)
            dms_ref[...] = dm
            dm_ref[...] = dm

        da = lax.dot_general(dms_ref[...], w_ref[...], NT, preferred_element_type=F32)
        if act:
            gate = gate_ref[...].astype(F32)
            up = up_ref[...].astype(F32)
            sg = _sigmoid(gate)
            dup_ref[...] = (da * (gate * sg)).astype(BF16)
            dgate_ref[...] = (da * up * (sg * (1.0 + gate * (1.0 - sg)))).astype(BF16)
        else:
            da_ref[...] = da.astype(BF16)

    row = pl.BlockSpec((tt, D), lambda i, j: (i, 0))
    col = pl.BlockSpec((tt, kc), lambda i, j: (i, j))
    vec = pl.BlockSpec((1, D), lambda i, j: (0, 0))
    n_col = 2 if act else 1
    return _pcall(
        body,
        name=name,
        grid=(T // tt, K // kc),
        in_specs=[row, row, vec, pl.BlockSpec((kc, D), lambda i, j: (j, 0))] + ([col, col] if act else []),
        out_specs=[row, pl.BlockSpec((8, D), lambda i, j: (0, 0))] + [col] * n_col,
        out_shape=[jax.ShapeDtypeStruct((T, D), BF16), jax.ShapeDtypeStruct((8, D), F32)] + [jax.ShapeDtypeStruct((T, K), BF16)] * n_col,
        scratch_shapes=[pltpu.VMEM((tt, D), BF16)],
        compiler_params=_cparams(2),
    )(dxo, m, g, w, *(gate_up or ()))


def matmul_rms_bwd(dxo, x, g, pairs, tt, nc, name):
    T, D = x.shape
    N = pairs[0][0].shape[1]
    nn = N // nc
    npairs = len(pairs)

    def body(*refs):
        dxo_ref, x_ref, g_ref = refs[:3]
        dz_refs = refs[3 : 3 + npairs]
        wt_refs = refs[3 + npairs : 3 + 2 * npairs]
        dx_ref, h_ref, dg_ref, acc_ref = refs[3 + 2 * npairs :]
        j = pl.program_id(1)

        @pl.when(_first_step(2))
        def _():
            dg_ref[...] = jnp.zeros_like(dg_ref)

        @pl.when(j == 0)
        def _():
            acc_ref[...] = jnp.zeros_like(acc_ref)

        for dz_ref, wt_ref in zip(dz_refs, wt_refs):
            acc_ref[...] += jnp.dot(dz_ref[...], wt_ref[...], preferred_element_type=F32)

        @pl.when(j == nn - 1)
        def _():
            xv = x_ref[...]
            gv = g_ref[...]
            r = _rstd(xv)
            xh = xv * r
            h_ref[...] = (xh * gv).astype(BF16)
            dh = acc_ref[...]
            dg_ref[...] += _sum8(dh * xh)
            dxh = dh * gv
            dx_ref[...] = dxo_ref[...] + r * (dxh - xh * jnp.mean(dxh * xh, axis=-1, keepdims=True))

    row = pl.BlockSpec((tt, D), lambda i, j: (i, 0))
    return _pcall(
        body,
        name=name,
        grid=(T // tt, nn),
        in_specs=[row, row, pl.BlockSpec((1, D), lambda i, j: (0, 0))]
        + [pl.BlockSpec((tt, nc), lambda i, j: (i, j))] * npairs
        + [pl.BlockSpec((nc, D), lambda i, j: (j, 0))] * npairs,
        out_specs=[row, row, pl.BlockSpec((8, D), lambda i, j: (0, 0))],
        out_shape=[jax.ShapeDtypeStruct((T, D), F32), jax.ShapeDtypeStruct((T, D), BF16), jax.ShapeDtypeStruct((8, D), F32)],
        scratch_shapes=[pltpu.VMEM((tt, D), F32)],
        compiler_params=_cparams(2),
    )(dxo, x, g, *[p[0] for p in pairs], *[p[1] for p in pairs])


def wgrad_tn(a, b, tk, name):
    T, M = a.shape
    N = b.shape[1]
    tm = M if M * N * 4 <= WGRAD_ACC_BYTES else M // 2
    nt = T // tk

    def body(a_ref, b_ref, o_ref, acc_ref):
        k = pl.program_id(1)

        @pl.when(k == 0)
        def _():
            acc_ref[...] = jnp.zeros_like(acc_ref)

        acc_ref[...] += lax.dot_general(a_ref[...], b_ref[...], TN, preferred_element_type=F32)

        @pl.when(k == nt - 1)
        def _():
            o_ref[...] = acc_ref[...].astype(BF16)

    return _pcall(
        body,
        name=name,
        grid=(M // tm, nt),
        in_specs=[pl.BlockSpec((tk, tm), lambda i, k: (k, i)), pl.BlockSpec((tk, N), lambda i, k: (k, 0))],
        out_specs=pl.BlockSpec((tm, N), lambda i, k: (i, 0)),
        out_shape=jax.ShapeDtypeStruct((M, N), BF16),
        scratch_shapes=[pltpu.VMEM((tm, N), F32)],
        compiler_params=_cparams(2),
    )(a, b)


def _sgu_recompute(z_ref, lng_ref, lnb_ref, S):
    z = z_ref[...].astype(F32)
    zu, zv = z[:, :S], z[:, S:]
    tu, tv = _gelu_tanh(zu), _gelu_tanh(zv)
    u = 0.5 * zu * (1.0 + tu)
    v = 0.5 * zv * (1.0 + tv)
    vc = v - jnp.mean(v, axis=-1, keepdims=True)
    rstd = lax.rsqrt(jnp.mean(vc * vc, axis=-1, keepdims=True) + EPS)
    vhat = vc * rstd
    vln = (vhat * lng_ref[...] + lnb_ref[...]).astype(BF16)
    return zu, zv, tu, tv, u, vhat, rstd, vln


def sgu_mix_fwd(zpre, lng, lnb, ws, bias_full, tt, name):
    T, S2 = zpre.shape
    S = S2 // 2
    dg = S // GROUPS

    def body(z_ref, lng_ref, lnb_ref, ws_ref, bias_ref, o_ref):
        _, _, _, _, u, _, _, vln = _sgu_recompute(z_ref, lng_ref, lnb_ref, S)
        for c in range(tt // CHUNK):
            rows = slice(c * CHUNK, (c + 1) * CHUNK)
            for gi in range(GROUPS):
                cols = slice(gi * dg, (gi + 1) * dg)
                mixed = jnp.dot(ws_ref[gi], vln[rows, cols], preferred_element_type=F32) + bias_ref[:, cols]
                o_ref[rows, cols] = (u[rows, cols] * mixed).astype(BF16)

    vec = pl.BlockSpec((1, S), lambda i: (0, 0))
    return _pcall(
        body,
        name=name,
        grid=(T // tt,),
        in_specs=[
            pl.BlockSpec((tt, S2), lambda i: (i, 0)),
            vec,
            vec,
            pl.BlockSpec((GROUPS, CHUNK, CHUNK), lambda i: (0, 0, 0)),
            pl.BlockSpec((CHUNK, S), lambda i: (0, 0)),
        ],
        out_specs=pl.BlockSpec((tt, S), lambda i: (i, 0)),
        out_shape=jax.ShapeDtypeStruct((T, S), BF16),
        compiler_params=_cparams(1),
    )(zpre, lng, lnb, ws, bias_full)


def sgu_mix_bwd(dgated, zpre, lng, lnb, ws, ws_t, bias_full, tt, name):
    T, S2 = zpre.shape
    S = S2 // 2
    dg = S // GROUPS

    def body(dgt_ref, z_ref, lng_ref, lnb_ref, ws_ref, wst_ref, bias_ref, dz_ref, dws_ref, dbias_ref, dlng_ref, dlnb_ref, du_ref, dvln_ref):
        @pl.when(pl.program_id(0) == 0)
        def _():
            dws_ref[...] = jnp.zeros_like(dws_ref)
            dbias_ref[...] = jnp.zeros_like(dbias_ref)
            dlng_ref[...] = jnp.zeros_like(dlng_ref)
            dlnb_ref[...] = jnp.zeros_like(dlnb_ref)

        zu, zv, tu, tv, u, vhat, rstd, vln = _sgu_recompute(z_ref, lng_ref, lnb_ref, S)
        for c in range(tt // CHUNK):
            rows = slice(c * CHUNK, (c + 1) * CHUNK)
            for gi in range(GROUPS):
                cols = slice(gi * dg, (gi + 1) * dg)
                v_cg = vln[rows, cols]
                mixed = jnp.dot(ws_ref[gi], v_cg, preferred_element_type=F32) + bias_ref[:, cols]
                dgt = dgt_ref[rows, cols].astype(F32)
                du_ref[rows, cols] = dgt * mixed
                dmx = dgt * u[rows, cols]
                dbias_ref[:, cols] += dmx
                dmx16 = dmx.astype(BF16)
                dws_ref[gi] += lax.dot_general(dmx16, v_cg, NT, preferred_element_type=F32)
                dvln_ref[rows, cols] = jnp.dot(wst_ref[gi], dmx16, preferred_element_type=F32)

        dvln = dvln_ref[...]
        dlng_ref[...] += _sum8(dvln * vhat)
        dlnb_ref[...] += _sum8(dvln)
        dvhat = dvln * lng_ref[...]
        dv = rstd * (dvhat - jnp.mean(dvhat, axis=-1, keepdims=True) - vhat * jnp.mean(dvhat * vhat, axis=-1, keepdims=True))
        dz_ref[:, :S] = (du_ref[...] * _gelu_grad(zu, tu)).astype(BF16)
        dz_ref[:, S:] = (dv * _gelu_grad(zv, tv)).astype(BF16)

    vec = pl.BlockSpec((1, S), lambda i: (0, 0))
    wsp = pl.BlockSpec((GROUPS, CHUNK, CHUNK), lambda i: (0, 0, 0))
    full = pl.BlockSpec((CHUNK, S), lambda i: (0, 0))
    acc8 = pl.BlockSpec((8, S), lambda i: (0, 0))
    return _pcall(
        body,
        name=name,
        grid=(T // tt,),
        in_specs=[pl.BlockSpec((tt, S), lambda i: (i, 0)), pl.BlockSpec((tt, S2), lambda i: (i, 0)), vec, vec, wsp, wsp, full],
        out_specs=[pl.BlockSpec((tt, S2), lambda i: (i, 0)), wsp, full, acc8, acc8],
        out_shape=[
            jax.ShapeDtypeStruct((T, S2), BF16),
            jax.ShapeDtypeStruct((GROUPS, CHUNK, CHUNK), F32),
            jax.ShapeDtypeStruct((CHUNK, S), F32),
            jax.ShapeDtypeStruct((8, S), F32),
            jax.ShapeDtypeStruct((8, S), F32),
        ],
        scratch_shapes=[pltpu.VMEM((tt, S), F32), pltpu.VMEM((tt, S), F32)],
        compiler_params=_cparams(1),
    )(dgated, zpre, lng, lnb, ws, ws_t, bias_full)


def conv_mid_fwd(p, wdw, bdw, lng, lnb, tt, name):
    T, C2 = p.shape
    C = C2 // 2

    def body(p_ref, w_ref, b_ref, lng_ref, lnb_ref, yc_ref, ys_ref, ybuf):
        @pl.when(pl.program_id(0) == 0)
        def _():
            ybuf[0:HALO, :] = jnp.zeros((HALO, C), F32)

        @pl.when(pl.program_id(0) > 0)
        def _():
            ybuf[0:HALO, :] = ybuf[tt : tt + HALO, :]

        pv = p_ref[...].astype(F32)
        ybuf[HALO:, :] = pv[:, :C] * _sigmoid(pv[:, C:])
        acc = jnp.broadcast_to(b_ref[...], (tt, C))
        for k in range(CONV_W):
            acc = acc + w_ref[k : k + 1, :] * ybuf[pl.ds(HALO - (CONV_W - 1) + k, tt), :]
        yc_ref[...] = acc
        yc = acc - jnp.mean(acc, axis=-1, keepdims=True)
        yn = yc * lax.rsqrt(jnp.mean(yc * yc, axis=-1, keepdims=True) + EPS) * lng_ref[...] + lnb_ref[...]
        ys_ref[...] = (yn * _sigmoid(yn)).astype(BF16)

    vec = pl.BlockSpec((1, C), lambda i: (0, 0))
    row = pl.BlockSpec((tt, C), lambda i: (i, 0))
    return _pcall(
        body,
        name=name,
        grid=(T // tt,),
        in_specs=[pl.BlockSpec((tt, C2), lambda i: (i, 0)), pl.BlockSpec((HALO, C), lambda i: (0, 0)), vec, vec, vec],
        out_specs=[row, row],
        out_shape=[jax.ShapeDtypeStruct((T, C), F32), jax.ShapeDtypeStruct((T, C), BF16)],
        scratch_shapes=[pltpu.VMEM((tt + HALO, C), F32)],
        compiler_params=_cparams(1),
    )(p, wdw, bdw, lng, lnb)


def conv_mid_bwd(dys, yc, p, wdw, lng, lnb, tt, name):
    T, C2 = p.shape
    C = C2 // 2
    n = T // tt

    def body(dys_ref, yc_ref, p_ref, w_ref, lng_ref, lnb_ref, dp_ref, dw_ref, db_ref, dlng_ref, dlnb_ref, dbuf):
        @pl.when(pl.program_id(0) == 0)
        def _():
            dbuf[tt : tt + HALO, :] = jnp.zeros((HALO, C), F32)
            dw_ref[...] = jnp.zeros_like(dw_ref)
            db_ref[...] = jnp.zeros_like(db_ref)
            dlng_ref[...] = jnp.zeros_like(dlng_ref)
            dlnb_ref[...] = jnp.zeros_like(dlnb_ref)

        @pl.when(pl.program_id(0) > 0)
        def _():
            dbuf[tt : tt + HALO, :] = dbuf[0:HALO, :]

        ycv = yc_ref[...]
        ycc = ycv - jnp.mean(ycv, axis=-1, keepdims=True)
        rstd = lax.rsqrt(jnp.mean(ycc * ycc, axis=-1, keepdims=True) + EPS)
        yhat = ycc * rstd
        lng_v = lng_ref[...]
        yn = yhat * lng_v + lnb_ref[...]
        sg = _sigmoid(yn)
        dyn = dys_ref[...].astype(F32) * (sg * (1.0 + yn * (1.0 - sg)))
        dlng_ref[...] += _sum8(dyn * yhat)
        dlnb_ref[...] += _sum8(dyn)
        dyh = dyn * lng_v
        dyc = rstd * (dyh - jnp.mean(dyh, axis=-1, keepdims=True) - yhat * jnp.mean(dyh * yhat, axis=-1, keepdims=True))
        db_ref[...] += _sum8(dyc)
        dbuf[0:tt, :] = dyc

        pv = p_ref[...].astype(F32)
        a = pv[:, :C]
        sgate = _sigmoid(pv[:, C:])
        y = a * sgate
        dy = jnp.zeros((tt, C), F32)
        for k in range(CONV_W):
            shifted = dbuf[pl.ds(CONV_W - 1 - k, tt), :]
            dy = dy + w_ref[k : k + 1, :] * shifted
            dw_ref[8 * k : 8 * k + 8, :] += _sum8(y * shifted)
        dp_ref[:, :C] = (dy * sgate).astype(BF16)
        dp_ref[:, C:] = (dy * a * sgate * (1.0 - sgate)).astype(BF16)

    vec = pl.BlockSpec((1, C), lambda i: (0, 0))
    row = pl.BlockSpec((tt, C), lambda i: (n - 1 - i, 0))
    row2 = pl.BlockSpec((tt, C2), lambda i: (n - 1 - i, 0))
    acc8 = pl.BlockSpec((8, C), lambda i: (0, 0))
    return _pcall(
        body,
        name=name,
        grid=(n,),
        in_specs=[row, row, row2, pl.BlockSpec((HALO, C), lambda i: (0, 0)), vec, vec],
        out_specs=[row2, pl.BlockSpec((HALO * 8, C), lambda i: (0, 0)), acc8, acc8, acc8],
        out_shape=[
            jax.ShapeDtypeStruct((T, C2), BF16),
            jax.ShapeDtypeStruct((HALO * 8, C), F32),
            jax.ShapeDtypeStruct((8, C), F32),
            jax.ShapeDtypeStruct((8, C), F32),
            jax.ShapeDtypeStruct((8, C), F32),
        ],
        scratch_shapes=[pltpu.VMEM((tt + HALO, C), F32)],
        compiler_params=_cparams(1),
    )(dys, yc, p, wdw, lng, lnb)


def loss_head(y, target, tt, name):
    T, D = y.shape

    def body(y_ref, t_ref, dy_ref, sq_ref):
        @pl.when(pl.program_id(0) == 0)
        def _():
            sq_ref[...] = jnp.zeros_like(sq_ref)

        err = y_ref[...] - t_ref[...]
        dy_ref[...] = err * (1.0 / D)
        sq_ref[...] += _sum8(err * err)

    row = pl.BlockSpec((tt, D), lambda i: (i, 0))
    return _pcall(
        body,
        name=name,
        grid=(T // tt,),
        in_specs=[row, row],
        out_specs=[row, pl.BlockSpec((8, D), lambda i: (0, 0))],
        out_shape=[jax.ShapeDtypeStruct((T, D), F32), jax.ShapeDtypeStruct((8, D), F32)],
        compiler_params=_cparams(1),
    )(y, target)


def adamw(w, g, m, v, name):
    R, C = w.shape
    tr = _tile(R, 512, 8)
    c1 = 1.0 - ADAM_B1**ADAM_STEP
    c2 = 1.0 - ADAM_B2**ADAM_STEP

    def body(w_ref, g_ref, m_ref, v_ref, d_ref, mo_ref, vo_ref):
        gv = g_ref[...]
        m2 = ADAM_B1 * m_ref[...] + (1.0 - ADAM_B1) * gv
        v2 = ADAM_B2 * v_ref[...] + (1.0 - ADAM_B2) * (gv * gv)
        mo_ref[...] = m2
        vo_ref[...] = v2
        d_ref[...] = -ADAM_LR * ((m2 / c1) / (jnp.sqrt(v2 / c2) + ADAM_EPS) + ADAM_WD * w_ref[...])

    blk = pl.BlockSpec((tr, C), lambda i: (i, 0))
    return _pcall(
        body,
        name=name,
        grid=(R // tr,),
        in_specs=[blk] * 4,
        out_specs=[blk] * 3,
        out_shape=[jax.ShapeDtypeStruct((R, C), F32)] * 3,
        compiler_params=_cparams(1),
    )(w, g, m, v)


HBM_SPEC = pl.BlockSpec(memory_space=pl.ANY)
MESH_ID = pl.DeviceIdType.MESH


def _place():
    x, y, c = (lax.axis_index(a) for a in MESH_AXES)
    chips = [(1 - x, y), (x, 1 - y), (1 - x, 1 - y)]
    return x, y, c, chips


def all_gather_blocks(packs, name):
    n = len(packs)

    def body(*refs):
        in_refs, out_refs = refs[:n], refs[n : 2 * n]
        send_sems, recv_sems, local_sems = refs[2 * n :]
        x, y, c, chips = _place()
        me, sibling = (x, y, c), (x, y, 1 - c)

        def slot(a, px, py, pc):
            return out_refs[a].at[4 * px + 2 * py + pc]

        def copy(a, k, block, to, from_input=False):
            return pltpu.make_async_remote_copy(
                src_ref=in_refs[a] if from_input else slot(a, *block),
                dst_ref=slot(a, *block),
                send_sem=send_sems.at[a, k],
                recv_sem=recv_sems.at[a, k],
                device_id=to,
                device_id_type=MESH_ID,
            )

        mine = [pltpu.make_async_copy(in_refs[a], slot(a, *me), local_sems.at[a]) for a in range(n)]
        for cp in mine:
            cp.start()
        first = []
        for a in range(n):
            first.append(copy(a, 0, me, sibling, from_input=True))
            first += [copy(a, 1 + j, me, (*chip, c), from_input=True) for j, chip in enumerate(chips)]
        for cp in first:
            cp.start()
        passed = []
        for j, chip in enumerate(chips):
            for a in range(n):
                copy(a, 1 + j, (*chip, c), me).wait_recv()
                fwd = copy(a, 4 + j, (*chip, c), sibling)
                fwd.start()
                passed.append(fwd)
        for a in range(n):
            copy(a, 0, sibling, me).wait_recv()
            for j, chip in enumerate(chips):
                copy(a, 4 + j, (*chip, 1 - c), me).wait_recv()
        for cp in first + passed:
            cp.wait_send()
        for cp in mine:
            cp.wait()

    return _pcall(
        body,
        name=name,
        in_specs=[HBM_SPEC] * n,
        out_specs=[HBM_SPEC] * n,
        out_shape=[jax.ShapeDtypeStruct((N_DEV,) + p.shape, p.dtype) for p in packs],
        scratch_shapes=[pltpu.SemaphoreType.DMA((n, 7)), pltpu.SemaphoreType.DMA((n, 7)), pltpu.SemaphoreType.DMA((n,))],
    )(*packs)


def exchange_with_sibling(dpack, small, name):
    _, R, W = dpack.shape

    def body(dpack_ref, small_ref, got_ref, got_small_ref, send_sems, recv_sems):
        x, y, c, _ = _place()
        sibling = (x, y, 1 - c)
        copies = [
            pltpu.make_async_remote_copy(
                src_ref=dpack_ref.at[2 * p + (1 - c)],
                dst_ref=got_ref.at[p],
                send_sem=send_sems.at[p],
                recv_sem=recv_sems.at[p],
                device_id=sibling,
                device_id_type=MESH_ID,
            )
            for p in range(4)
        ]
        copies.append(
            pltpu.make_async_remote_copy(
                src_ref=small_ref, dst_ref=got_small_ref, send_sem=send_sems.at[4], recv_sem=recv_sems.at[4], device_id=sibling, device_id_type=MESH_ID
            )
        )
        for cp in copies:
            cp.start()
        for cp in copies:
            cp.wait()

    return _pcall(
        body,
        name=name,
        in_specs=[HBM_SPEC] * 2,
        out_specs=[HBM_SPEC] * 2,
        out_shape=[jax.ShapeDtypeStruct((4, R, W), dpack.dtype), jax.ShapeDtypeStruct(small.shape, small.dtype)],
        scratch_shapes=[pltpu.SemaphoreType.DMA((5,)), pltpu.SemaphoreType.DMA((5,))],
    )(dpack, small)


def exchange_with_chips(part, small, name):
    _, R, W = part.shape

    def body(part_ref, small_ref, got_ref, got_small_ref, send_sems, recv_sems):
        _, _, c, chips = _place()
        copies = []
        for j, (px, py) in enumerate(chips):
            copies.append(
                pltpu.make_async_remote_copy(
                    src_ref=part_ref.at[2 * px + py],
                    dst_ref=got_ref.at[j],
                    send_sem=send_sems.at[j],
                    recv_sem=recv_sems.at[j],
                    device_id=(px, py, c),
                    device_id_type=MESH_ID,
                )
            )
            copies.append(
                pltpu.make_async_remote_copy(
                    src_ref=small_ref,
                    dst_ref=got_small_ref.at[j],
                    send_sem=send_sems.at[3 + j],
                    recv_sem=recv_sems.at[3 + j],
                    device_id=(px, py, c),
                    device_id_type=MESH_ID,
                )
            )
        for cp in copies:
            cp.start()
        for cp in copies:
            cp.wait()

    return _pcall(
        body,
        name=name,
        in_specs=[HBM_SPEC] * 2,
        out_specs=[HBM_SPEC] * 2,
        out_shape=[jax.ShapeDtypeStruct((3, R, W), part.dtype), jax.ShapeDtypeStruct((3,) + small.shape, small.dtype)],
        scratch_shapes=[pltpu.SemaphoreType.DMA((6,)), pltpu.SemaphoreType.DMA((6,))],
    )(part, small)


def add_sibling_blocks(place, dpack, got, name):
    _, R, W = dpack.shape
    tr = _tile(R, 512, 16)
    dpack4 = dpack.reshape(4, 2, R, W)

    def body(place_ref, mine_ref, got_ref, o_ref):
        o_ref[...] = (mine_ref[...].astype(F32) + got_ref[...].astype(F32)).astype(o_ref.dtype)

    blk = pl.BlockSpec((None, tr, W), lambda p, i, place: (p, i, 0))
    return _pcall(
        body,
        name=name,
        grid_spec=pltpu.PrefetchScalarGridSpec(
            num_scalar_prefetch=1,
            grid=(4, R // tr),
            in_specs=[pl.BlockSpec((None, None, tr, W), lambda p, i, place: (p, place[0], i, 0)), blk],
            out_specs=blk,
        ),
        out_shape=jax.ShapeDtypeStruct((4, R, W), dpack.dtype),
        compiler_params=_cparams(2),
    )(place, dpack4, got)


def add_chip_blocks(place, part, got, name):
    _, R, W = part.shape
    tr = _tile(R, 512, 16)

    def body(place_ref, mine_ref, g0_ref, g1_ref, g2_ref, o_ref):
        o_ref[...] = ((mine_ref[...].astype(F32) + g0_ref[...].astype(F32)) + g1_ref[...].astype(F32)) + g2_ref[...].astype(F32)

    def got_blk(j):
        return pl.BlockSpec((None, tr, W), lambda i, place: (j, i, 0))

    return _pcall(
        body,
        name=name,
        grid_spec=pltpu.PrefetchScalarGridSpec(
            num_scalar_prefetch=1,
            grid=(R // tr,),
            in_specs=[pl.BlockSpec((None, tr, W), lambda i, place: (place[1], i, 0)), got_blk(0), got_blk(1), got_blk(2)],
            out_specs=pl.BlockSpec((tr, W), lambda i, place: (i, 0)),
        ),
        out_shape=jax.ShapeDtypeStruct((R, W), F32),
        compiler_params=_cparams(1),
    )(place, part, got, got, got)


def add_small(a, b, name):
    def body(a_ref, b_ref, o_ref):
        o_ref[...] = a_ref[...] + b_ref[...]

    return _pcall(body, name=name, out_shape=jax.ShapeDtypeStruct(a.shape, a.dtype))(a, b)


def add_small_chips(place, mine, got, name):
    def body(place_ref, mine_ref, got_ref, o_ref):
        p_me = place_ref[1]
        by_mask = {0: mine_ref[...], 2: got_ref[0], 1: got_ref[1], 3: got_ref[2]}
        total = None
        for q in range(4):
            mask = jnp.bitwise_xor(p_me, q)
            term = jnp.where(mask == 0, by_mask[0], jnp.where(mask == 1, by_mask[1], jnp.where(mask == 2, by_mask[2], by_mask[3])))
            total = term if total is None else total + term
        o_ref[...] = total

    return _pcall(
        body,
        name=name,
        in_specs=[pl.BlockSpec(memory_space=pltpu.SMEM), pl.BlockSpec(memory_space=pltpu.VMEM), pl.BlockSpec(memory_space=pltpu.VMEM)],
        out_specs=pl.BlockSpec(memory_space=pltpu.VMEM),
        out_shape=jax.ShapeDtypeStruct(mine.shape, mine.dtype),
        compiler_params=pltpu.CompilerParams(vmem_limit_bytes=VMEM_LIMIT_V7X),
    )(place, mine, got)


def _sublayers(depth):
    out = []
    for layer in range(depth):
        out.append(("ffn", layer, 0))
        out.append(("sgu" if layer % N_MIXERS == 0 else "conv", layer, layer // N_MIXERS))
        out.append(("ffn", layer, 1))
    return out


def _row_blocks(sub, shards):
    kind, layer, idx = sub
    if kind == "ffn":
        return {"gate_t": shards["ff_w_gate"][layer, idx].T, "up_t": shards["ff_w_up"][layer, idx].T, "down": shards["ff_w_down"][layer, idx]}
    if kind == "sgu":
        return {"in_t": shards["sgu_w_in"][idx].T, "out": shards["sgu_w_out"][idx]}
    return {"pw1_t": shards["conv_w_pw1"][idx].T, "pw2": shards["conv_w_pw2"][idx]}


def kernel(x, norm_g, ff_w_gate, ff_w_up, ff_w_down, sgu_w_in, sgu_ln_g, sgu_ln_b, sgu_w_spatial, sgu_b_spatial, sgu_w_out, conv_w_pw1, conv_w_dw, conv_b_dw, conv_ln_g, conv_ln_b, conv_w_pw2, loss_target, m_norm_g, m_ff_w_gate, m_ff_w_up, m_ff_w_down, m_sgu_w_in, m_sgu_ln_g, m_sgu_ln_b, m_sgu_w_spatial, m_sgu_b_spatial, m_sgu_w_out, m_conv_w_pw1, m_conv_w_dw, m_conv_b_dw, m_conv_ln_g, m_conv_ln_b, m_conv_w_pw2, v_norm_g, v_ff_w_gate, v_ff_w_up, v_ff_w_down, v_sgu_w_in, v_sgu_ln_g, v_sgu_ln_b, v_sgu_w_spatial, v_sgu_b_spatial, v_sgu_w_out, v_conv_w_pw1, v_conv_w_dw, v_conv_b_dw, v_conv_ln_g, v_conv_ln_b, v_conv_w_pw2):
    names = ["norm_g", "ff_w_gate", "ff_w_up", "ff_w_down", "sgu_w_in", "sgu_ln_g", "sgu_ln_b", "sgu_w_spatial", "sgu_b_spatial", "sgu_w_out", "conv_w_pw1", "conv_w_dw", "conv_b_dw", "conv_ln_g", "conv_ln_b", "conv_w_pw2"]
    weights = dict(zip(names, [norm_g, ff_w_gate, ff_w_up, ff_w_down, sgu_w_in, sgu_ln_g, sgu_ln_b, sgu_w_spatial, sgu_b_spatial, sgu_w_out, conv_w_pw1, conv_w_dw, conv_b_dw, conv_ln_g, conv_ln_b, conv_w_pw2]))
    moments_m = dict(zip(names, [m_norm_g, m_ff_w_gate, m_ff_w_up, m_ff_w_down, m_sgu_w_in, m_sgu_ln_g, m_sgu_ln_b, m_sgu_w_spatial, m_sgu_b_spatial, m_sgu_w_out, m_conv_w_pw1, m_conv_w_dw, m_conv_b_dw, m_conv_ln_g, m_conv_ln_b, m_conv_w_pw2]))
    moments_v = dict(zip(names, [v_norm_g, v_ff_w_gate, v_ff_w_up, v_ff_w_down, v_sgu_w_in, v_sgu_ln_g, v_sgu_ln_b, v_sgu_w_spatial, v_sgu_b_spatial, v_sgu_w_out, v_conv_w_pw1, v_conv_w_dw, v_conv_b_dw, v_conv_ln_g, v_conv_ln_b, v_conv_w_pw2]))

    _, T, D = x.shape
    depth = norm_g.shape[0]
    n_conv = conv_w_dw.shape[0]
    n_sgu = sgu_w_in.shape[0]
    S = sgu_ln_g.shape[1]
    lanes = norm_g.shape[2]
    subs = _sublayers(depth)

    cx, cy, cc = (lax.axis_index(a) for a in MESH_AXES)
    place = jnp.stack([cc, 2 * cx + cy]).astype(jnp.int32)
    my_block = 4 * cx + 2 * cy + cc

    blocks = [_row_blocks(sub, weights) for sub in subs]
    layout = []
    off = 0
    for si, blk in enumerate(blocks):
        for nm, arr in blk.items():
            layout.append((si, nm, off, arr.shape[0]))
            off += arr.shape[0]
    pack = jnp.concatenate([arr.astype(BF16) for blk in blocks for arr in blk.values()], axis=0)
    R = pack.shape[0]

    dw_pad = jnp.pad(conv_w_dw, ((0, 0), (0, HALO - CONV_W), (0, 0)))
    small_parts = [norm_g.reshape(-1, lanes), dw_pad.reshape(-1, lanes), conv_b_dw, conv_ln_g, conv_ln_b]
    small_rows = [p.shape[0] for p in small_parts]
    small = jnp.concatenate(small_parts, axis=0)
    small = jnp.pad(small, ((0, (-small.shape[0]) % 8), (0, 0)))

    gpack, gsmall = all_gather_blocks([pack, small], name="all_gather_weights")

    W = [dict() for _ in subs]
    for si, nm, o, r in layout:
        W[si][nm] = gpack[:, o : o + r, :].reshape(N_DEV * r, D)
    small_full = jnp.transpose(gsmall, (1, 0, 2)).reshape(gsmall.shape[1], N_DEV * lanes)
    so = [0]
    for r in small_rows:
        so.append(so[-1] + r)
    norm_full = small_full[so[0] : so[1]].reshape(depth, -1, D)
    dw_full = small_full[so[1] : so[2]].reshape(n_conv, HALO, D)
    bdw_full, clng_full, clnb_full = (small_full[so[k] : so[k + 1]] for k in (2, 3, 4))

    causal = jnp.tril(jnp.ones((CHUNK, CHUNK), dtype=bool))
    ws_all = jnp.where(causal[None, None], sgu_w_spatial, 0.0).astype(BF16)
    wst_all = jnp.swapaxes(ws_all, -1, -2)
    bias_full_all = jnp.repeat(jnp.swapaxes(sgu_b_spatial, -1, -2), S // GROUPS, axis=-1)

    tt = _tile(T, 512, CHUNK)
    tt_mix = _tile(T, 256, CHUNK)
    F = W[0]["down"].shape[0]
    fc = _tile(F, 1408, 128)

    def vec(v):
        return v.reshape(1, -1)

    xs = x[0]
    saved = []
    for si, (kind, layer, idx) in enumerate(subs):
        w = W[si]
        if kind == "ffn":
            g_pre, g_post = vec(norm_full[layer, 4 * idx]), vec(norm_full[layer, 4 * idx + 1])
            gate, up = rms_matmul_nt(xs, g_pre, [w["gate_t"], w["up_t"]], tt, fc, name="ffn_in")
            x_new, o, a = matmul_rms_res(xs, (gate, up), w["down"], g_post, FFN_SCALE, tt, fc, name="ffn_out")
            saved.append((xs, gate, up, o, a))
        elif kind == "sgu":
            g_pre, g_post = vec(norm_full[layer, 2]), vec(norm_full[layer, 3])
            (zpre,) = rms_matmul_nt(xs, g_pre, [w["in_t"]], tt, _tile(w["in_t"].shape[0], 512, 128), name="sgu_in")
            gated = sgu_mix_fwd(zpre, vec(sgu_ln_g[idx]), vec(sgu_ln_b[idx]), ws_all[idx], bias_full_all[idx], tt_mix, name="sgu_mix")
            x_new, mm = matmul_rms_res(xs, (gated,), w["out"], g_post, 1.0, tt, _tile(w["out"].shape[0], 512, 128), name="sgu_out")
            saved.append((xs, zpre, gated, mm))
        else:
            g_pre, g_post = vec(norm_full[layer, 2]), vec(norm_full[layer, 3])
            (p,) = rms_matmul_nt(xs, g_pre, [w["pw1_t"]], tt, _tile(w["pw1_t"].shape[0], 512, 128), name="conv_pw1")
            yc, ys = conv_mid_fwd(p, dw_full[idx], vec(bdw_full[idx]), vec(clng_full[idx]), vec(clnb_full[idx]), tt, name="conv_mid")
            x_new, mm = matmul_rms_res(xs, (ys,), w["pw2"], g_post, 1.0, tt, _tile(w["pw2"].shape[0], 512, 128), name="conv_pw2")
            saved.append((xs, p, yc, ys, mm))
        xs = x_new

    dx, sq = loss_head(xs, loss_target[0], tt, name="loss_head")
    loss = lax.psum(0.5 * jnp.sum(sq) / D, MESH_AXES)

    dW = [dict() for _ in subs]
    d_norm = [[None] * norm_full.shape[1] for _ in range(depth)]
    d_sgu = [None] * n_sgu
    d_conv = [None] * n_conv
    for si in reversed(range(len(subs))):
        kind, layer, idx = subs[si]
        w = W[si]
        if kind == "ffn":
            xs_in, gate, up, o, a = saved[si]
            g_pre, g_post = vec(norm_full[layer, 4 * idx]), vec(norm_full[layer, 4 * idx + 1])
            do, dg_post, dgate, dup = rms_bwd_matmul_nt(dx, o, g_post, w["down"], FFN_SCALE, (gate, up), tt, fc, name="ffn_out_bwd")
            dx, h, dg_pre = matmul_rms_bwd(dx, xs_in, g_pre, [(dgate, w["gate_t"]), (dup, w["up_t"])], tt, fc, name="ffn_in_bwd")
            dW[si] = {"gate_t": wgrad_tn(dgate, h, tt, name="wgrad_ffn_in"), "up_t": wgrad_tn(dup, h, tt, name="wgrad_ffn_in"), "down": wgrad_tn(a, do, tt, name="wgrad_ffn_out")}
            d_norm[layer][4 * idx], d_norm[layer][4 * idx + 1] = dg_pre, dg_post
        elif kind == "sgu":
            xs_in, zpre, gated, mm = saved[si]
            g_pre, g_post = vec(norm_full[layer, 2]), vec(norm_full[layer, 3])
            dm, dg_post, dgated = rms_bwd_matmul_nt(dx, mm, g_post, w["out"], 1.0, None, tt, _tile(w["out"].shape[0], 512, 128), name="sgu_out_bwd")
            dzpre, dws, dbias, dlng, dlnb = sgu_mix_bwd(dgated, zpre, vec(sgu_ln_g[idx]), vec(sgu_ln_b[idx]), ws_all[idx], wst_all[idx], bias_full_all[idx], tt_mix, name="sgu_mix_bwd")
            dx, h, dg_pre = matmul_rms_bwd(dx, xs_in, g_pre, [(dzpre, w["in_t"])], tt, _tile(w["in_t"].shape[0], 512, 128), name="sgu_in_bwd")
            dW[si] = {"in_t": wgrad_tn(dzpre, h, tt, name="wgrad_sgu_in"), "out": wgrad_tn(gated, dm, tt, name="wgrad_sgu_out")}
            d_norm[layer][2], d_norm[layer][3] = dg_pre, dg_post
            d_sgu[idx] = (dws, dbias, dlng, dlnb)
        else:
            xs_in, p, yc, ys, mm = saved[si]
            g_pre, g_post = vec(norm_full[layer, 2]), vec(norm_full[layer, 3])
            dm, dg_post, dys = rms_bwd_matmul_nt(dx, mm, g_post, w["pw2"], 1.0, None, tt, _tile(w["pw2"].shape[0], 512, 128), name="conv_pw2_bwd")
            dp, dwdw, dbdw, dlng, dlnb = conv_mid_bwd(dys, yc, p, dw_full[idx], vec(clng_full[idx]), vec(clnb_full[idx]), tt, name="conv_mid_bwd")
            dx, h, dg_pre = matmul_rms_bwd(dx, xs_in, g_pre, [(dp, w["pw1_t"])], tt, _tile(w["pw1_t"].shape[0], 512, 128), name="conv_pw1_bwd")
            dW[si] = {"pw1_t": wgrad_tn(dp, h, tt, name="wgrad_conv_pw1"), "pw2": wgrad_tn(ys, dm, tt, name="wgrad_conv_pw2")}
            d_norm[layer][2], d_norm[layer][3] = dg_pre, dg_post
            d_conv[idx] = (dwdw, dbdw, dlng, dlnb)
    grad_x = dx[None]

    dpack = jnp.concatenate([dW[si][nm].reshape(N_DEV, r, D) for si, nm, o, r in layout], axis=1)

    def sum8(v):
        return jnp.sum(v, axis=0)

    g_norm = jnp.stack([jnp.stack([sum8(d) for d in row]) for row in d_norm])
    g_dw = jnp.stack([jnp.sum(d[0].reshape(HALO, 8, D), axis=1) for d in d_conv])
    g_bdw, g_clng, g_clnb = (jnp.stack([sum8(d[k]) for d in d_conv]) for k in (1, 2, 3))
    g_slng, g_slnb = (jnp.stack([sum8(d[k]) for d in d_sgu]) for k in (2, 3))
    g_bsp = jnp.stack([jnp.sum(d[1].reshape(CHUNK, GROUPS, S // GROUPS), axis=-1).T for d in d_sgu])
    g_wsp = jnp.stack([jnp.where(causal[None], d[0], 0.0) for d in d_sgu])
    sparts = [g_norm, g_dw, g_bdw, g_clng, g_clnb, g_slng, g_slnb, g_bsp, g_wsp]
    srows = [p.size // D for p in sparts]
    sgrad = jnp.concatenate([p.reshape(-1, D) for p in sparts], axis=0)
    sgrad = jnp.pad(sgrad, ((0, (-sgrad.shape[0]) % 8), (0, 0)))

    got_a, got_small_a = exchange_with_sibling(dpack, sgrad, name="grads_to_sibling")
    part = add_sibling_blocks(place, dpack, got_a, name="add_sibling_blocks")
    spart = add_small(sgrad, got_small_a, name="add_sibling_small")
    got_b, got_small_b = exchange_with_chips(part, spart, name="grads_to_chips")
    gshard = add_chip_blocks(place, part, got_b, name="add_chip_blocks")
    stotal = add_small_chips(place, spart, got_small_b, name="add_chip_small")

    gs = {}
    for si, nm, o, r in layout:
        gs[(si, nm)] = gshard[o : o + r]
    grads = {}
    ffn_si = {(layer, idx): si for si, (kind, layer, idx) in enumerate(subs) if kind == "ffn"}
    sgu_si = {idx: si for si, (kind, layer, idx) in enumerate(subs) if kind == "sgu"}
    conv_si = {idx: si for si, (kind, layer, idx) in enumerate(subs) if kind == "conv"}
    grads["ff_w_gate"] = jnp.stack([jnp.stack([gs[(ffn_si[(l, f)], "gate_t")].T for f in range(2)]) for l in range(depth)])
    grads["ff_w_up"] = jnp.stack([jnp.stack([gs[(ffn_si[(l, f)], "up_t")].T for f in range(2)]) for l in range(depth)])
    grads["ff_w_down"] = jnp.stack([jnp.stack([gs[(ffn_si[(l, f)], "down")] for f in range(2)]) for l in range(depth)])
    grads["sgu_w_in"] = jnp.stack([gs[(sgu_si[j], "in_t")].T for j in range(n_sgu)])
    grads["sgu_w_out"] = jnp.stack([gs[(sgu_si[j], "out")] for j in range(n_sgu)])
    grads["conv_w_pw1"] = jnp.stack([gs[(conv_si[j], "pw1_t")].T for j in range(n_conv)])
    grads["conv_w_pw2"] = jnp.stack([gs[(conv_si[j], "pw2")] for j in range(n_conv)])

    so = [0]
    for r in srows:
        so.append(so[-1] + r)
    sp = [stotal[so[k] : so[k + 1]] for k in range(len(sparts))]

    def my_lanes(v):
        return lax.dynamic_slice_in_dim(v, my_block * lanes, lanes, axis=-1)

    grads["norm_g"] = my_lanes(sp[0].reshape(g_norm.shape))
    grads["conv_w_dw"] = my_lanes(sp[1].reshape(g_dw.shape))[:, :CONV_W]
    grads["conv_b_dw"] = my_lanes(sp[2])
    grads["conv_ln_g"] = my_lanes(sp[3])
    grads["conv_ln_b"] = my_lanes(sp[4])
    grads["sgu_ln_g"] = sp[5].reshape(g_slng.shape)
    grads["sgu_ln_b"] = sp[6].reshape(g_slnb.shape)
    grads["sgu_b_spatial"] = sp[7].reshape(g_bsp.shape)
    grads["sgu_w_spatial"] = sp[8].reshape(g_wsp.shape)

    deltas, new_m, new_v = {}, {}, {}
    for nm in names:
        w = weights[nm]
        two_d = (-1, w.shape[-1])
        d, m2, v2 = adamw(w.reshape(two_d), grads[nm].reshape(two_d), moments_m[nm].reshape(two_d), moments_v[nm].reshape(two_d), name="adamw")
        deltas[nm], new_m[nm], new_v[nm] = d.reshape(w.shape), m2.reshape(w.shape), v2.reshape(w.shape)

    return (loss, grad_x, *[grads[n] for n in names], *[deltas[n] for n in names], *[new_m[n] for n in names], *[new_v[n] for n in names])
```

```python
import jax
import jax.numpy as jnp
from jax import lax
from jax.experimental import pallas as pl
from jax.experimental.pallas import tpu as pltpu

F32 = jnp.float32
BF16 = jnp.bfloat16

EPS = 1e-6
FFN_SCALE = 0.5
N_MIXERS = 2
CHUNK = 128
GROUPS = 8
CONV_W = 31
HALO = 32
GELU_C0 = 0.7978845608028654
GELU_C1 = 0.044715
ADAM_LR, ADAM_B1, ADAM_B2, ADAM_EPS, ADAM_WD, ADAM_STEP = 0.001, 0.9, 0.999, 1e-08, 0.01, 10

MESH_AXES = ("x", "y", "c")
N_DEV = 8
VMEM_LIMIT_V7X = 56 * 1024 * 1024
WGRAD_ACC_BYTES = 16 * 1024 * 1024
MXU_COLS_V7X = 256

NT = (((1,), (1,)), ((), ()))
TN = (((0,), (0,)), ((), ()))

HBM_SPEC = pl.BlockSpec(memory_space=pl.ANY)
MESH_ID = pl.DeviceIdType.MESH


def _pcall(body, **kw):
    return pl.pallas_call(body, **kw)


def _cparams(n_axes):
    return pltpu.CompilerParams(dimension_semantics=("arbitrary",) * n_axes, vmem_limit_bytes=VMEM_LIMIT_V7X)


def _tile(n, pref, align):
    if n <= pref:
        return n
    t = (pref // align) * align
    while t > align and n % t:
        t -= align
    assert n % t == 0, (n, pref, align)
    return t


def _rstd(x):
    return lax.rsqrt(jnp.mean(x * x, axis=-1, keepdims=True) + EPS)


def _sum8(v):
    t, d = v.shape
    return v.reshape(t // 8, 8, d).sum(axis=0)


def _sigmoid(x):
    return jax.nn.sigmoid(x)


def _gelu_tanh(z):
    return jnp.tanh(GELU_C0 * (z + GELU_C1 * z * z * z))


def _gelu_grad(z, t):
    return 0.5 * (1.0 + t) + 0.5 * z * (1.0 - t * t) * (GELU_C0 * (1.0 + 3.0 * GELU_C1 * z * z))


def _resident(shape):
    return pl.BlockSpec(shape, lambda i: (0,) * len(shape), pipeline_mode=pl.Buffered(1))


def _exchange_io(comm):
    n = len(comm)
    out_shape = [jax.ShapeDtypeStruct(((N_DEV,) + a.shape) if kind == "gather" else a.shape, a.dtype) for kind, a in comm]
    scratch = [pltpu.SemaphoreType.DMA((n, N_DEV - 1)), pltpu.SemaphoreType.DMA((n, N_DEV - 1)), pltpu.SemaphoreType.DMA((n,))]
    return [HBM_SPEC] * n, [HBM_SPEC] * n, out_shape, scratch


def _exchange_copies(kinds, in_refs, out_refs, send_sems, recv_sems, local_sems, with_arrivals=True):
    x, y, c = (lax.axis_index(a) for a in MESH_AXES)
    me = 4 * x + 2 * y + c
    local, sends, arrivals = [], [], []
    for a, kind in enumerate(kinds):
        src, dst = in_refs[a], out_refs[a]
        gather = kind == "gather"
        local.append(pltpu.make_async_copy(src if gather else src.at[me], dst.at[me], local_sems.at[a]))
        for k in range(N_DEV - 1):
            mask = k + 1
            px = 1 - x if mask & 4 else x
            py = 1 - y if mask & 2 else y
            pc = 1 - c if mask & 1 else c
            peer = 4 * px + 2 * py + pc
            block = src if gather else src.at[peer]
            for into, lst in ((me, sends), (peer, arrivals)):
                if lst is arrivals and not with_arrivals:
                    continue
                lst.append(
                    pltpu.make_async_remote_copy(
                        src_ref=block, dst_ref=dst.at[into], send_sem=send_sems.at[a, k], recv_sem=recv_sems.at[a, k], device_id=(px, py, pc), device_id_type=MESH_ID
                    )
                )
    return local, sends, arrivals


def _exchange_start(copies):
    local, sends, _ = copies
    for cp in local + sends:
        cp.start()


def _exchange_finish(copies):
    local, sends, arrivals = copies
    for cp in arrivals:
        cp.wait_recv()
    for cp in sends:
        cp.wait_send()
    for cp in local:
        cp.wait()


def exchange(comm, name):
    kinds = [k for k, _ in comm]
    n = len(comm)
    in_specs, out_specs, out_shape, scratch = _exchange_io(comm)

    def body(*refs):
        copies = _exchange_copies(kinds, refs[:n], refs[n : 2 * n], *refs[2 * n :])
        _exchange_start(copies)
        _exchange_finish(copies)

    return _pcall(body, name=name, in_specs=in_specs, out_specs=out_specs, out_shape=out_shape, scratch_shapes=scratch)(*[a for _, a in comm])


def _token_call(body, name, n_tiles, in_specs, inputs, out_specs, out_shape, scratch, comm):
    n_in, n_out, n_scr = len(inputs), len(out_shape), len(scratch)
    comm = comm or []
    nc = len(comm)
    full_body = body
    if nc:
        kinds = [k for k, _ in comm]
        c_in, c_out, c_shape, c_scr = _exchange_io(comm)
        in_specs, out_specs, out_shape, scratch = in_specs + c_in, out_specs + c_out, out_shape + c_shape, scratch + c_scr

        def full_body(*refs):
            ins, cins = refs[:n_in], refs[n_in : n_in + nc]
            o0 = n_in + nc
            outs, couts = refs[o0 : o0 + n_out], refs[o0 + n_out : o0 + n_out + nc]
            s0 = o0 + n_out + nc
            scr, sems = refs[s0 : s0 + n_scr], refs[s0 + n_scr :]

            @pl.when(pl.program_id(0) == 0)
            def _():
                _exchange_start(_exchange_copies(kinds, cins, couts, *sems, with_arrivals=False))

            body(*ins, *outs, *scr)

            @pl.when(pl.program_id(0) == n_tiles - 1)
            def _():
                _exchange_finish(_exchange_copies(kinds, cins, couts, *sems))

    res = _pcall(
        full_body, name=name, grid=(n_tiles,), in_specs=in_specs, out_specs=out_specs, out_shape=out_shape, scratch_shapes=scratch, compiler_params=_cparams(1)
    )(*inputs, *[a for _, a in comm])
    return res[:n_out], res[n_out:]


def rms_matmul_nt(x, g, wts, tt, name, comm=None):
    T, D = x.shape
    N = wts[0].shape[0]
    nw = len(wts)
    nc = _tile(N, 1408, 128)

    def body(*refs):
        x_ref, g_ref = refs[0], refs[1]
        w_refs, o_refs = refs[2 : 2 + nw], refs[2 + nw : 2 + 2 * nw]
        xv = x_ref[...]
        h = (xv * _rstd(xv) * g_ref[...]).astype(BF16)
        for w_ref, o_ref in zip(w_refs, o_refs):
            for j in range(N // nc):
                cols = slice(j * nc, (j + 1) * nc)
                o_ref[:, cols] = lax.dot_general(h, w_ref[cols, :], NT, preferred_element_type=F32).astype(BF16)

    return _token_call(
        body,
        name,
        T // tt,
        [pl.BlockSpec((tt, D), lambda i: (i, 0)), _resident((1, D))] + [_resident((N, D))] * nw,
        [x, g, *wts],
        [pl.BlockSpec((tt, N), lambda i: (i, 0))] * nw,
        [jax.ShapeDtypeStruct((T, N), BF16)] * nw,
        [],
        comm,
    )


def matmul_rms_res(x, a_in, w, g, scale, tt, name):
    T, D = x.shape
    K = w.shape[0]
    act = len(a_in) == 2
    ec = _tile(K, MXU_COLS_V7X, 128)

    def body(*refs):
        if act:
            x_ref, gate_ref, up_ref, w_ref, g_ref, xo_ref, m_ref, a_ref = refs
            for j in range(K // ec):
                cols = slice(j * ec, (j + 1) * ec)
                gate = gate_ref[:, cols].astype(F32)
                a_ref[:, cols] = (gate * _sigmoid(gate) * up_ref[:, cols].astype(F32)).astype(BF16)
        else:
            x_ref, a_ref, w_ref, g_ref, xo_ref, m_ref = refs
        m = jnp.dot(a_ref[...], w_ref[...], preferred_element_type=F32)
        m_ref[...] = m
        xo_ref[...] = x_ref[...] + scale * (m * _rstd(m) * g_ref[...])

    row = pl.BlockSpec((tt, D), lambda i: (i, 0))
    col = pl.BlockSpec((tt, K), lambda i: (i, 0))
    outs, _ = _token_call(
        body,
        name,
        T // tt,
        [row] + [col] * len(a_in) + [_resident((K, D)), _resident((1, D))],
        [x, *a_in, w, g],
        [row, row] + ([col] if act else []),
        [jax.ShapeDtypeStruct((T, D), F32)] * 2 + ([jax.ShapeDtypeStruct((T, K), BF16)] if act else []),
        [],
        None,
    )
    return outs


def rms_bwd_matmul_nt(dxo, m, g, w, scale, gate_up, tt, name, comm=None):
    T, D = dxo.shape
    K = w.shape[0]
    act = gate_up is not None
    ec = _tile(K, MXU_COLS_V7X if act else 1024, 128)

    def body(*refs):
        if act:
            dxo_ref, m_ref, g_ref, w_ref, gate_ref, up_ref, dm_ref, dg_ref, dgate_ref, dup_ref = refs
        else:
            dxo_ref, m_ref, g_ref, w_ref, dm_ref, dg_ref, da_ref = refs

        @pl.when(pl.program_id(0) == 0)
        def _():
            dg_ref[...] = jnp.zeros_like(dg_ref)

        dy = scale * dxo_ref[...]
        mv = m_ref[...]
        r = _rstd(mv)
        mh = mv * r
        dg_ref[...] += _sum8(dy * mh)
        dmh = dy * g_ref[...]
        dm = (r * (dmh - mh * jnp.mean(dmh * mh, axis=-1, keepdims=True))).astype(BF16)
        dm_ref[...] = dm
        for j in range(K // ec):
            cols = slice(j * ec, (j + 1) * ec)
            da = lax.dot_general(dm, w_ref[cols, :], NT, preferred_element_type=F32)
            if act:
                gate = gate_ref[:, cols].astype(F32)
                sg = _sigmoid(gate)
                dup_ref[:, cols] = (da * (gate * sg)).astype(BF16)
                dgate_ref[:, cols] = (da * up_ref[:, cols].astype(F32) * (sg * (1.0 + gate * (1.0 - sg)))).astype(BF16)
            else:
                da_ref[:, cols] = da.astype(BF16)

    row = pl.BlockSpec((tt, D), lambda i: (i, 0))
    col = pl.BlockSpec((tt, K), lambda i: (i, 0))
    n_col = 2 if act else 1
    return _token_call(
        body,
        name,
        T // tt,
        [row, row, _resident((1, D)), _resident((K, D))] + ([col, col] if act else []),
        [dxo, m, g, w, *(gate_up or ())],
        [row, pl.BlockSpec((8, D), lambda i: (0, 0))] + [col] * n_col,
        [jax.ShapeDtypeStruct((T, D), BF16), jax.ShapeDtypeStruct((8, D), F32)] + [jax.ShapeDtypeStruct((T, K), BF16)] * n_col,
        [],
        comm,
    )


def matmul_rms_bwd(dxo, x, g, pairs, tt, name):
    T, D = x.shape
    N = pairs[0][0].shape[1]
    npairs = len(pairs)

    def body(*refs):
        dxo_ref, x_ref, g_ref = refs[:3]
        dz_refs = refs[3 : 3 + npairs]
        wt_refs = refs[3 + npairs : 3 + 2 * npairs]
        dx_ref, h_ref, dg_ref = refs[3 + 2 * npairs :]

        @pl.when(pl.program_id(0) == 0)
        def _():
            dg_ref[...] = jnp.zeros_like(dg_ref)

        dh = None
        for dz_ref, wt_ref in zip(dz_refs, wt_refs):
            part = jnp.dot(dz_ref[...], wt_ref[...], preferred_element_type=F32)
            dh = part if dh is None else dh + part
        xv = x_ref[...]
        gv = g_ref[...]
        r = _rstd(xv)
        xh = xv * r
        h_ref[...] = (xh * gv).astype(BF16)
        dg_ref[...] += _sum8(dh * xh)
        dxh = dh * gv
        dx_ref[...] = dxo_ref[...] + r * (dxh - xh * jnp.mean(dxh * xh, axis=-1, keepdims=True))

    row = pl.BlockSpec((tt, D), lambda i: (i, 0))
    outs, _ = _token_call(
        body,
        name,
        T // tt,
        [row, row, _resident((1, D))] + [pl.BlockSpec((tt, N), lambda i: (i, 0))] * npairs + [_resident((N, D))] * npairs,
        [dxo, x, g, *[p[0] for p in pairs], *[p[1] for p in pairs]],
        [row, row, pl.BlockSpec((8, D), lambda i: (0, 0))],
        [jax.ShapeDtypeStruct((T, D), F32), jax.ShapeDtypeStruct((T, D), BF16), jax.ShapeDtypeStruct((8, D), F32)],
        [],
        None,
    )
    return outs


def wgrad_tn(a, b, tk, name):
    T, M = a.shape
    N = b.shape[1]
    tm = M if M * N * 4 <= WGRAD_ACC_BYTES else M // 2
    nt = T // tk

    def body(a_ref, b_ref, o_ref, acc_ref):
        k = pl.program_id(1)

        @pl.when(k == 0)
        def _():
            acc_ref[...] = jnp.zeros_like(acc_ref)

        acc_ref[...] += lax.dot_general(a_ref[...], b_ref[...], TN, preferred_element_type=F32)

        @pl.when(k == nt - 1)
        def _():
            o_ref[...] = acc_ref[...].astype(BF16)

    return _pcall(
        body,
        name=name,
        grid=(M // tm, nt),
        in_specs=[pl.BlockSpec((tk, tm), lambda i, k: (k, i)), pl.BlockSpec((tk, N), lambda i, k: (k, 0))],
        out_specs=pl.BlockSpec((tm, N), lambda i, k: (i, 0)),
        out_shape=jax.ShapeDtypeStruct((M, N), BF16),
        scratch_shapes=[pltpu.VMEM((tm, N), F32)],
        compiler_params=_cparams(2),
    )(a, b)


def _sgu_recompute(z_ref, lng_ref, lnb_ref, S):
    z = z_ref[...].astype(F32)
    zu, zv = z[:, :S], z[:, S:]
    tu, tv = _gelu_tanh(zu), _gelu_tanh(zv)
    u = 0.5 * zu * (1.0 + tu)
    v = 0.5 * zv * (1.0 + tv)
    vc = v - jnp.mean(v, axis=-1, keepdims=True)
    rstd = lax.rsqrt(jnp.mean(vc * vc, axis=-1, keepdims=True) + EPS)
    vhat = vc * rstd
    vln = (vhat * lng_ref[...] + lnb_ref[...]).astype(BF16)
    return zu, zv, tu, tv, u, vhat, rstd, vln


def sgu_mix_fwd(zpre, lng, lnb, ws, bias_full, tt, name):
    T, S2 = zpre.shape
    S = S2 // 2
    dg = S // GROUPS

    def body(z_ref, lng_ref, lnb_ref, ws_ref, bias_ref, o_ref):
        _, _, _, _, u, _, _, vln = _sgu_recompute(z_ref, lng_ref, lnb_ref, S)
        for c in range(tt // CHUNK):
            rows = slice(c * CHUNK, (c + 1) * CHUNK)
            for gi in range(GROUPS):
                cols = slice(gi * dg, (gi + 1) * dg)
                mixed = jnp.dot(ws_ref[gi], vln[rows, cols], preferred_element_type=F32) + bias_ref[:, cols]
                o_ref[rows, cols] = (u[rows, cols] * mixed).astype(BF16)

    vec = pl.BlockSpec((1, S), lambda i: (0, 0))
    return _pcall(
        body,
        name=name,
        grid=(T // tt,),
        in_specs=[
            pl.BlockSpec((tt, S2), lambda i: (i, 0)),
            vec,
            vec,
            pl.BlockSpec((GROUPS, CHUNK, CHUNK), lambda i: (0, 0, 0)),
            pl.BlockSpec((CHUNK, S), lambda i: (0, 0)),
        ],
        out_specs=pl.BlockSpec((tt, S), lambda i: (i, 0)),
        out_shape=jax.ShapeDtypeStruct((T, S), BF16),
        compiler_params=_cparams(1),
    )(zpre, lng, lnb, ws, bias_full)


def sgu_mix_bwd(dgated, zpre, lng, lnb, ws, ws_t, bias_full, tt, name):
    T, S2 = zpre.shape
    S = S2 // 2
    dg = S // GROUPS

    def body(dgt_ref, z_ref, lng_ref, lnb_ref, ws_ref, wst_ref, bias_ref, dz_ref, dws_ref, dbias_ref, dlng_ref, dlnb_ref, du_ref, dvln_ref):
        @pl.when(pl.program_id(0) == 0)
        def _():
            dws_ref[...] = jnp.zeros_like(dws_ref)
            dbias_ref[...] = jnp.zeros_like(dbias_ref)
            dlng_ref[...] = jnp.zeros_like(dlng_ref)
            dlnb_ref[...] = jnp.zeros_like(dlnb_ref)

        zu, zv, tu, tv, u, vhat, rstd, vln = _sgu_recompute(z_ref, lng_ref, lnb_ref, S)
        for c in range(tt // CHUNK):
            rows = slice(c * CHUNK, (c + 1) * CHUNK)
            for gi in range(GROUPS):
                cols = slice(gi * dg, (gi + 1) * dg)
                v_cg = vln[rows, cols]
                mixed = jnp.dot(ws_ref[gi], v_cg, preferred_element_type=F32) + bias_ref[:, cols]
                dgt = dgt_ref[rows, cols].astype(F32)
                du_ref[rows, cols] = dgt * mixed
                dmx = dgt * u[rows, cols]
                dbias_ref[:, cols] += dmx
                dmx16 = dmx.astype(BF16)
                dws_ref[gi] += lax.dot_general(dmx16, v_cg, NT, preferred_element_type=F32)
                dvln_ref[rows, cols] = jnp.dot(wst_ref[gi], dmx16, preferred_element_type=F32)

        dvln = dvln_ref[...]
        dlng_ref[...] += _sum8(dvln * vhat)
        dlnb_ref[...] += _sum8(dvln)
        dvhat = dvln * lng_ref[...]
        dv = rstd * (dvhat - jnp.mean(dvhat, axis=-1, keepdims=True) - vhat * jnp.mean(dvhat * vhat, axis=-1, keepdims=True))
        dz_ref[:, :S] = (du_ref[...] * _gelu_grad(zu, tu)).astype(BF16)
        dz_ref[:, S:] = (dv * _gelu_grad(zv, tv)).astype(BF16)

    vec = pl.BlockSpec((1, S), lambda i: (0, 0))
    wsp = pl.BlockSpec((GROUPS, CHUNK, CHUNK), lambda i: (0, 0, 0))
    full = pl.BlockSpec((CHUNK, S), lambda i: (0, 0))
    acc8 = pl.BlockSpec((8, S), lambda i: (0, 0))
    return _pcall(
        body,
        name=name,
        grid=(T // tt,),
        in_specs=[pl.BlockSpec((tt, S), lambda i: (i, 0)), pl.BlockSpec((tt, S2), lambda i: (i, 0)), vec, vec, wsp, wsp, full],
        out_specs=[pl.BlockSpec((tt, S2), lambda i: (i, 0)), wsp, full, acc8, acc8],
        out_shape=[
            jax.ShapeDtypeStruct((T, S2), BF16),
            jax.ShapeDtypeStruct((GROUPS, CHUNK, CHUNK), F32),
            jax.ShapeDtypeStruct((CHUNK, S), F32),
            jax.ShapeDtypeStruct((8, S), F32),
            jax.ShapeDtypeStruct((8, S), F32),
        ],
        scratch_shapes=[pltpu.VMEM((tt, S), F32), pltpu.VMEM((tt, S), F32)],
        compiler_params=_cparams(1),
    )(dgated, zpre, lng, lnb, ws, ws_t, bias_full)


def conv_mid_fwd(p, wdw, bdw, lng, lnb, tt, name):
    T, C2 = p.shape
    C = C2 // 2

    def body(p_ref, w_ref, b_ref, lng_ref, lnb_ref, yc_ref, ys_ref, ybuf):
        @pl.when(pl.program_id(0) == 0)
        def _():
            ybuf[0:HALO, :] = jnp.zeros((HALO, C), F32)

        @pl.when(pl.program_id(0) > 0)
        def _():
            ybuf[0:HALO, :] = ybuf[tt : tt + HALO, :]

        pv = p_ref[...].astype(F32)
        ybuf[HALO:, :] = pv[:, :C] * _sigmoid(pv[:, C:])
        acc = jnp.broadcast_to(b_ref[...], (tt, C))
        for k in range(CONV_W):
            acc = acc + w_ref[k : k + 1, :] * ybuf[pl.ds(HALO - (CONV_W - 1) + k, tt), :]
        yc_ref[...] = acc
        yc = acc - jnp.mean(acc, axis=-1, keepdims=True)
        yn = yc * lax.rsqrt(jnp.mean(yc * yc, axis=-1, keepdims=True) + EPS) * lng_ref[...] + lnb_ref[...]
        ys_ref[...] = (yn * _sigmoid(yn)).astype(BF16)

    vec = pl.BlockSpec((1, C), lambda i: (0, 0))
    row = pl.BlockSpec((tt, C), lambda i: (i, 0))
    return _pcall(
        body,
        name=name,
        grid=(T // tt,),
        in_specs=[pl.BlockSpec((tt, C2), lambda i: (i, 0)), pl.BlockSpec((HALO, C), lambda i: (0, 0)), vec, vec, vec],
        out_specs=[row, row],
        out_shape=[jax.ShapeDtypeStruct((T, C), F32), jax.ShapeDtypeStruct((T, C), BF16)],
        scratch_shapes=[pltpu.VMEM((tt + HALO, C), F32)],
        compiler_params=_cparams(1),
    )(p, wdw, bdw, lng, lnb)


def conv_mid_bwd(dys, yc, p, wdw, lng, lnb, tt, name):
    T, C2 = p.shape
    C = C2 // 2
    n = T // tt

    def body(dys_ref, yc_ref, p_ref, w_ref, lng_ref, lnb_ref, dp_ref, dw_ref, db_ref, dlng_ref, dlnb_ref, dbuf):
        @pl.when(pl.program_id(0) == 0)
        def _():
            dbuf[tt : tt + HALO, :] = jnp.zeros((HALO, C), F32)
            dw_ref[...] = jnp.zeros_like(dw_ref)
            db_ref[...] = jnp.zeros_like(db_ref)
            dlng_ref[...] = jnp.zeros_like(dlng_ref)
            dlnb_ref[...] = jnp.zeros_like(dlnb_ref)

        @pl.when(pl.program_id(0) > 0)
        def _():
            dbuf[tt : tt + HALO, :] = dbuf[0:HALO, :]

        ycv = yc_ref[...]
        ycc = ycv - jnp.mean(ycv, axis=-1, keepdims=True)
        rstd = lax.rsqrt(jnp.mean(ycc * ycc, axis=-1, keepdims=True) + EPS)
        yhat = ycc * rstd
        lng_v = lng_ref[...]
        yn = yhat * lng_v + lnb_ref[...]
        sg = _sigmoid(yn)
        dyn = dys_ref[...].astype(F32) * (sg * (1.0 + yn * (1.0 - sg)))
        dlng_ref[...] += _sum8(dyn * yhat)
        dlnb_ref[...] += _sum8(dyn)
        dyh = dyn * lng_v
        dyc = rstd * (dyh - jnp.mean(dyh, axis=-1, keepdims=True) - yhat * jnp.mean(dyh * yhat, axis=-1, keepdims=True))
        db_ref[...] += _sum8(dyc)
        dbuf[0:tt, :] = dyc

        pv = p_ref[...].astype(F32)
        a = pv[:, :C]
        sgate = _sigmoid(pv[:, C:])
        y = a * sgate
        dy = jnp.zeros((tt, C), F32)
        for k in range(CONV_W):
            shifted = dbuf[pl.ds(CONV_W - 1 - k, tt), :]
            dy = dy + w_ref[k : k + 1, :] * shifted
            dw_ref[8 * k : 8 * k + 8, :] += _sum8(y * shifted)
        dp_ref[:, :C] = (dy * sgate).astype(BF16)
        dp_ref[:, C:] = (dy * a * sgate * (1.0 - sgate)).astype(BF16)

    vec = pl.BlockSpec((1, C), lambda i: (0, 0))
    row = pl.BlockSpec((tt, C), lambda i: (n - 1 - i, 0))
    row2 = pl.BlockSpec((tt, C2), lambda i: (n - 1 - i, 0))
    acc8 = pl.BlockSpec((8, C), lambda i: (0, 0))
    return _pcall(
        body,
        name=name,
        grid=(n,),
        in_specs=[row, row, row2, pl.BlockSpec((HALO, C), lambda i: (0, 0)), vec, vec],
        out_specs=[row2, pl.BlockSpec((HALO * 8, C), lambda i: (0, 0)), acc8, acc8, acc8],
        out_shape=[
            jax.ShapeDtypeStruct((T, C2), BF16),
            jax.ShapeDtypeStruct((HALO * 8, C), F32),
            jax.ShapeDtypeStruct((8, C), F32),
            jax.ShapeDtypeStruct((8, C), F32),
            jax.ShapeDtypeStruct((8, C), F32),
        ],
        scratch_shapes=[pltpu.VMEM((tt + HALO, C), F32)],
        compiler_params=_cparams(1),
    )(dys, yc, p, wdw, lng, lnb)


def loss_head(y, target, tt, name):
    T, D = y.shape

    def body(y_ref, t_ref, dy_ref, sq_ref):
        @pl.when(pl.program_id(0) == 0)
        def _():
            sq_ref[...] = jnp.zeros_like(sq_ref)

        err = y_ref[...] - t_ref[...]
        dy_ref[...] = err * (1.0 / D)
        sq_ref[...] += _sum8(err * err)

    row = pl.BlockSpec((tt, D), lambda i: (i, 0))
    return _pcall(
        body,
        name=name,
        grid=(T // tt,),
        in_specs=[row, row],
        out_specs=[row, pl.BlockSpec((8, D), lambda i: (0, 0))],
        out_shape=[jax.ShapeDtypeStruct((T, D), F32), jax.ShapeDtypeStruct((8, D), F32)],
        compiler_params=_cparams(1),
    )(y, target)


def sum_slots(slots, name):
    _, r, w = slots.shape

    def body(s_ref, o_ref):
        total = s_ref[0].astype(F32)
        for p in range(1, N_DEV):
            total = total + s_ref[p].astype(F32)
        o_ref[...] = total

    return _pcall(body, name=name, out_shape=jax.ShapeDtypeStruct((r, w), F32), compiler_params=pltpu.CompilerParams(vmem_limit_bytes=VMEM_LIMIT_V7X))(slots)


def adamw(w, g, m, v, name):
    R, C = w.shape
    tr = _tile(R, 512, 8)
    c1 = 1.0 - ADAM_B1**ADAM_STEP
    c2 = 1.0 - ADAM_B2**ADAM_STEP

    def body(w_ref, g_ref, m_ref, v_ref, d_ref, mo_ref, vo_ref):
        gv = g_ref[...]
        m2 = ADAM_B1 * m_ref[...] + (1.0 - ADAM_B1) * gv
        v2 = ADAM_B2 * v_ref[...] + (1.0 - ADAM_B2) * (gv * gv)
        mo_ref[...] = m2
        vo_ref[...] = v2
        d_ref[...] = -ADAM_LR * ((m2 / c1) / (jnp.sqrt(v2 / c2) + ADAM_EPS) + ADAM_WD * w_ref[...])

    blk = pl.BlockSpec((tr, C), lambda i: (i, 0))
    return _pcall(
        body,
        name=name,
        grid=(R // tr,),
        in_specs=[blk] * 4,
        out_specs=[blk] * 3,
        out_shape=[jax.ShapeDtypeStruct((R, C), F32)] * 3,
        compiler_params=_cparams(1),
    )(w, g, m, v)


def _sublayers(depth):
    out = []
    for layer in range(depth):
        out.append(("ffn", layer, 0))
        out.append(("sgu" if layer % N_MIXERS == 0 else "conv", layer, layer // N_MIXERS))
        out.append(("ffn", layer, 1))
    return out


def _row_blocks(sub, shards):
    kind, layer, idx = sub
    if kind == "ffn":
        blk = {"gate_t": shards["ff_w_gate"][layer, idx].T, "up_t": shards["ff_w_up"][layer, idx].T, "down": shards["ff_w_down"][layer, idx]}
    elif kind == "sgu":
        blk = {"in_t": shards["sgu_w_in"][idx].T, "out": shards["sgu_w_out"][idx]}
    else:
        blk = {"pw1_t": shards["conv_w_pw1"][idx].T, "pw2": shards["conv_w_pw2"][idx]}
    return {nm: arr.astype(BF16) for nm, arr in blk.items()}


def _pad8(a):
    return jnp.pad(a, ((0, (-a.shape[0]) % 8), (0, 0)))


def kernel(x, norm_g, ff_w_gate, ff_w_up, ff_w_down, sgu_w_in, sgu_ln_g, sgu_ln_b, sgu_w_spatial, sgu_b_spatial, sgu_w_out, conv_w_pw1, conv_w_dw, conv_b_dw, conv_ln_g, conv_ln_b, conv_w_pw2, loss_target, m_norm_g, m_ff_w_gate, m_ff_w_up, m_ff_w_down, m_sgu_w_in, m_sgu_ln_g, m_sgu_ln_b, m_sgu_w_spatial, m_sgu_b_spatial, m_sgu_w_out, m_conv_w_pw1, m_conv_w_dw, m_conv_b_dw, m_conv_ln_g, m_conv_ln_b, m_conv_w_pw2, v_norm_g, v_ff_w_gate, v_ff_w_up, v_ff_w_down, v_sgu_w_in, v_sgu_ln_g, v_sgu_ln_b, v_sgu_w_spatial, v_sgu_b_spatial, v_sgu_w_out, v_conv_w_pw1, v_conv_w_dw, v_conv_b_dw, v_conv_ln_g, v_conv_ln_b, v_conv_w_pw2):
    names = ["norm_g", "ff_w_gate", "ff_w_up", "ff_w_down", "sgu_w_in", "sgu_ln_g", "sgu_ln_b", "sgu_w_spatial", "sgu_b_spatial", "sgu_w_out", "conv_w_pw1", "conv_w_dw", "conv_b_dw", "conv_ln_g", "conv_ln_b", "conv_w_pw2"]
    weights = dict(zip(names, [norm_g, ff_w_gate, ff_w_up, ff_w_down, sgu_w_in, sgu_ln_g, sgu_ln_b, sgu_w_spatial, sgu_b_spatial, sgu_w_out, conv_w_pw1, conv_w_dw, conv_b_dw, conv_ln_g, conv_ln_b, conv_w_pw2]))
    moments_m = dict(zip(names, [m_norm_g, m_ff_w_gate, m_ff_w_up, m_ff_w_down, m_sgu_w_in, m_sgu_ln_g, m_sgu_ln_b, m_sgu_w_spatial, m_sgu_b_spatial, m_sgu_w_out, m_conv_w_pw1, m_conv_w_dw, m_conv_b_dw, m_conv_ln_g, m_conv_ln_b, m_conv_w_pw2]))
    moments_v = dict(zip(names, [v_norm_g, v_ff_w_gate, v_ff_w_up, v_ff_w_down, v_sgu_w_in, v_sgu_ln_g, v_sgu_ln_b, v_sgu_w_spatial, v_sgu_b_spatial, v_sgu_w_out, v_conv_w_pw1, v_conv_w_dw, v_conv_b_dw, v_conv_ln_g, v_conv_ln_b, v_conv_w_pw2]))

    _, T, D = x.shape
    depth = norm_g.shape[0]
    n_conv = conv_w_dw.shape[0]
    n_sgu = sgu_w_in.shape[0]
    S = sgu_ln_g.shape[1]
    lanes = norm_g.shape[2]
    subs = _sublayers(depth)
    n_sub = len(subs)

    cx, cy, cc = (lax.axis_index(a) for a in MESH_AXES)
    my_block = 4 * cx + 2 * cy + cc

    blocks = [_row_blocks(sub, weights) for sub in subs]
    dw_pad = jnp.pad(conv_w_dw, ((0, 0), (0, HALO - CONV_W), (0, 0)))
    small_parts = [_pad8(p) for p in (norm_g.reshape(-1, lanes), dw_pad.reshape(-1, lanes), conv_b_dw, conv_ln_g, conv_ln_b)]
    small_rows = [p.shape[0] for p in small_parts]
    small = jnp.concatenate(small_parts, axis=0)

    def gathered(block_names, outs):
        return {nm: o.reshape(N_DEV * o.shape[1], o.shape[2]) for nm, o in zip(block_names, outs)}

    first = exchange([("gather", a) for a in blocks[0].values()] + [("gather", small)], name="gather_first")
    W = [None] * n_sub
    W[0] = gathered(blocks[0].keys(), first[:-1])
    gsmall = first[-1]

    small_full = jnp.transpose(gsmall, (1, 0, 2)).reshape(gsmall.shape[1], N_DEV * lanes)
    so = [0]
    for r in small_rows:
        so.append(so[-1] + r)
    norm_full = small_full[so[0] : so[0] + depth * norm_g.shape[1]].reshape(depth, -1, D)
    dw_full = small_full[so[1] : so[1] + n_conv * HALO].reshape(n_conv, HALO, D)
    bdw_full, clng_full, clnb_full = (small_full[so[k] : so[k] + n_conv] for k in (2, 3, 4))

    causal = jnp.tril(jnp.ones((CHUNK, CHUNK), dtype=bool))
    ws_all = jnp.where(causal[None, None], sgu_w_spatial, 0.0).astype(BF16)
    wst_all = jnp.swapaxes(ws_all, -1, -2)
    bias_full_all = jnp.repeat(jnp.swapaxes(sgu_b_spatial, -1, -2), S // GROUPS, axis=-1)

    tt = _tile(T, 512, CHUNK)
    tt_mix = _tile(T, 256, CHUNK)

    def vec(v):
        return v.reshape(1, -1)

    def norms(kind, layer, idx):
        pre = 4 * idx if kind == "ffn" else 2
        return vec(norm_full[layer, pre]), vec(norm_full[layer, pre + 1])

    xs = x[0]
    saved = []
    for si, (kind, layer, idx) in enumerate(subs):
        w = W[si]
        g_pre, g_post = norms(kind, layer, idx)
        nxt = [("gather", a) for a in blocks[si + 1].values()] if si + 1 < n_sub else None
        if kind == "ffn":
            (gate, up), got = rms_matmul_nt(xs, g_pre, [w["gate_t"], w["up_t"]], tt, "ffn_in", nxt)
            x_new, o, a = matmul_rms_res(xs, (gate, up), w["down"], g_post, FFN_SCALE, tt, "ffn_out")
            saved.append((xs, gate, up, o, a))
        elif kind == "sgu":
            (zpre,), got = rms_matmul_nt(xs, g_pre, [w["in_t"]], tt, "sgu_in", nxt)
            gated = sgu_mix_fwd(zpre, vec(sgu_ln_g[idx]), vec(sgu_ln_b[idx]), ws_all[idx], bias_full_all[idx], tt_mix, name="sgu_mix")
            x_new, mm = matmul_rms_res(xs, (gated,), w["out"], g_post, 1.0, tt, "sgu_out")
            saved.append((xs, zpre, gated, mm))
        else:
            (p,), got = rms_matmul_nt(xs, g_pre, [w["pw1_t"]], tt, "conv_pw1", nxt)
            yc, ys = conv_mid_fwd(p, dw_full[idx], vec(bdw_full[idx]), vec(clng_full[idx]), vec(clnb_full[idx]), tt, name="conv_mid")
            x_new, mm = matmul_rms_res(xs, (ys,), w["pw2"], g_post, 1.0, tt, "conv_pw2")
            saved.append((xs, p, yc, ys, mm))
        if nxt:
            W[si + 1] = gathered(blocks[si + 1].keys(), got)
        xs = x_new

    dx, sq = loss_head(xs, loss_target[0], tt, name="loss_head")
    loss = lax.psum(0.5 * jnp.sum(sq) / D, MESH_AXES)

    d_norm = [[None] * norm_full.shape[1] for _ in range(depth)]
    d_sgu = [None] * n_sgu
    d_conv = [None] * n_conv
    slots = [None] * n_sub
    pending = None

    def scatter_of(p):
        return [("scatter", dwm.reshape(N_DEV, dwm.shape[0] // N_DEV, D)) for dwm in p[1].values()] if p else None

    for si in reversed(range(n_sub)):
        kind, layer, idx = subs[si]
        w = W[si]
        g_pre, g_post = norms(kind, layer, idx)
        pre = 4 * idx if kind == "ffn" else 2
        if kind == "ffn":
            xs_in, gate, up, o, a = saved[si]
            (do, dg_post, dgate, dup), got = rms_bwd_matmul_nt(dx, o, g_post, w["down"], FFN_SCALE, (gate, up), tt, "ffn_out_bwd", scatter_of(pending))
            dx, h, dg_pre = matmul_rms_bwd(dx, xs_in, g_pre, [(dgate, w["gate_t"]), (dup, w["up_t"])], tt, "ffn_in_bwd")
            dws_now = {"gate_t": wgrad_tn(dgate, h, tt, name="wgrad_ffn_in"), "up_t": wgrad_tn(dup, h, tt, name="wgrad_ffn_in"), "down": wgrad_tn(a, do, tt, name="wgrad_ffn_out")}
        elif kind == "sgu":
            xs_in, zpre, gated, mm = saved[si]
            (dm, dg_post, dgated), got = rms_bwd_matmul_nt(dx, mm, g_post, w["out"], 1.0, None, tt, "sgu_out_bwd", scatter_of(pending))
            dzpre, dws, dbias, dlng, dlnb = sgu_mix_bwd(dgated, zpre, vec(sgu_ln_g[idx]), vec(sgu_ln_b[idx]), ws_all[idx], wst_all[idx], bias_full_all[idx], tt_mix, name="sgu_mix_bwd")
            dx, h, dg_pre = matmul_rms_bwd(dx, xs_in, g_pre, [(dzpre, w["in_t"])], tt, "sgu_in_bwd")
            dws_now = {"in_t": wgrad_tn(dzpre, h, tt, name="wgrad_sgu_in"), "out": wgrad_tn(gated, dm, tt, name="wgrad_sgu_out")}
            d_sgu[idx] = (dws, dbias, dlng, dlnb)
        else:
            xs_in, p, yc, ys, mm = saved[si]
            (dm, dg_post, dys), got = rms_bwd_matmul_nt(dx, mm, g_post, w["pw2"], 1.0, None, tt, "conv_pw2_bwd", scatter_of(pending))
            dp, dwdw, dbdw, dlng, dlnb = conv_mid_bwd(dys, yc, p, dw_full[idx], vec(clng_full[idx]), vec(clnb_full[idx]), tt, name="conv_mid_bwd")
            dx, h, dg_pre = matmul_rms_bwd(dx, xs_in, g_pre, [(dp, w["pw1_t"])], tt, "conv_pw1_bwd")
            dws_now = {"pw1_t": wgrad_tn(dp, h, tt, name="wgrad_conv_pw1"), "pw2": wgrad_tn(ys, dm, tt, name="wgrad_conv_pw2")}
            d_conv[idx] = (dwdw, dbdw, dlng, dlnb)
        d_norm[layer][pre], d_norm[layer][pre + 1] = dg_pre, dg_post
        if pending:
            slots[pending[0]] = dict(zip(pending[1].keys(), got))
        pending = (si, dws_now)
    grad_x = dx[None]

    def sum8(v):
        return jnp.sum(v, axis=0)

    g_norm = jnp.stack([jnp.stack([sum8(d) for d in row]) for row in d_norm])
    g_dw = jnp.stack([jnp.sum(d[0].reshape(HALO, 8, D), axis=1) for d in d_conv])
    g_bdw, g_clng, g_clnb = (jnp.stack([sum8(d[k]) for d in d_conv]) for k in (1, 2, 3))
    g_slng, g_slnb = (jnp.stack([sum8(d[k]) for d in d_sgu]) for k in (2, 3))
    g_bsp = jnp.stack([jnp.sum(d[1].reshape(CHUNK, GROUPS, S // GROUPS), axis=-1).T for d in d_sgu])
    g_wsp = jnp.stack([jnp.where(causal[None], d[0], 0.0) for d in d_sgu])
    sparts = [g_norm, g_dw, g_bdw, g_clng, g_clnb, g_slng, g_slnb, g_bsp, g_wsp]
    srows = [p.size // D for p in sparts]
    sgrad = jnp.concatenate([_pad8(p.reshape(-1, D)) for p in sparts], axis=0)

    last = exchange(scatter_of(pending) + [("gather", sgrad)], name="scatter_last")
    slots[pending[0]] = dict(zip(pending[1].keys(), last[:-1]))
    stotal = sum_slots(last[-1], name="sum_slots")

    gs = {(si, nm): sum_slots(s, name="sum_slots") for si in range(n_sub) for nm, s in slots[si].items()}
    grads = {}
    ffn_si = {(layer, idx): si for si, (kind, layer, idx) in enumerate(subs) if kind == "ffn"}
    sgu_si = {idx: si for si, (kind, layer, idx) in enumerate(subs) if kind == "sgu"}
    conv_si = {idx: si for si, (kind, layer, idx) in enumerate(subs) if kind == "conv"}
    grads["ff_w_gate"] = jnp.stack([jnp.stack([gs[(ffn_si[(l, f)], "gate_t")].T for f in range(2)]) for l in range(depth)])
    grads["ff_w_up"] = jnp.stack([jnp.stack([gs[(ffn_si[(l, f)], "up_t")].T for f in range(2)]) for l in range(depth)])
    grads["ff_w_down"] = jnp.stack([jnp.stack([gs[(ffn_si[(l, f)], "down")] for f in range(2)]) for l in range(depth)])
    grads["sgu_w_in"] = jnp.stack([gs[(sgu_si[j], "in_t")].T for j in range(n_sgu)])
    grads["sgu_w_out"] = jnp.stack([gs[(sgu_si[j], "out")] for j in range(n_sgu)])
    grads["conv_w_pw1"] = jnp.stack([gs[(conv_si[j], "pw1_t")].T for j in range(n_conv)])
    grads["conv_w_pw2"] = jnp.stack([gs[(conv_si[j], "pw2")] for j in range(n_conv)])

    so = [0]
    for r in srows:
        so.append(so[-1] + r + (-r) % 8)
    sp = [stotal[so[k] : so[k] + srows[k]] for k in range(len(sparts))]

    def my_lanes(v):
        return lax.dynamic_slice_in_dim(v, my_block * lanes, lanes, axis=-1)

    grads["norm_g"] = my_lanes(sp[0].reshape(g_norm.shape))
    grads["conv_w_dw"] = my_lanes(sp[1].reshape(g_dw.shape))[:, :CONV_W]
    grads["conv_b_dw"] = my_lanes(sp[2])
    grads["conv_ln_g"] = my_lanes(sp[3])
    grads["conv_ln_b"] = my_lanes(sp[4])
    grads["sgu_ln_g"] = sp[5].reshape(g_slng.shape)
    grads["sgu_ln_b"] = sp[6].reshape(g_slnb.shape)
    grads["sgu_b_spatial"] = sp[7].reshape(g_bsp.shape)
    grads["sgu_w_spatial"] = sp[8].reshape(g_wsp.shape)

    deltas, new_m, new_v = {}, {}, {}
    for nm in names:
        w = weights[nm]
        two_d = (-1, w.shape[-1])
        d, m2, v2 = adamw(w.reshape(two_d), grads[nm].reshape(two_d), moments_m[nm].reshape(two_d), moments_v[nm].reshape(two_d), name="adamw")
        deltas[nm], new_m[nm], new_v[nm] = d.reshape(w.shape), m2.reshape(w.shape), v2.reshape(w.shape)

    return (loss, grad_x, *[grads[n] for n in names], *[deltas[n] for n in names], *[new_m[n] for n in names], *[new_v[n] for n in names])
```

```python
import jax
import jax.numpy as jnp
from jax import lax
from jax.experimental import pallas as pl
from jax.experimental.pallas import tpu as pltpu

F32 = jnp.float32
BF16 = jnp.bfloat16

EPS = 1e-6
FFN_SCALE = 0.5
N_MIXERS = 2
CHUNK = 128
GROUPS = 8
CONV_W = 31
HALO = 32
GELU_C0 = 0.7978845608028654
GELU_C1 = 0.044715
ADAM_LR, ADAM_B1, ADAM_B2, ADAM_EPS, ADAM_WD, ADAM_STEP = 0.001, 0.9, 0.999, 1e-08, 0.01, 10

MESH_AXES = ("x", "y", "c")
N_DEV = 8
VMEM_LIMIT_V7X = 56 * 1024 * 1024
WGRAD_ACC_BYTES = 16 * 1024 * 1024
MXU_COLS_V7X = 256
SUBLANES, LANES = 8, 128

NT = (((1,), (1,)), ((), ()))
TN = (((0,), (0,)), ((), ()))

HBM_SPEC = pl.BlockSpec(memory_space=pl.ANY)
MESH_ID = pl.DeviceIdType.MESH


def _pcall(body, **kw):
    return pl.pallas_call(body, **kw)


def _cparams(n_axes):
    return pltpu.CompilerParams(dimension_semantics=("arbitrary",) * n_axes, vmem_limit_bytes=VMEM_LIMIT_V7X)


def _tile(n, pref, align):
    if n <= pref:
        return n
    t = (pref // align) * align
    while t > align and n % t:
        t -= align
    assert n % t == 0, (n, pref, align)
    return t


def _rstd(x):
    return lax.rsqrt(jnp.mean(x * x, axis=-1, keepdims=True) + EPS)


def _sum8(v):
    t, d = v.shape
    return v.reshape(t // 8, 8, d).sum(axis=0)


def _sigmoid(x):
    return jax.nn.sigmoid(x)


def _gelu_tanh(z):
    return jnp.tanh(GELU_C0 * (z + GELU_C1 * z * z * z))


def _gelu_grad(z, t):
    return 0.5 * (1.0 + t) + 0.5 * z * (1.0 - t * t) * (GELU_C0 * (1.0 + 3.0 * GELU_C1 * z * z))


def _resident(shape):
    return pl.BlockSpec(shape, lambda i: (0,) * len(shape), pipeline_mode=pl.Buffered(1))


def _exchange_io(comm):
    n = len(comm)
    out_shape = [jax.ShapeDtypeStruct(((N_DEV,) + a.shape) if kind == "gather" else a.shape, a.dtype) for kind, a in comm]
    scratch = [pltpu.SemaphoreType.DMA((n, N_DEV - 1)), pltpu.SemaphoreType.DMA((n, N_DEV - 1)), pltpu.SemaphoreType.DMA((n,))]
    return [HBM_SPEC] * n, [HBM_SPEC] * n, out_shape, scratch


def _exchange_copies(kinds, in_refs, out_refs, send_sems, recv_sems, local_sems, with_arrivals=True):
    x, y, c = (lax.axis_index(a) for a in MESH_AXES)
    me = 4 * x + 2 * y + c
    local, sends, arrivals = [], [], []
    for a, kind in enumerate(kinds):
        src, dst = in_refs[a], out_refs[a]
        gather = kind == "gather"
        local.append(pltpu.make_async_copy(src if gather else src.at[me], dst.at[me], local_sems.at[a]))
        for k in range(N_DEV - 1):
            mask = k + 1
            px = 1 - x if mask & 4 else x
            py = 1 - y if mask & 2 else y
            pc = 1 - c if mask & 1 else c
            peer = 4 * px + 2 * py + pc
            block = src if gather else src.at[peer]
            for into, lst in ((me, sends), (peer, arrivals)):
                if lst is arrivals and not with_arrivals:
                    continue
                lst.append(
                    pltpu.make_async_remote_copy(
                        src_ref=block, dst_ref=dst.at[into], send_sem=send_sems.at[a, k], recv_sem=recv_sems.at[a, k], device_id=(px, py, pc), device_id_type=MESH_ID
                    )
                )
    return local, sends, arrivals


def _exchange_start(copies):
    local, sends, _ = copies
    for cp in local + sends:
        cp.start()


def _exchange_finish(copies):
    local, sends, arrivals = copies
    for cp in arrivals:
        cp.wait_recv()
    for cp in sends:
        cp.wait_send()
    for cp in local:
        cp.wait()


def exchange(comm, name):
    kinds = [k for k, _ in comm]
    n = len(comm)
    in_specs, out_specs, out_shape, scratch = _exchange_io(comm)

    def body(*refs):
        copies = _exchange_copies(kinds, refs[:n], refs[n : 2 * n], *refs[2 * n :])
        _exchange_start(copies)
        _exchange_finish(copies)

    return _pcall(body, name=name, in_specs=in_specs, out_specs=out_specs, out_shape=out_shape, scratch_shapes=scratch)(*[a for _, a in comm])


def _token_call(body, name, n_tiles, in_specs, inputs, out_specs, out_shape, scratch, comm):
    n_in, n_out, n_scr = len(inputs), len(out_shape), len(scratch)
    comm = comm or []
    nc = len(comm)
    full_body = body
    if nc:
        kinds = [k for k, _ in comm]
        c_in, c_out, c_shape, c_scr = _exchange_io(comm)
        in_specs, out_specs, out_shape, scratch = in_specs + c_in, out_specs + c_out, out_shape + c_shape, scratch + c_scr

        def full_body(*refs):
            ins, cins = refs[:n_in], refs[n_in : n_in + nc]
            o0 = n_in + nc
            outs, couts = refs[o0 : o0 + n_out], refs[o0 + n_out : o0 + n_out + nc]
            s0 = o0 + n_out + nc
            scr, sems = refs[s0 : s0 + n_scr], refs[s0 + n_scr :]

            @pl.when(pl.program_id(0) == 0)
            def _():
                _exchange_start(_exchange_copies(kinds, cins, couts, *sems, with_arrivals=False))

            body(*ins, *outs, *scr)

            @pl.when(pl.program_id(0) == n_tiles - 1)
            def _():
                _exchange_finish(_exchange_copies(kinds, cins, couts, *sems))

    res = _pcall(
        full_body, name=name, grid=(n_tiles,), in_specs=in_specs, out_specs=out_specs, out_shape=out_shape, scratch_shapes=scratch, compiler_params=_cparams(1)
    )(*inputs, *[a for _, a in comm])
    return res[:n_out], res[n_out:]


def rms_matmul_nt(x, g, wts, tt, name, comm=None):
    T, D = x.shape
    N = wts[0].shape[0]
    nw = len(wts)
    nc = _tile(N, 1408, 128)

    def body(*refs):
        x_ref, g_ref = refs[0], refs[1]
        w_refs, o_refs = refs[2 : 2 + nw], refs[2 + nw : 2 + 2 * nw]
        xv = x_ref[...]
        h = (xv * _rstd(xv) * g_ref[...]).astype(BF16)
        for w_ref, o_ref in zip(w_refs, o_refs):
            for j in range(N // nc):
                cols = slice(j * nc, (j + 1) * nc)
                o_ref[:, cols] = lax.dot_general(h, w_ref[cols, :], NT, preferred_element_type=F32).astype(BF16)

    return _token_call(
        body,
        name,
        T // tt,
        [pl.BlockSpec((tt, D), lambda i: (i, 0)), _resident((1, D))] + [_resident((N, D))] * nw,
        [x, g, *wts],
        [pl.BlockSpec((tt, N), lambda i: (i, 0))] * nw,
        [jax.ShapeDtypeStruct((T, N), BF16)] * nw,
        [],
        comm,
    )


def matmul_rms_res(x, a_in, w, g, scale, tt, name):
    T, D = x.shape
    K = w.shape[0]
    act = len(a_in) == 2
    ec = _tile(K, MXU_COLS_V7X, 128)

    def body(*refs):
        if act:
            x_ref, gate_ref, up_ref, w_ref, g_ref, xo_ref, m_ref, a_ref = refs
            for j in range(K // ec):
                cols = slice(j * ec, (j + 1) * ec)
                gate = gate_ref[:, cols].astype(F32)
                a_ref[:, cols] = (gate * _sigmoid(gate) * up_ref[:, cols].astype(F32)).astype(BF16)
        else:
            x_ref, a_ref, w_ref, g_ref, xo_ref, m_ref = refs
        m = jnp.dot(a_ref[...], w_ref[...], preferred_element_type=F32)
        m_ref[...] = m
        xo_ref[...] = x_ref[...] + scale * (m * _rstd(m) * g_ref[...])

    row = pl.BlockSpec((tt, D), lambda i: (i, 0))
    col = pl.BlockSpec((tt, K), lambda i: (i, 0))
    outs, _ = _token_call(
        body,
        name,
        T // tt,
        [row] + [col] * len(a_in) + [_resident((K, D)), _resident((1, D))],
        [x, *a_in, w, g],
        [row, row] + ([col] if act else []),
        [jax.ShapeDtypeStruct((T, D), F32)] * 2 + ([jax.ShapeDtypeStruct((T, K), BF16)] if act else []),
        [],
        None,
    )
    return outs


def rms_bwd_matmul_nt(dxo, m, g, w, scale, gate_up, tt, name, comm=None):
    T, D = dxo.shape
    K = w.shape[0]
    act = gate_up is not None
    ec = _tile(K, MXU_COLS_V7X if act else 1024, 128)

    def body(*refs):
        if act:
            dxo_ref, m_ref, g_ref, w_ref, gate_ref, up_ref, dm_ref, dg_ref, dgate_ref, dup_ref = refs
        else:
            dxo_ref, m_ref, g_ref, w_ref, dm_ref, dg_ref, da_ref = refs

        @pl.when(pl.program_id(0) == 0)
        def _():
            dg_ref[...] = jnp.zeros_like(dg_ref)

        dy = scale * dxo_ref[...]
        mv = m_ref[...]
        r = _rstd(mv)
        mh = mv * r
        dg_ref[...] += _sum8(dy * mh)
        dmh = dy * g_ref[...]
        dm = (r * (dmh - mh * jnp.mean(dmh * mh, axis=-1, keepdims=True))).astype(BF16)
        dm_ref[...] = dm
        for j in range(K // ec):
            cols = slice(j * ec, (j + 1) * ec)
            da = lax.dot_general(dm, w_ref[cols, :], NT, preferred_element_type=F32)
            if act:
                gate = gate_ref[:, cols].astype(F32)
                sg = _sigmoid(gate)
                dup_ref[:, cols] = (da * (gate * sg)).astype(BF16)
                dgate_ref[:, cols] = (da * up_ref[:, cols].astype(F32) * (sg * (1.0 + gate * (1.0 - sg)))).astype(BF16)
            else:
                da_ref[:, cols] = da.astype(BF16)

    row = pl.BlockSpec((tt, D), lambda i: (i, 0))
    col = pl.BlockSpec((tt, K), lambda i: (i, 0))
    n_col = 2 if act else 1
    return _token_call(
        body,
        name,
        T // tt,
        [row, row, _resident((1, D)), _resident((K, D))] + ([col, col] if act else []),
        [dxo, m, g, w, *(gate_up or ())],
        [row, pl.BlockSpec((8, D), lambda i: (0, 0))] + [col] * n_col,
        [jax.ShapeDtypeStruct((T, D), BF16), jax.ShapeDtypeStruct((8, D), F32)] + [jax.ShapeDtypeStruct((T, K), BF16)] * n_col,
        [],
        comm,
    )


def matmul_rms_bwd(dxo, x, g, pairs, tt, name):
    T, D = x.shape
    N = pairs[0][0].shape[1]
    npairs = len(pairs)

    def body(*refs):
        dxo_ref, x_ref, g_ref = refs[:3]
        dz_refs = refs[3 : 3 + npairs]
        wt_refs = refs[3 + npairs : 3 + 2 * npairs]
        dx_ref, h_ref, dg_ref = refs[3 + 2 * npairs :]

        @pl.when(pl.program_id(0) == 0)
        def _():
            dg_ref[...] = jnp.zeros_like(dg_ref)

        dh = None
        for dz_ref, wt_ref in zip(dz_refs, wt_refs):
            part = jnp.dot(dz_ref[...], wt_ref[...], preferred_element_type=F32)
            dh = part if dh is None else dh + part
        xv = x_ref[...]
        gv = g_ref[...]
        r = _rstd(xv)
        xh = xv * r
        h_ref[...] = (xh * gv).astype(BF16)
        dg_ref[...] += _sum8(dh * xh)
        dxh = dh * gv
        dx_ref[...] = dxo_ref[...] + r * (dxh - xh * jnp.mean(dxh * xh, axis=-1, keepdims=True))

    row = pl.BlockSpec((tt, D), lambda i: (i, 0))
    outs, _ = _token_call(
        body,
        name,
        T // tt,
        [row, row, _resident((1, D))] + [pl.BlockSpec((tt, N), lambda i: (i, 0))] * npairs + [_resident((N, D))] * npairs,
        [dxo, x, g, *[p[0] for p in pairs], *[p[1] for p in pairs]],
        [row, row, pl.BlockSpec((8, D), lambda i: (0, 0))],
        [jax.ShapeDtypeStruct((T, D), F32), jax.ShapeDtypeStruct((T, D), BF16), jax.ShapeDtypeStruct((8, D), F32)],
        [],
        None,
    )
    return outs


def wgrad_tn(a, b, tk, name):
    T, M = a.shape
    N = b.shape[1]
    tm = M if M * N * 4 <= WGRAD_ACC_BYTES else M // 2
    nt = T // tk

    def body(a_ref, b_ref, o_ref, acc_ref):
        k = pl.program_id(1)

        @pl.when(k == 0)
        def _():
            acc_ref[...] = jnp.zeros_like(acc_ref)

        acc_ref[...] += lax.dot_general(a_ref[...], b_ref[...], TN, preferred_element_type=F32)

        @pl.when(k == nt - 1)
        def _():
            o_ref[...] = acc_ref[...].astype(BF16)

    return _pcall(
        body,
        name=name,
        grid=(M // tm, nt),
        in_specs=[pl.BlockSpec((tk, tm), lambda i, k: (k, i)), pl.BlockSpec((tk, N), lambda i, k: (k, 0))],
        out_specs=pl.BlockSpec((tm, N), lambda i, k: (i, 0)),
        out_shape=jax.ShapeDtypeStruct((M, N), BF16),
        scratch_shapes=[pltpu.VMEM((tm, N), F32)],
        compiler_params=_cparams(2),
    )(a, b)


def _sgu_recompute(z_ref, lng_ref, lnb_ref, S):
    z = z_ref[...].astype(F32)
    zu, zv = z[:, :S], z[:, S:]
    tu, tv = _gelu_tanh(zu), _gelu_tanh(zv)
    u = 0.5 * zu * (1.0 + tu)
    v = 0.5 * zv * (1.0 + tv)
    vc = v - jnp.mean(v, axis=-1, keepdims=True)
    rstd = lax.rsqrt(jnp.mean(vc * vc, axis=-1, keepdims=True) + EPS)
    vhat = vc * rstd
    vln = (vhat * lng_ref[...] + lnb_ref[...]).astype(BF16)
    return zu, zv, tu, tv, u, vhat, rstd, vln


def sgu_mix_fwd(zpre, lng, lnb, ws, bias_full, tt, name, comm=None):
    T, S2 = zpre.shape
    S = S2 // 2
    dg = S // GROUPS

    def body(z_ref, lng_ref, lnb_ref, ws_ref, bias_ref, o_ref):
        _, _, _, _, u, _, _, vln = _sgu_recompute(z_ref, lng_ref, lnb_ref, S)
        for c in range(tt // CHUNK):
            rows = slice(c * CHUNK, (c + 1) * CHUNK)
            for gi in range(GROUPS):
                cols = slice(gi * dg, (gi + 1) * dg)
                mixed = jnp.dot(ws_ref[gi], vln[rows, cols], preferred_element_type=F32) + bias_ref[:, cols]
                o_ref[rows, cols] = (u[rows, cols] * mixed).astype(BF16)

    vec = pl.BlockSpec((1, S), lambda i: (0, 0))
    return _token_call(
        body,
        name,
        T // tt,
        [
            pl.BlockSpec((tt, S2), lambda i: (i, 0)),
            vec,
            vec,
            pl.BlockSpec((GROUPS, CHUNK, CHUNK), lambda i: (0, 0, 0)),
            pl.BlockSpec((CHUNK, S), lambda i: (0, 0)),
        ],
        [zpre, lng, lnb, ws, bias_full],
        [pl.BlockSpec((tt, S), lambda i: (i, 0))],
        [jax.ShapeDtypeStruct((T, S), BF16)],
        [],
        comm,
    )


def sgu_mix_bwd(dgated, zpre, lng, lnb, ws, ws_t, bias_full, tt, name, comm=None):
    T, S2 = zpre.shape
    S = S2 // 2
    dg = S // GROUPS

    def body(dgt_ref, z_ref, lng_ref, lnb_ref, ws_ref, wst_ref, bias_ref, dz_ref, dws_ref, dbias_ref, dlng_ref, dlnb_ref, du_ref, dvln_ref):
        @pl.when(pl.program_id(0) == 0)
        def _():
            dws_ref[...] = jnp.zeros_like(dws_ref)
            dbias_ref[...] = jnp.zeros_like(dbias_ref)
            dlng_ref[...] = jnp.zeros_like(dlng_ref)
            dlnb_ref[...] = jnp.zeros_like(dlnb_ref)

        zu, zv, tu, tv, u, vhat, rstd, vln = _sgu_recompute(z_ref, lng_ref, lnb_ref, S)
        for c in range(tt // CHUNK):
            rows = slice(c * CHUNK, (c + 1) * CHUNK)
            for gi in range(GROUPS):
                cols = slice(gi * dg, (gi + 1) * dg)
                v_cg = vln[rows, cols]
                mixed = jnp.dot(ws_ref[gi], v_cg, preferred_element_type=F32) + bias_ref[:, cols]
                dgt = dgt_ref[rows, cols].astype(F32)
                du_ref[rows, cols] = dgt * mixed
                dmx = dgt * u[rows, cols]
                dbias_ref[:, cols] += dmx
                dmx16 = dmx.astype(BF16)
                dws_ref[gi] += lax.dot_general(dmx16, v_cg, NT, preferred_element_type=F32)
                dvln_ref[rows, cols] = jnp.dot(wst_ref[gi], dmx16, preferred_element_type=F32)

        dvln = dvln_ref[...]
        dlng_ref[...] += _sum8(dvln * vhat)
        dlnb_ref[...] += _sum8(dvln)
        dvhat = dvln * lng_ref[...]
        dv = rstd * (dvhat - jnp.mean(dvhat, axis=-1, keepdims=True) - vhat * jnp.mean(dvhat * vhat, axis=-1, keepdims=True))
        dz_ref[:, :S] = (du_ref[...] * _gelu_grad(zu, tu)).astype(BF16)
        dz_ref[:, S:] = (dv * _gelu_grad(zv, tv)).astype(BF16)

    vec = pl.BlockSpec((1, S), lambda i: (0, 0))
    wsp = pl.BlockSpec((GROUPS, CHUNK, CHUNK), lambda i: (0, 0, 0))
    full = pl.BlockSpec((CHUNK, S), lambda i: (0, 0))
    acc8 = pl.BlockSpec((8, S), lambda i: (0, 0))
    return _token_call(
        body,
        name,
        T // tt,
        [pl.BlockSpec((tt, S), lambda i: (i, 0)), pl.BlockSpec((tt, S2), lambda i: (i, 0)), vec, vec, wsp, wsp, full],
        [dgated, zpre, lng, lnb, ws, ws_t, bias_full],
        [pl.BlockSpec((tt, S2), lambda i: (i, 0)), wsp, full, acc8, acc8],
        [
            jax.ShapeDtypeStruct((T, S2), BF16),
            jax.ShapeDtypeStruct((GROUPS, CHUNK, CHUNK), F32),
            jax.ShapeDtypeStruct((CHUNK, S), F32),
            jax.ShapeDtypeStruct((8, S), F32),
            jax.ShapeDtypeStruct((8, S), F32),
        ],
        [pltpu.VMEM((tt, S), F32), pltpu.VMEM((tt, S), F32)],
        comm,
    )


def _shifted_planes(sh, tt):
    n = tt + HALO - SUBLANES
    for s in range(1, SUBLANES):
        sh[s, 0:n, :] = sh[0, pl.ds(s, n), :]


def conv_mid_fwd(p, wdw, bdw, lng, lnb, tt, name, comm=None):
    T, C2 = p.shape
    C = C2 // 2

    def body(p_ref, w_ref, b_ref, lng_ref, lnb_ref, yc_ref, ys_ref, sh):
        @pl.when(pl.program_id(0) == 0)
        def _():
            sh[0, 0:HALO, :] = jnp.zeros((HALO, C), F32)

        @pl.when(pl.program_id(0) > 0)
        def _():
            sh[0, 0:HALO, :] = sh[0, tt : tt + HALO, :]

        pv = p_ref[...].astype(F32)
        sh[0, HALO:, :] = pv[:, :C] * _sigmoid(pv[:, C:])
        _shifted_planes(sh, tt)
        for lb in range(C // LANES):
            lanes = slice(lb * LANES, (lb + 1) * LANES)
            wk = [jnp.broadcast_to(w_ref[k : k + 1, lanes], (SUBLANES, LANES)) for k in range(CONV_W)]
            bias = jnp.broadcast_to(b_ref[:, lanes], (SUBLANES, LANES))

            def rows(i, carry):
                r0 = pl.multiple_of(i * SUBLANES, SUBLANES)
                acc = bias
                for k in range(CONV_W):
                    o = HALO - (CONV_W - 1) + k
                    acc = acc + wk[k] * sh[o % SUBLANES, pl.ds(r0 + (o // SUBLANES) * SUBLANES, SUBLANES), lanes]
                yc_ref[pl.ds(r0, SUBLANES), lanes] = acc
                return carry

            lax.fori_loop(0, tt // SUBLANES, rows, 0, unroll=2)
        acc = yc_ref[...]
        yc = acc - jnp.mean(acc, axis=-1, keepdims=True)
        yn = yc * lax.rsqrt(jnp.mean(yc * yc, axis=-1, keepdims=True) + EPS) * lng_ref[...] + lnb_ref[...]
        ys_ref[...] = (yn * _sigmoid(yn)).astype(BF16)

    vec = pl.BlockSpec((1, C), lambda i: (0, 0))
    row = pl.BlockSpec((tt, C), lambda i: (i, 0))
    return _token_call(
        body,
        name,
        T // tt,
        [pl.BlockSpec((tt, C2), lambda i: (i, 0)), pl.BlockSpec((HALO, C), lambda i: (0, 0)), vec, vec, vec],
        [p, wdw, bdw, lng, lnb],
        [row, row],
        [jax.ShapeDtypeStruct((T, C), F32), jax.ShapeDtypeStruct((T, C), BF16)],
        [pltpu.VMEM((SUBLANES, tt + HALO, C), F32)],
        comm,
    )


def conv_mid_bwd(dys, yc, p, wdw, lng, lnb, tt, name, comm=None):
    T, C2 = p.shape
    C = C2 // 2
    n = T // tt

    def body(dys_ref, yc_ref, p_ref, w_ref, lng_ref, lnb_ref, dp_ref, dw_ref, db_ref, dlng_ref, dlnb_ref, sh, y_s, dy_s):
        @pl.when(pl.program_id(0) == 0)
        def _():
            sh[0, tt : tt + HALO, :] = jnp.zeros((HALO, C), F32)
            dw_ref[...] = jnp.zeros_like(dw_ref)
            db_ref[...] = jnp.zeros_like(db_ref)
            dlng_ref[...] = jnp.zeros_like(dlng_ref)
            dlnb_ref[...] = jnp.zeros_like(dlnb_ref)

        @pl.when(pl.program_id(0) > 0)
        def _():
            sh[0, tt : tt + HALO, :] = sh[0, 0:HALO, :]

        ycv = yc_ref[...]
        ycc = ycv - jnp.mean(ycv, axis=-1, keepdims=True)
        rstd = lax.rsqrt(jnp.mean(ycc * ycc, axis=-1, keepdims=True) + EPS)
        yhat = ycc * rstd
        lng_v = lng_ref[...]
        yn = yhat * lng_v + lnb_ref[...]
        sg = _sigmoid(yn)
        dyn = dys_ref[...].astype(F32) * (sg * (1.0 + yn * (1.0 - sg)))
        dlng_ref[...] += _sum8(dyn * yhat)
        dlnb_ref[...] += _sum8(dyn)
        dyh = dyn * lng_v
        dyc = rstd * (dyh - jnp.mean(dyh, axis=-1, keepdims=True) - yhat * jnp.mean(dyh * yhat, axis=-1, keepdims=True))
        db_ref[...] += _sum8(dyc)
        sh[0, 0:tt, :] = dyc
        _shifted_planes(sh, tt)

        pv = p_ref[...].astype(F32)
        a = pv[:, :C]
        sgate = _sigmoid(pv[:, C:])
        y_s[...] = a * sgate
        for lb in range(C // LANES):
            lanes = slice(lb * LANES, (lb + 1) * LANES)
            for k0, k1 in ((0, CONV_W // 2), (CONV_W // 2, CONV_W)):
                wk = [jnp.broadcast_to(w_ref[k : k + 1, lanes], (SUBLANES, LANES)) for k in range(k0, k1)]

                def rows(i, dw_acc, k0=k0, k1=k1, wk=wk, lanes=lanes):
                    r0 = pl.multiple_of(i * SUBLANES, SUBLANES)
                    yb = y_s[pl.ds(r0, SUBLANES), lanes]
                    acc = jnp.zeros((SUBLANES, LANES), F32) if k0 == 0 else dy_s[pl.ds(r0, SUBLANES), lanes]
                    new = []
                    for n_k, k in enumerate(range(k0, k1)):
                        o = CONV_W - 1 - k
                        blk = sh[o % SUBLANES, pl.ds(r0 + (o // SUBLANES) * SUBLANES, SUBLANES), lanes]
                        acc = acc + wk[n_k] * blk
                        new.append(dw_acc[n_k] + yb * blk)
                    dy_s[pl.ds(r0, SUBLANES), lanes] = acc
                    return tuple(new)

                dw_acc = lax.fori_loop(0, tt // SUBLANES, rows, tuple(jnp.zeros((SUBLANES, LANES), F32) for _ in range(k0, k1)), unroll=2)
                for n_k, k in enumerate(range(k0, k1)):
                    dw_ref[SUBLANES * k : SUBLANES * (k + 1), lanes] += dw_acc[n_k]
        dy = dy_s[...]
        dp_ref[:, :C] = (dy * sgate).astype(BF16)
        dp_ref[:, C:] = (dy * a * sgate * (1.0 - sgate)).astype(BF16)

    vec = pl.BlockSpec((1, C), lambda i: (0, 0))
    row = pl.BlockSpec((tt, C), lambda i: (n - 1 - i, 0))
    row2 = pl.BlockSpec((tt, C2), lambda i: (n - 1 - i, 0))
    acc8 = pl.BlockSpec((8, C), lambda i: (0, 0))
    return _token_call(
        body,
        name,
        n,
        [row, row, row2, pl.BlockSpec((HALO, C), lambda i: (0, 0)), vec, vec],
        [dys, yc, p, wdw, lng, lnb],
        [row2, pl.BlockSpec((HALO * 8, C), lambda i: (0, 0)), acc8, acc8, acc8],
        [
            jax.ShapeDtypeStruct((T, C2), BF16),
            jax.ShapeDtypeStruct((HALO * 8, C), F32),
            jax.ShapeDtypeStruct((8, C), F32),
            jax.ShapeDtypeStruct((8, C), F32),
            jax.ShapeDtypeStruct((8, C), F32),
        ],
        [pltpu.VMEM((SUBLANES, tt + HALO, C), F32), pltpu.VMEM((tt, C), F32), pltpu.VMEM((tt, C), F32)],
        comm,
    )


def loss_head(y, target, tt, name):
    T, D = y.shape

    def body(y_ref, t_ref, dy_ref, sq_ref):
        @pl.when(pl.program_id(0) == 0)
        def _():
            sq_ref[...] = jnp.zeros_like(sq_ref)

        err = y_ref[...] - t_ref[...]
        dy_ref[...] = err * (1.0 / D)
        sq_ref[...] += _sum8(err * err)

    row = pl.BlockSpec((tt, D), lambda i: (i, 0))
    return _pcall(
        body,
        name=name,
        grid=(T // tt,),
        in_specs=[row, row],
        out_specs=[row, pl.BlockSpec((8, D), lambda i: (0, 0))],
        out_shape=[jax.ShapeDtypeStruct((T, D), F32), jax.ShapeDtypeStruct((8, D), F32)],
        compiler_params=_cparams(1),
    )(y, target)


def sum_slots(slots, name):
    _, r, w = slots.shape

    def body(s_ref, o_ref):
        total = s_ref[0].astype(F32)
        for p in range(1, N_DEV):
            total = total + s_ref[p].astype(F32)
        o_ref[...] = total

    return _pcall(body, name=name, out_shape=jax.ShapeDtypeStruct((r, w), F32), compiler_params=pltpu.CompilerParams(vmem_limit_bytes=VMEM_LIMIT_V7X))(slots)


def adamw(w, g, m, v, name):
    R, C = w.shape
    tr = _tile(R, 512, 8)
    c1 = 1.0 - ADAM_B1**ADAM_STEP
    c2 = 1.0 - ADAM_B2**ADAM_STEP

    def body(w_ref, g_ref, m_ref, v_ref, d_ref, mo_ref, vo_ref):
        gv = g_ref[...]
        m2 = ADAM_B1 * m_ref[...] + (1.0 - ADAM_B1) * gv
        v2 = ADAM_B2 * v_ref[...] + (1.0 - ADAM_B2) * (gv * gv)
        mo_ref[...] = m2
        vo_ref[...] = v2
        d_ref[...] = -ADAM_LR * ((m2 / c1) / (jnp.sqrt(v2 / c2) + ADAM_EPS) + ADAM_WD * w_ref[...])

    blk = pl.BlockSpec((tr, C), lambda i: (i, 0))
    return _pcall(
        body,
        name=name,
        grid=(R // tr,),
        in_specs=[blk] * 4,
        out_specs=[blk] * 3,
        out_shape=[jax.ShapeDtypeStruct((R, C), F32)] * 3,
        compiler_params=_cparams(1),
    )(w, g, m, v)


def _sublayers(depth):
    out = []
    for layer in range(depth):
        out.append(("ffn", layer, 0))
        out.append(("sgu" if layer % N_MIXERS == 0 else "conv", layer, layer // N_MIXERS))
        out.append(("ffn", layer, 1))
    return out


def _row_blocks(sub, shards):
    kind, layer, idx = sub
    if kind == "ffn":
        blk = {"gate_t": shards["ff_w_gate"][layer, idx].T, "up_t": shards["ff_w_up"][layer, idx].T, "down": shards["ff_w_down"][layer, idx]}
    elif kind == "sgu":
        blk = {"in_t": shards["sgu_w_in"][idx].T, "out": shards["sgu_w_out"][idx]}
    else:
        blk = {"pw1_t": shards["conv_w_pw1"][idx].T, "pw2": shards["conv_w_pw2"][idx]}
    return {nm: arr.astype(BF16) for nm, arr in blk.items()}


def _pad8(a):
    return jnp.pad(a, ((0, (-a.shape[0]) % 8), (0, 0)))


def kernel(x, norm_g, ff_w_gate, ff_w_up, ff_w_down, sgu_w_in, sgu_ln_g, sgu_ln_b, sgu_w_spatial, sgu_b_spatial, sgu_w_out, conv_w_pw1, conv_w_dw, conv_b_dw, conv_ln_g, conv_ln_b, conv_w_pw2, loss_target, m_norm_g, m_ff_w_gate, m_ff_w_up, m_ff_w_down, m_sgu_w_in, m_sgu_ln_g, m_sgu_ln_b, m_sgu_w_spatial, m_sgu_b_spatial, m_sgu_w_out, m_conv_w_pw1, m_conv_w_dw, m_conv_b_dw, m_conv_ln_g, m_conv_ln_b, m_conv_w_pw2, v_norm_g, v_ff_w_gate, v_ff_w_up, v_ff_w_down, v_sgu_w_in, v_sgu_ln_g, v_sgu_ln_b, v_sgu_w_spatial, v_sgu_b_spatial, v_sgu_w_out, v_conv_w_pw1, v_conv_w_dw, v_conv_b_dw, v_conv_ln_g, v_conv_ln_b, v_conv_w_pw2):
    names = ["norm_g", "ff_w_gate", "ff_w_up", "ff_w_down", "sgu_w_in", "sgu_ln_g", "sgu_ln_b", "sgu_w_spatial", "sgu_b_spatial", "sgu_w_out", "conv_w_pw1", "conv_w_dw", "conv_b_dw", "conv_ln_g", "conv_ln_b", "conv_w_pw2"]
    weights = dict(zip(names, [norm_g, ff_w_gate, ff_w_up, ff_w_down, sgu_w_in, sgu_ln_g, sgu_ln_b, sgu_w_spatial, sgu_b_spatial, sgu_w_out, conv_w_pw1, conv_w_dw, conv_b_dw, conv_ln_g, conv_ln_b, conv_w_pw2]))
    moments_m = dict(zip(names, [m_norm_g, m_ff_w_gate, m_ff_w_up, m_ff_w_down, m_sgu_w_in, m_sgu_ln_g, m_sgu_ln_b, m_sgu_w_spatial, m_sgu_b_spatial, m_sgu_w_out, m_conv_w_pw1, m_conv_w_dw, m_conv_b_dw, m_conv_ln_g, m_conv_ln_b, m_conv_w_pw2]))
    moments_v = dict(zip(names, [v_norm_g, v_ff_w_gate, v_ff_w_up, v_ff_w_down, v_sgu_w_in, v_sgu_ln_g, v_sgu_ln_b, v_sgu_w_spatial, v_sgu_b_spatial, v_sgu_w_out, v_conv_w_pw1, v_conv_w_dw, v_conv_b_dw, v_conv_ln_g, v_conv_ln_b, v_conv_w_pw2]))

    _, T, D = x.shape
    depth = norm_g.shape[0]
    n_conv = conv_w_dw.shape[0]
    n_sgu = sgu_w_in.shape[0]
    S = sgu_ln_g.shape[1]
    lanes = norm_g.shape[2]
    subs = _sublayers(depth)
    n_sub = len(subs)

    cx, cy, cc = (lax.axis_index(a) for a in MESH_AXES)
    my_block = 4 * cx + 2 * cy + cc

    blocks = [_row_blocks(sub, weights) for sub in subs]
    dw_pad = jnp.pad(conv_w_dw, ((0, 0), (0, HALO - CONV_W), (0, 0)))
    small_parts = [_pad8(p) for p in (norm_g.reshape(-1, lanes), dw_pad.reshape(-1, lanes), conv_b_dw, conv_ln_g, conv_ln_b)]
    small_rows = [p.shape[0] for p in small_parts]
    small = jnp.concatenate(small_parts, axis=0)

    def gathered(block_names, outs):
        return {nm: o.reshape(N_DEV * o.shape[1], o.shape[2]) for nm, o in zip(block_names, outs)}

    first = exchange([("gather", a) for a in blocks[0].values()] + [("gather", small)], name="gather_first")
    W = [None] * n_sub
    W[0] = gathered(blocks[0].keys(), first[:-1])
    gsmall = first[-1]

    small_full = jnp.transpose(gsmall, (1, 0, 2)).reshape(gsmall.shape[1], N_DEV * lanes)
    so = [0]
    for r in small_rows:
        so.append(so[-1] + r)
    norm_full = small_full[so[0] : so[0] + depth * norm_g.shape[1]].reshape(depth, -1, D)
    dw_full = small_full[so[1] : so[1] + n_conv * HALO].reshape(n_conv, HALO, D)
    bdw_full, clng_full, clnb_full = (small_full[so[k] : so[k] + n_conv] for k in (2, 3, 4))

    causal = jnp.tril(jnp.ones((CHUNK, CHUNK), dtype=bool))
    ws_all = jnp.where(causal[None, None], sgu_w_spatial, 0.0).astype(BF16)
    wst_all = jnp.swapaxes(ws_all, -1, -2)
    bias_full_all = jnp.repeat(jnp.swapaxes(sgu_b_spatial, -1, -2), S // GROUPS, axis=-1)

    tt = _tile(T, 512, CHUNK)
    tt_mix = _tile(T, 256, CHUNK)

    tk = _tile(T, 1024, CHUNK)

    def vec(v):
        return v.reshape(1, -1)

    def split_first(comm):
        return (comm[:1], comm[1:]) if comm else (None, None)

    def norms(kind, layer, idx):
        pre = 4 * idx if kind == "ffn" else 2
        return vec(norm_full[layer, pre]), vec(norm_full[layer, pre + 1])

    xs = x[0]
    saved = []
    for si, (kind, layer, idx) in enumerate(subs):
        w = W[si]
        g_pre, g_post = norms(kind, layer, idx)
        nxt = [("gather", a) for a in blocks[si + 1].values()] if si + 1 < n_sub else None
        nxt_first, nxt_rest = split_first(nxt)
        if kind == "ffn":
            (gate, up), got = rms_matmul_nt(xs, g_pre, [w["gate_t"], w["up_t"]], tt, "ffn_in", nxt)
            x_new, o, a = matmul_rms_res(xs, (gate, up), w["down"], g_post, FFN_SCALE, tt, "ffn_out")
            saved.append((xs, gate, up, o, a))
        elif kind == "sgu":
            (zpre,), got = rms_matmul_nt(xs, g_pre, [w["in_t"]], tt, "sgu_in", nxt_first)
            (gated,), got_rest = sgu_mix_fwd(zpre, vec(sgu_ln_g[idx]), vec(sgu_ln_b[idx]), ws_all[idx], bias_full_all[idx], tt_mix, "sgu_mix", nxt_rest)
            got = list(got) + list(got_rest)
            x_new, mm = matmul_rms_res(xs, (gated,), w["out"], g_post, 1.0, tt, "sgu_out")
            saved.append((xs, zpre, gated, mm))
        else:
            (p,), got = rms_matmul_nt(xs, g_pre, [w["pw1_t"]], tt, "conv_pw1", nxt_first)
            (yc, ys), got_rest = conv_mid_fwd(p, dw_full[idx], vec(bdw_full[idx]), vec(clng_full[idx]), vec(clnb_full[idx]), tt, "conv_mid", nxt_rest)
            got = list(got) + list(got_rest)
            x_new, mm = matmul_rms_res(xs, (ys,), w["pw2"], g_post, 1.0, tt, "conv_pw2")
            saved.append((xs, p, yc, ys, mm))
        if nxt:
            W[si + 1] = gathered(blocks[si + 1].keys(), got)
        xs = x_new

    dx, sq = loss_head(xs, loss_target[0], tt, name="loss_head")
    loss = lax.psum(0.5 * jnp.sum(sq) / D, MESH_AXES)

    d_norm = [[None] * norm_full.shape[1] for _ in range(depth)]
    d_sgu = [None] * n_sgu
    d_conv = [None] * n_conv
    slots = [None] * n_sub
    pending = None

    def scatter_of(p):
        return [("scatter", dwm.reshape(N_DEV, dwm.shape[0] // N_DEV, D)) for dwm in p[1].values()] if p else None

    for si in reversed(range(n_sub)):
        kind, layer, idx = subs[si]
        w = W[si]
        g_pre, g_post = norms(kind, layer, idx)
        pre = 4 * idx if kind == "ffn" else 2
        sc_first, sc_rest = split_first(scatter_of(pending))
        if kind == "ffn":
            xs_in, gate, up, o, a = saved[si]
            (do, dg_post, dgate, dup), got = rms_bwd_matmul_nt(dx, o, g_post, w["down"], FFN_SCALE, (gate, up), tt, "ffn_out_bwd", scatter_of(pending))
            dx, h, dg_pre = matmul_rms_bwd(dx, xs_in, g_pre, [(dgate, w["gate_t"]), (dup, w["up_t"])], tt, "ffn_in_bwd")
            dws_now = {"gate_t": wgrad_tn(dgate, h, tk, name="wgrad_ffn_in"), "up_t": wgrad_tn(dup, h, tk, name="wgrad_ffn_in"), "down": wgrad_tn(a, do, tk, name="wgrad_ffn_out")}
        elif kind == "sgu":
            xs_in, zpre, gated, mm = saved[si]
            (dm, dg_post, dgated), got = rms_bwd_matmul_nt(dx, mm, g_post, w["out"], 1.0, None, tt, "sgu_out_bwd", sc_first)
            (dzpre, dws, dbias, dlng, dlnb), got_rest = sgu_mix_bwd(dgated, zpre, vec(sgu_ln_g[idx]), vec(sgu_ln_b[idx]), ws_all[idx], wst_all[idx], bias_full_all[idx], tt_mix, "sgu_mix_bwd", sc_rest)
            got = list(got) + list(got_rest)
            dx, h, dg_pre = matmul_rms_bwd(dx, xs_in, g_pre, [(dzpre, w["in_t"])], tt, "sgu_in_bwd")
            dws_now = {"in_t": wgrad_tn(dzpre, h, tt, name="wgrad_sgu_in"), "out": wgrad_tn(gated, dm, tk, name="wgrad_sgu_out")}
            d_sgu[idx] = (dws, dbias, dlng, dlnb)
        else:
            xs_in, p, yc, ys, mm = saved[si]
            (dm, dg_post, dys), got = rms_bwd_matmul_nt(dx, mm, g_post, w["pw2"], 1.0, None, tt, "conv_pw2_bwd", sc_first)
            (dp, dwdw, dbdw, dlng, dlnb), got_rest = conv_mid_bwd(dys, yc, p, dw_full[idx], vec(clng_full[idx]), vec(clnb_full[idx]), tt, "conv_mid_bwd", sc_rest)
            got = list(got) + list(got_rest)
            dx, h, dg_pre = matmul_rms_bwd(dx, xs_in, g_pre, [(dp, w["pw1_t"])], tt, "conv_pw1_bwd")
            dws_now = {"pw1_t": wgrad_tn(dp, h, tk, name="wgrad_conv_pw1"), "pw2": wgrad_tn(ys, dm, tk, name="wgrad_conv_pw2")}
            d_conv[idx] = (dwdw, dbdw, dlng, dlnb)
        d_norm[layer][pre], d_norm[layer][pre + 1] = dg_pre, dg_post
        if pending:
            slots[pending[0]] = dict(zip(pending[1].keys(), got))
        pending = (si, dws_now)
    grad_x = dx[None]

    def sum8(v):
        return jnp.sum(v, axis=0)

    g_norm = jnp.stack([jnp.stack([sum8(d) for d in row]) for row in d_norm])
    g_dw = jnp.stack([jnp.sum(d[0].reshape(HALO, 8, D), axis=1) for d in d_conv])
    g_bdw, g_clng, g_clnb = (jnp.stack([sum8(d[k]) for d in d_conv]) for k in (1, 2, 3))
    g_slng, g_slnb = (jnp.stack([sum8(d[k]) for d in d_sgu]) for k in (2, 3))
    g_bsp = jnp.stack([jnp.sum(d[1].reshape(CHUNK, GROUPS, S // GROUPS), axis=-1).T for d in d_sgu])
    g_wsp = jnp.stack([jnp.where(causal[None], d[0], 0.0) for d in d_sgu])
    sparts = [g_norm, g_dw, g_bdw, g_clng, g_clnb, g_slng, g_slnb, g_bsp, g_wsp]
    srows = [p.size // D for p in sparts]
    sgrad = jnp.concatenate([_pad8(p.reshape(-1, D)) for p in sparts], axis=0)

    last = exchange(scatter_of(pending) + [("gather", sgrad)], name="scatter_last")
    slots[pending[0]] = dict(zip(pending[1].keys(), last[:-1]))
    stotal = sum_slots(last[-1], name="sum_slots")

    gs = {(si, nm): sum_slots(s, name="sum_slots") for si in range(n_sub) for nm, s in slots[si].items()}
    grads = {}
    ffn_si = {(layer, idx): si for si, (kind, layer, idx) in enumerate(subs) if kind == "ffn"}
    sgu_si = {idx: si for si, (kind, layer, idx) in enumerate(subs) if kind == "sgu"}
    conv_si = {idx: si for si, (kind, layer, idx) in enumerate(subs) if kind == "conv"}
    grads["ff_w_gate"] = jnp.stack([jnp.stack([gs[(ffn_si[(l, f)], "gate_t")].T for f in range(2)]) for l in range(depth)])
    grads["ff_w_up"] = jnp.stack([jnp.stack([gs[(ffn_si[(l, f)], "up_t")].T for f in range(2)]) for l in range(depth)])
    grads["ff_w_down"] = jnp.stack([jnp.stack([gs[(ffn_si[(l, f)], "down")] for f in range(2)]) for l in range(depth)])
    grads["sgu_w_in"] = jnp.stack([gs[(sgu_si[j], "in_t")].T for j in range(n_sgu)])
    grads["sgu_w_out"] = jnp.stack([gs[(sgu_si[j], "out")] for j in range(n_sgu)])
    grads["conv_w_pw1"] = jnp.stack([gs[(conv_si[j], "pw1_t")].T for j in range(n_conv)])
    grads["conv_w_pw2"] = jnp.stack([gs[(conv_si[j], "pw2")] for j in range(n_conv)])

    so = [0]
    for r in srows:
        so.append(so[-1] + r + (-r) % 8)
    sp = [stotal[so[k] : so[k] + srows[k]] for k in range(len(sparts))]

    def my_lanes(v):
        return lax.dynamic_slice_in_dim(v, my_block * lanes, lanes, axis=-1)

    grads["norm_g"] = my_lanes(sp[0].reshape(g_norm.shape))
    grads["conv_w_dw"] = my_lanes(sp[1].reshape(g_dw.shape))[:, :CONV_W]
    grads["conv_b_dw"] = my_lanes(sp[2])
    grads["conv_ln_g"] = my_lanes(sp[3])
    grads["conv_ln_b"] = my_lanes(sp[4])
    grads["sgu_ln_g"] = sp[5].reshape(g_slng.shape)
    grads["sgu_ln_b"] = sp[6].reshape(g_slnb.shape)
    grads["sgu_b_spatial"] = sp[7].reshape(g_bsp.shape)
    grads["sgu_w_spatial"] = sp[8].reshape(g_wsp.shape)

    deltas, new_m, new_v = {}, {}, {}
    for nm in names:
        w = weights[nm]
        two_d = (-1, w.shape[-1])
        d, m2, v2 = adamw(w.reshape(two_d), grads[nm].reshape(two_d), moments_m[nm].reshape(two_d), moments_v[nm].reshape(two_d), name="adamw")
        deltas[nm], new_m[nm], new_v[nm] = d.reshape(w.shape), m2.reshape(w.shape), v2.reshape(w.shape)

    return (loss, grad_x, *[grads[n] for n in names], *[deltas[n] for n in names], *[new_m[n] for n in names], *[new_v[n] for n in names])
```

```python
import jax
import jax.numpy as jnp
from jax import lax
from jax.experimental import pallas as pl
from jax.experimental.pallas import tpu as pltpu

F32 = jnp.float32
BF16 = jnp.bfloat16

EPS = 1e-6
FFN_SCALE = 0.5
N_MIXERS = 2
CHUNK = 128
GROUPS = 8
CONV_W = 31
HALO = 32
GELU_C0 = 0.7978845608028654
GELU_C1 = 0.044715
ADAM_LR, ADAM_B1, ADAM_B2, ADAM_EPS, ADAM_WD, ADAM_STEP = 0.001, 0.9, 0.999, 1e-08, 0.01, 10

MESH_AXES = ("x", "y", "c")
N_DEV = 8
VMEM_LIMIT_V7X = 56 * 1024 * 1024
WGRAD_ACC_BYTES = 16 * 1024 * 1024
MXU_COLS_V7X = 256
SUBLANES, LANES = 8, 128
CONV_ROWS = 32

NT = (((1,), (1,)), ((), ()))
TN = (((0,), (0,)), ((), ()))

HBM_SPEC = pl.BlockSpec(memory_space=pl.ANY)
MESH_ID = pl.DeviceIdType.MESH


def _pcall(body, **kw):
    return pl.pallas_call(body, **kw)


def _cparams(n_axes):
    return pltpu.CompilerParams(dimension_semantics=("arbitrary",) * n_axes, vmem_limit_bytes=VMEM_LIMIT_V7X)


def _tile(n, pref, align):
    if n <= pref:
        return n
    t = (pref // align) * align
    while t > align and n % t:
        t -= align
    assert n % t == 0, (n, pref, align)
    return t


def _rstd(x):
    return lax.rsqrt(jnp.mean(x * x, axis=-1, keepdims=True) + EPS)


def _sum8(v):
    t, d = v.shape
    return v.reshape(t // 8, 8, d).sum(axis=0)


def _sigmoid(x):
    return jax.nn.sigmoid(x)


def _gelu_tanh(z):
    return jnp.tanh(GELU_C0 * (z + GELU_C1 * z * z * z))


def _gelu_grad(z, t):
    return 0.5 * (1.0 + t) + 0.5 * z * (1.0 - t * t) * (GELU_C0 * (1.0 + 3.0 * GELU_C1 * z * z))


def _resident(shape):
    return pl.BlockSpec(shape, lambda i: (0,) * len(shape), pipeline_mode=pl.Buffered(1))


def _exchange_io(comm):
    n = len(comm)
    out_shape = [jax.ShapeDtypeStruct(((N_DEV,) + a.shape) if kind == "gather" else a.shape, a.dtype) for kind, a in comm]
    scratch = [pltpu.SemaphoreType.DMA((n, N_DEV - 1)), pltpu.SemaphoreType.DMA((n, N_DEV - 1)), pltpu.SemaphoreType.DMA((n,))]
    return [HBM_SPEC] * n, [HBM_SPEC] * n, out_shape, scratch


def _exchange_copies(kinds, in_refs, out_refs, send_sems, recv_sems, local_sems, with_arrivals=True):
    x, y, c = (lax.axis_index(a) for a in MESH_AXES)
    me = 4 * x + 2 * y + c
    local, sends, arrivals = [], [], []
    for a, kind in enumerate(kinds):
        src, dst = in_refs[a], out_refs[a]
        gather = kind == "gather"
        local.append(pltpu.make_async_copy(src if gather else src.at[me], dst.at[me], local_sems.at[a]))
        for k in range(N_DEV - 1):
            mask = k + 1
            px = 1 - x if mask & 4 else x
            py = 1 - y if mask & 2 else y
            pc = 1 - c if mask & 1 else c
            peer = 4 * px + 2 * py + pc
            block = src if gather else src.at[peer]
            for into, lst in ((me, sends), (peer, arrivals)):
                if lst is arrivals and not with_arrivals:
                    continue
                lst.append(
                    pltpu.make_async_remote_copy(
                        src_ref=block, dst_ref=dst.at[into], send_sem=send_sems.at[a, k], recv_sem=recv_sems.at[a, k], device_id=(px, py, pc), device_id_type=MESH_ID
                    )
                )
    return local, sends, arrivals


def _exchange_start(copies):
    local, sends, _ = copies
    for cp in local + sends:
        cp.start()


def _exchange_finish(copies):
    local, sends, arrivals = copies
    for cp in arrivals:
        cp.wait_recv()
    for cp in sends:
        cp.wait_send()
    for cp in local:
        cp.wait()


def exchange(comm, name):
    kinds = [k for k, _ in comm]
    n = len(comm)
    in_specs, out_specs, out_shape, scratch = _exchange_io(comm)

    def body(*refs):
        copies = _exchange_copies(kinds, refs[:n], refs[n : 2 * n], *refs[2 * n :])
        _exchange_start(copies)
        _exchange_finish(copies)

    return _pcall(body, name=name, in_specs=in_specs, out_specs=out_specs, out_shape=out_shape, scratch_shapes=scratch)(*[a for _, a in comm])


def _token_call(body, name, n_tiles, in_specs, inputs, out_specs, out_shape, scratch, comm):
    n_in, n_out, n_scr = len(inputs), len(out_shape), len(scratch)
    comm = comm or []
    nc = len(comm)
    full_body = body
    if nc:
        kinds = [k for k, _ in comm]
        c_in, c_out, c_shape, c_scr = _exchange_io(comm)
        in_specs, out_specs, out_shape, scratch = in_specs + c_in, out_specs + c_out, out_shape + c_shape, scratch + c_scr

        def full_body(*refs):
            ins, cins = refs[:n_in], refs[n_in : n_in + nc]
            o0 = n_in + nc
            outs, couts = refs[o0 : o0 + n_out], refs[o0 + n_out : o0 + n_out + nc]
            s0 = o0 + n_out + nc
            scr, sems = refs[s0 : s0 + n_scr], refs[s0 + n_scr :]

            @pl.when(pl.program_id(0) == 0)
            def _():
                _exchange_start(_exchange_copies(kinds, cins, couts, *sems, with_arrivals=False))

            body(*ins, *outs, *scr)

            @pl.when(pl.program_id(0) == n_tiles - 1)
            def _():
                _exchange_finish(_exchange_copies(kinds, cins, couts, *sems))

    res = _pcall(
        full_body, name=name, grid=(n_tiles,), in_specs=in_specs, out_specs=out_specs, out_shape=out_shape, scratch_shapes=scratch, compiler_params=_cparams(1)
    )(*inputs, *[a for _, a in comm])
    return res[:n_out], res[n_out:]


def rms_matmul_nt(x, g, wts, tt, name, comm=None):
    T, D = x.shape
    N = wts[0].shape[0]
    nw = len(wts)
    nc = _tile(N, 1408, 128)

    def body(*refs):
        x_ref, g_ref = refs[0], refs[1]
        w_refs, o_refs = refs[2 : 2 + nw], refs[2 + nw : 2 + 2 * nw]
        xv = x_ref[...]
        h = (xv * _rstd(xv) * g_ref[...]).astype(BF16)
        for w_ref, o_ref in zip(w_refs, o_refs):
            for j in range(N // nc):
                cols = slice(j * nc, (j + 1) * nc)
                o_ref[:, cols] = lax.dot_general(h, w_ref[cols, :], NT, preferred_element_type=F32).astype(BF16)

    return _token_call(
        body,
        name,
        T // tt,
        [pl.BlockSpec((tt, D), lambda i: (i, 0)), _resident((1, D))] + [_resident((N, D))] * nw,
        [x, g, *wts],
        [pl.BlockSpec((tt, N), lambda i: (i, 0))] * nw,
        [jax.ShapeDtypeStruct((T, N), BF16)] * nw,
        [],
        comm,
    )


def matmul_rms_res(x, a_in, w, g, scale, tt, name):
    T, D = x.shape
    K = w.shape[0]
    act = len(a_in) == 2
    ec = _tile(K, MXU_COLS_V7X, 128)

    def body(*refs):
        if act:
            x_ref, gate_ref, up_ref, w_ref, g_ref, xo_ref, m_ref, a_ref = refs
            for j in range(K // ec):
                cols = slice(j * ec, (j + 1) * ec)
                gate = gate_ref[:, cols].astype(F32)
                a_ref[:, cols] = (gate * _sigmoid(gate) * up_ref[:, cols].astype(F32)).astype(BF16)
        else:
            x_ref, a_ref, w_ref, g_ref, xo_ref, m_ref = refs
        m = jnp.dot(a_ref[...], w_ref[...], preferred_element_type=F32)
        m_ref[...] = m
        xo_ref[...] = x_ref[...] + scale * (m * _rstd(m) * g_ref[...])

    row = pl.BlockSpec((tt, D), lambda i: (i, 0))
    col = pl.BlockSpec((tt, K), lambda i: (i, 0))
    outs, _ = _token_call(
        body,
        name,
        T // tt,
        [row] + [col] * len(a_in) + [_resident((K, D)), _resident((1, D))],
        [x, *a_in, w, g],
        [row, row] + ([col] if act else []),
        [jax.ShapeDtypeStruct((T, D), F32)] * 2 + ([jax.ShapeDtypeStruct((T, K), BF16)] if act else []),
        [],
        None,
    )
    return outs


def rms_bwd_matmul_nt(dxo, m, g, w, scale, gate_up, tt, name, comm=None):
    T, D = dxo.shape
    K = w.shape[0]
    act = gate_up is not None
    ec = _tile(K, MXU_COLS_V7X if act else 1024, 128)

    def body(*refs):
        if act:
            dxo_ref, m_ref, g_ref, w_ref, gate_ref, up_ref, dm_ref, dg_ref, dgate_ref, dup_ref = refs
        else:
            dxo_ref, m_ref, g_ref, w_ref, dm_ref, dg_ref, da_ref = refs

        @pl.when(pl.program_id(0) == 0)
        def _():
            dg_ref[...] = jnp.zeros_like(dg_ref)

        dy = scale * dxo_ref[...]
        mv = m_ref[...]
        r = _rstd(mv)
        mh = mv * r
        dg_ref[...] += _sum8(dy * mh)
        dmh = dy * g_ref[...]
        dm = (r * (dmh - mh * jnp.mean(dmh * mh, axis=-1, keepdims=True))).astype(BF16)
        dm_ref[...] = dm
        for j in range(K // ec):
            cols = slice(j * ec, (j + 1) * ec)
            da = lax.dot_general(dm, w_ref[cols, :], NT, preferred_element_type=F32)
            if act:
                gate = gate_ref[:, cols].astype(F32)
                sg = _sigmoid(gate)
                dup_ref[:, cols] = (da * (gate * sg)).astype(BF16)
                dgate_ref[:, cols] = (da * up_ref[:, cols].astype(F32) * (sg * (1.0 + gate * (1.0 - sg)))).astype(BF16)
            else:
                da_ref[:, cols] = da.astype(BF16)

    row = pl.BlockSpec((tt, D), lambda i: (i, 0))
    col = pl.BlockSpec((tt, K), lambda i: (i, 0))
    n_col = 2 if act else 1
    return _token_call(
        body,
        name,
        T // tt,
        [row, row, _resident((1, D)), _resident((K, D))] + ([col, col] if act else []),
        [dxo, m, g, w, *(gate_up or ())],
        [row, pl.BlockSpec((8, D), lambda i: (0, 0))] + [col] * n_col,
        [jax.ShapeDtypeStruct((T, D), BF16), jax.ShapeDtypeStruct((8, D), F32)] + [jax.ShapeDtypeStruct((T, K), BF16)] * n_col,
        [],
        comm,
    )


def matmul_rms_bwd(dxo, x, g, pairs, tt, name, comm=None):
    T, D = x.shape
    N = pairs[0][0].shape[1]
    npairs = len(pairs)

    def body(*refs):
        dxo_ref, x_ref, g_ref = refs[:3]
        dz_refs = refs[3 : 3 + npairs]
        wt_refs = refs[3 + npairs : 3 + 2 * npairs]
        dx_ref, h_ref, dg_ref = refs[3 + 2 * npairs :]

        @pl.when(pl.program_id(0) == 0)
        def _():
            dg_ref[...] = jnp.zeros_like(dg_ref)

        dh = None
        for dz_ref, wt_ref in zip(dz_refs, wt_refs):
            part = jnp.dot(dz_ref[...], wt_ref[...], preferred_element_type=F32)
            dh = part if dh is None else dh + part
        xv = x_ref[...]
        gv = g_ref[...]
        r = _rstd(xv)
        xh = xv * r
        h_ref[...] = (xh * gv).astype(BF16)
        dg_ref[...] += _sum8(dh * xh)
        dxh = dh * gv
        dx_ref[...] = dxo_ref[...] + r * (dxh - xh * jnp.mean(dxh * xh, axis=-1, keepdims=True))

    row = pl.BlockSpec((tt, D), lambda i: (i, 0))
    return _token_call(
        body,
        name,
        T // tt,
        [row, row, _resident((1, D))] + [pl.BlockSpec((tt, N), lambda i: (i, 0))] * npairs + [_resident((N, D))] * npairs,
        [dxo, x, g, *[p[0] for p in pairs], *[p[1] for p in pairs]],
        [row, row, pl.BlockSpec((8, D), lambda i: (0, 0))],
        [jax.ShapeDtypeStruct((T, D), F32), jax.ShapeDtypeStruct((T, D), BF16), jax.ShapeDtypeStruct((8, D), F32)],
        [],
        comm,
    )


def wgrad_tn(a, b, tk, name):
    T, M = a.shape
    N = b.shape[1]
    tm = M if M * N * 4 <= WGRAD_ACC_BYTES else M // 2
    nt = T // tk

    def body(a_ref, b_ref, o_ref, acc_ref):
        k = pl.program_id(1)

        @pl.when(k == 0)
        def _():
            acc_ref[...] = jnp.zeros_like(acc_ref)

        acc_ref[...] += lax.dot_general(a_ref[...], b_ref[...], TN, preferred_element_type=F32)

        @pl.when(k == nt - 1)
        def _():
            o_ref[...] = acc_ref[...].astype(BF16)

    return _pcall(
        body,
        name=name,
        grid=(M // tm, nt),
        in_specs=[pl.BlockSpec((tk, tm), lambda i, k: (k, i)), pl.BlockSpec((tk, N), lambda i, k: (k, 0))],
        out_specs=pl.BlockSpec((tm, N), lambda i, k: (i, 0)),
        out_shape=jax.ShapeDtypeStruct((M, N), BF16),
        scratch_shapes=[pltpu.VMEM((tm, N), F32)],
        compiler_params=_cparams(2),
    )(a, b)


def _sgu_recompute(z_ref, lng_ref, lnb_ref, S):
    z = z_ref[...].astype(F32)
    zu, zv = z[:, :S], z[:, S:]
    tu, tv = _gelu_tanh(zu), _gelu_tanh(zv)
    u = 0.5 * zu * (1.0 + tu)
    v = 0.5 * zv * (1.0 + tv)
    vc = v - jnp.mean(v, axis=-1, keepdims=True)
    rstd = lax.rsqrt(jnp.mean(vc * vc, axis=-1, keepdims=True) + EPS)
    vhat = vc * rstd
    vln = (vhat * lng_ref[...] + lnb_ref[...]).astype(BF16)
    return zu, zv, tu, tv, u, vhat, rstd, vln


def sgu_mix_fwd(zpre, lng, lnb, ws, bias_full, tt, name, comm=None):
    T, S2 = zpre.shape
    S = S2 // 2
    dg = S // GROUPS

    def body(z_ref, lng_ref, lnb_ref, ws_ref, bias_ref, o_ref):
        _, _, _, _, u, _, _, vln = _sgu_recompute(z_ref, lng_ref, lnb_ref, S)
        for c in range(tt // CHUNK):
            rows = slice(c * CHUNK, (c + 1) * CHUNK)
            for gi in range(GROUPS):
                cols = slice(gi * dg, (gi + 1) * dg)
                mixed = jnp.dot(ws_ref[gi], vln[rows, cols], preferred_element_type=F32) + bias_ref[:, cols]
                o_ref[rows, cols] = (u[rows, cols] * mixed).astype(BF16)

    vec = pl.BlockSpec((1, S), lambda i: (0, 0))
    return _token_call(
        body,
        name,
        T // tt,
        [
            pl.BlockSpec((tt, S2), lambda i: (i, 0)),
            vec,
            vec,
            pl.BlockSpec((GROUPS, CHUNK, CHUNK), lambda i: (0, 0, 0)),
            pl.BlockSpec((CHUNK, S), lambda i: (0, 0)),
        ],
        [zpre, lng, lnb, ws, bias_full],
        [pl.BlockSpec((tt, S), lambda i: (i, 0))],
        [jax.ShapeDtypeStruct((T, S), BF16)],
        [],
        comm,
    )


def sgu_mix_bwd(dgated, zpre, lng, lnb, ws, ws_t, bias_full, tt, name, comm=None):
    T, S2 = zpre.shape
    S = S2 // 2
    dg = S // GROUPS

    def body(dgt_ref, z_ref, lng_ref, lnb_ref, ws_ref, wst_ref, bias_ref, dz_ref, dws_ref, dbias_ref, dlng_ref, dlnb_ref, du_ref, dvln_ref):
        @pl.when(pl.program_id(0) == 0)
        def _():
            dws_ref[...] = jnp.zeros_like(dws_ref)
            dbias_ref[...] = jnp.zeros_like(dbias_ref)
            dlng_ref[...] = jnp.zeros_like(dlng_ref)
            dlnb_ref[...] = jnp.zeros_like(dlnb_ref)

        zu, zv, tu, tv, u, vhat, rstd, vln = _sgu_recompute(z_ref, lng_ref, lnb_ref, S)
        for c in range(tt // CHUNK):
            rows = slice(c * CHUNK, (c + 1) * CHUNK)
            for gi in range(GROUPS):
                cols = slice(gi * dg, (gi + 1) * dg)
                v_cg = vln[rows, cols]
                mixed = jnp.dot(ws_ref[gi], v_cg, preferred_element_type=F32) + bias_ref[:, cols]
                dgt = dgt_ref[rows, cols].astype(F32)
                du_ref[rows, cols] = dgt * mixed
                dmx = dgt * u[rows, cols]
                dbias_ref[:, cols] += dmx
                dmx16 = dmx.astype(BF16)
                dws_ref[gi] += lax.dot_general(dmx16, v_cg, NT, preferred_element_type=F32)
                dvln_ref[rows, cols] = jnp.dot(wst_ref[gi], dmx16, preferred_element_type=F32)

        dvln = dvln_ref[...]
        dlng_ref[...] += _sum8(dvln * vhat)
        dlnb_ref[...] += _sum8(dvln)
        dvhat = dvln * lng_ref[...]
        dv = rstd * (dvhat - jnp.mean(dvhat, axis=-1, keepdims=True) - vhat * jnp.mean(dvhat * vhat, axis=-1, keepdims=True))
        dz_ref[:, :S] = (du_ref[...] * _gelu_grad(zu, tu)).astype(BF16)
        dz_ref[:, S:] = (dv * _gelu_grad(zv, tv)).astype(BF16)

    vec = pl.BlockSpec((1, S), lambda i: (0, 0))
    wsp = pl.BlockSpec((GROUPS, CHUNK, CHUNK), lambda i: (0, 0, 0))
    full = pl.BlockSpec((CHUNK, S), lambda i: (0, 0))
    acc8 = pl.BlockSpec((8, S), lambda i: (0, 0))
    return _token_call(
        body,
        name,
        T // tt,
        [pl.BlockSpec((tt, S), lambda i: (i, 0)), pl.BlockSpec((tt, S2), lambda i: (i, 0)), vec, vec, wsp, wsp, full],
        [dgated, zpre, lng, lnb, ws, ws_t, bias_full],
        [pl.BlockSpec((tt, S2), lambda i: (i, 0)), wsp, full, acc8, acc8],
        [
            jax.ShapeDtypeStruct((T, S2), BF16),
            jax.ShapeDtypeStruct((GROUPS, CHUNK, CHUNK), F32),
            jax.ShapeDtypeStruct((CHUNK, S), F32),
            jax.ShapeDtypeStruct((8, S), F32),
            jax.ShapeDtypeStruct((8, S), F32),
        ],
        [pltpu.VMEM((tt, S), F32), pltpu.VMEM((tt, S), F32)],
        comm,
    )


def _shifted_planes(sh, tt):
    n = tt + HALO - SUBLANES
    for s in range(1, SUBLANES):
        sh[s, 0:n, :] = sh[0, pl.ds(s, n), :]


def conv_mid_fwd(p, wdw, bdw, lng, lnb, tt, name, comm=None):
    T, C2 = p.shape
    C = C2 // 2

    def body(p_ref, w_ref, b_ref, lng_ref, lnb_ref, yc_ref, ys_ref, sh):
        @pl.when(pl.program_id(0) == 0)
        def _():
            sh[0, 0:HALO, :] = jnp.zeros((HALO, C), F32)

        @pl.when(pl.program_id(0) > 0)
        def _():
            sh[0, 0:HALO, :] = sh[0, tt : tt + HALO, :]

        pv = p_ref[...].astype(F32)
        sh[0, HALO:, :] = pv[:, :C] * _sigmoid(pv[:, C:])
        _shifted_planes(sh, tt)
        for lb in range(C // LANES):
            lanes = slice(lb * LANES, (lb + 1) * LANES)
            wk = [jnp.broadcast_to(w_ref[k : k + 1, lanes], (SUBLANES, LANES)) for k in range(CONV_W)]
            bias = jnp.broadcast_to(b_ref[:, lanes], (SUBLANES, LANES))

            def rows(i, carry):
                r0 = pl.multiple_of(i * CONV_ROWS, CONV_ROWS)
                accs = [[bias, jnp.zeros((SUBLANES, LANES), F32)] for _ in range(CONV_ROWS // SUBLANES)]
                for k in range(CONV_W):
                    o = HALO - (CONV_W - 1) + k
                    for j, acc in enumerate(accs):
                        blk = sh[o % SUBLANES, pl.ds(r0 + (o // SUBLANES + j) * SUBLANES, SUBLANES), lanes]
                        acc[k % 2] = acc[k % 2] + wk[k] * blk
                for j, acc in enumerate(accs):
                    yc_ref[pl.ds(r0 + j * SUBLANES, SUBLANES), lanes] = acc[0] + acc[1]
                return carry

            lax.fori_loop(0, tt // CONV_ROWS, rows, 0)
        acc = yc_ref[...]
        yc = acc - jnp.mean(acc, axis=-1, keepdims=True)
        yn = yc * lax.rsqrt(jnp.mean(yc * yc, axis=-1, keepdims=True) + EPS) * lng_ref[...] + lnb_ref[...]
        ys_ref[...] = (yn * _sigmoid(yn)).astype(BF16)

    vec = pl.BlockSpec((1, C), lambda i: (0, 0))
    row = pl.BlockSpec((tt, C), lambda i: (i, 0))
    return _token_call(
        body,
        name,
        T // tt,
        [pl.BlockSpec((tt, C2), lambda i: (i, 0)), pl.BlockSpec((HALO, C), lambda i: (0, 0)), vec, vec, vec],
        [p, wdw, bdw, lng, lnb],
        [row, row],
        [jax.ShapeDtypeStruct((T, C), F32), jax.ShapeDtypeStruct((T, C), BF16)],
        [pltpu.VMEM((SUBLANES, tt + HALO, C), F32)],
        comm,
    )


def conv_mid_bwd(dys, yc, p, wdw, lng, lnb, tt, name, comm=None):
    T, C2 = p.shape
    C = C2 // 2
    n = T // tt

    def body(dys_ref, yc_ref, p_ref, w_ref, lng_ref, lnb_ref, dp_ref, dw_ref, db_ref, dlng_ref, dlnb_ref, sh, y_s, dy_s):
        @pl.when(pl.program_id(0) == 0)
        def _():
            sh[0, tt : tt + HALO, :] = jnp.zeros((HALO, C), F32)
            dw_ref[...] = jnp.zeros_like(dw_ref)
            db_ref[...] = jnp.zeros_like(db_ref)
            dlng_ref[...] = jnp.zeros_like(dlng_ref)
            dlnb_ref[...] = jnp.zeros_like(dlnb_ref)

        @pl.when(pl.program_id(0) > 0)
        def _():
            sh[0, tt : tt + HALO, :] = sh[0, 0:HALO, :]

        ycv = yc_ref[...]
        ycc = ycv - jnp.mean(ycv, axis=-1, keepdims=True)
        rstd = lax.rsqrt(jnp.mean(ycc * ycc, axis=-1, keepdims=True) + EPS)
        yhat = ycc * rstd
        lng_v = lng_ref[...]
        yn = yhat * lng_v + lnb_ref[...]
        sg = _sigmoid(yn)
        dyn = dys_ref[...].astype(F32) * (sg * (1.0 + yn * (1.0 - sg)))
        dlng_ref[...] += _sum8(dyn * yhat)
        dlnb_ref[...] += _sum8(dyn)
        dyh = dyn * lng_v
        dyc = rstd * (dyh - jnp.mean(dyh, axis=-1, keepdims=True) - yhat * jnp.mean(dyh * yhat, axis=-1, keepdims=True))
        db_ref[...] += _sum8(dyc)
        sh[0, 0:tt, :] = dyc
        _shifted_planes(sh, tt)

        pv = p_ref[...].astype(F32)
        a = pv[:, :C]
        sgate = _sigmoid(pv[:, C:])
        y_s[...] = a * sgate
        for lb in range(C // LANES):
            lanes = slice(lb * LANES, (lb + 1) * LANES)
            for k0, k1 in ((0, CONV_W // 2), (CONV_W // 2, CONV_W)):
                wk = [jnp.broadcast_to(w_ref[k : k + 1, lanes], (SUBLANES, LANES)) for k in range(k0, k1)]

                def rows(i, dw_acc, k0=k0, k1=k1, wk=wk, lanes=lanes):
                    r0 = pl.multiple_of(i * CONV_ROWS, CONV_ROWS)
                    dw_acc = list(dw_acc)
                    zero = jnp.zeros((SUBLANES, LANES), F32)
                    ybs, accs = [], []
                    for j in range(CONV_ROWS // SUBLANES):
                        at = pl.ds(r0 + j * SUBLANES, SUBLANES)
                        ybs.append(y_s[at, lanes])
                        accs.append([zero if k0 == 0 else dy_s[at, lanes], zero])
                    for n_k, k in enumerate(range(k0, k1)):
                        o = CONV_W - 1 - k
                        for j, acc in enumerate(accs):
                            blk = sh[o % SUBLANES, pl.ds(r0 + (o // SUBLANES + j) * SUBLANES, SUBLANES), lanes]
                            acc[n_k % 2] = acc[n_k % 2] + wk[n_k] * blk
                            dw_acc[n_k] = dw_acc[n_k] + ybs[j] * blk
                    for j, acc in enumerate(accs):
                        dy_s[pl.ds(r0 + j * SUBLANES, SUBLANES), lanes] = acc[0] + acc[1]
                    return tuple(dw_acc)

                dw_acc = lax.fori_loop(0, tt // CONV_ROWS, rows, tuple(jnp.zeros((SUBLANES, LANES), F32) for _ in range(k0, k1)))
                for n_k, k in enumerate(range(k0, k1)):
                    dw_ref[SUBLANES * k : SUBLANES * (k + 1), lanes] += dw_acc[n_k]
        dy = dy_s[...]
        dp_ref[:, :C] = (dy * sgate).astype(BF16)
        dp_ref[:, C:] = (dy * a * sgate * (1.0 - sgate)).astype(BF16)

    vec = pl.BlockSpec((1, C), lambda i: (0, 0))
    row = pl.BlockSpec((tt, C), lambda i: (n - 1 - i, 0))
    row2 = pl.BlockSpec((tt, C2), lambda i: (n - 1 - i, 0))
    acc8 = pl.BlockSpec((8, C), lambda i: (0, 0))
    return _token_call(
        body,
        name,
        n,
        [row, row, row2, pl.BlockSpec((HALO, C), lambda i: (0, 0)), vec, vec],
        [dys, yc, p, wdw, lng, lnb],
        [row2, pl.BlockSpec((HALO * 8, C), lambda i: (0, 0)), acc8, acc8, acc8],
        [
            jax.ShapeDtypeStruct((T, C2), BF16),
            jax.ShapeDtypeStruct((HALO * 8, C), F32),
            jax.ShapeDtypeStruct((8, C), F32),
            jax.ShapeDtypeStruct((8, C), F32),
            jax.ShapeDtypeStruct((8, C), F32),
        ],
        [pltpu.VMEM((SUBLANES, tt + HALO, C), F32), pltpu.VMEM((tt, C), F32), pltpu.VMEM((tt, C), F32)],
        comm,
    )


def loss_head(y, target, tt, name):
    T, D = y.shape

    def body(y_ref, t_ref, dy_ref, sq_ref):
        @pl.when(pl.program_id(0) == 0)
        def _():
            sq_ref[...] = jnp.zeros_like(sq_ref)

        err = y_ref[...] - t_ref[...]
        dy_ref[...] = err * (1.0 / D)
        sq_ref[...] += _sum8(err * err)

    row = pl.BlockSpec((tt, D), lambda i: (i, 0))
    return _pcall(
        body,
        name=name,
        grid=(T // tt,),
        in_specs=[row, row],
        out_specs=[row, pl.BlockSpec((8, D), lambda i: (0, 0))],
        out_shape=[jax.ShapeDtypeStruct((T, D), F32), jax.ShapeDtypeStruct((8, D), F32)],
        compiler_params=_cparams(1),
    )(y, target)


def sum_slots(slots, name):
    _, r, w = slots.shape

    def body(s_ref, o_ref):
        total = s_ref[0].astype(F32)
        for p in range(1, N_DEV):
            total = total + s_ref[p].astype(F32)
        o_ref[...] = total

    return _pcall(body, name=name, out_shape=jax.ShapeDtypeStruct((r, w), F32), compiler_params=pltpu.CompilerParams(vmem_limit_bytes=VMEM_LIMIT_V7X))(slots)


def adamw(w, g, m, v, name):
    R, C = w.shape
    tr = _tile(R, 512, 8)
    c1 = 1.0 - ADAM_B1**ADAM_STEP
    c2 = 1.0 - ADAM_B2**ADAM_STEP

    def body(w_ref, g_ref, m_ref, v_ref, d_ref, mo_ref, vo_ref):
        gv = g_ref[...]
        m2 = ADAM_B1 * m_ref[...] + (1.0 - ADAM_B1) * gv
        v2 = ADAM_B2 * v_ref[...] + (1.0 - ADAM_B2) * (gv * gv)
        mo_ref[...] = m2
        vo_ref[...] = v2
        d_ref[...] = -ADAM_LR * ((m2 / c1) / (jnp.sqrt(v2 / c2) + ADAM_EPS) + ADAM_WD * w_ref[...])

    blk = pl.BlockSpec((tr, C), lambda i: (i, 0))
    return _pcall(
        body,
        name=name,
        grid=(R // tr,),
        in_specs=[blk] * 4,
        out_specs=[blk] * 3,
        out_shape=[jax.ShapeDtypeStruct((R, C), F32)] * 3,
        compiler_params=_cparams(1),
    )(w, g, m, v)


def _sublayers(depth):
    out = []
    for layer in range(depth):
        out.append(("ffn", layer, 0))
        out.append(("sgu" if layer % N_MIXERS == 0 else "conv", layer, layer // N_MIXERS))
        out.append(("ffn", layer, 1))
    return out


def _row_blocks(sub, shards):
    kind, layer, idx = sub
    if kind == "ffn":
        blk = {"gate_t": shards["ff_w_gate"][layer, idx].T, "up_t": shards["ff_w_up"][layer, idx].T, "down": shards["ff_w_down"][layer, idx]}
    elif kind == "sgu":
        blk = {"in_t": shards["sgu_w_in"][idx].T, "out": shards["sgu_w_out"][idx]}
    else:
        blk = {"pw1_t": shards["conv_w_pw1"][idx].T, "pw2": shards["conv_w_pw2"][idx]}
    return {nm: arr.astype(BF16) for nm, arr in blk.items()}


def _pad8(a):
    return jnp.pad(a, ((0, (-a.shape[0]) % 8), (0, 0)))


def kernel(x, norm_g, ff_w_gate, ff_w_up, ff_w_down, sgu_w_in, sgu_ln_g, sgu_ln_b, sgu_w_spatial, sgu_b_spatial, sgu_w_out, conv_w_pw1, conv_w_dw, conv_b_dw, conv_ln_g, conv_ln_b, conv_w_pw2, loss_target, m_norm_g, m_ff_w_gate, m_ff_w_up, m_ff_w_down, m_sgu_w_in, m_sgu_ln_g, m_sgu_ln_b, m_sgu_w_spatial, m_sgu_b_spatial, m_sgu_w_out, m_conv_w_pw1, m_conv_w_dw, m_conv_b_dw, m_conv_ln_g, m_conv_ln_b, m_conv_w_pw2, v_norm_g, v_ff_w_gate, v_ff_w_up, v_ff_w_down, v_sgu_w_in, v_sgu_ln_g, v_sgu_ln_b, v_sgu_w_spatial, v_sgu_b_spatial, v_sgu_w_out, v_conv_w_pw1, v_conv_w_dw, v_conv_b_dw, v_conv_ln_g, v_conv_ln_b, v_conv_w_pw2):
    names = ["norm_g", "ff_w_gate", "ff_w_up", "ff_w_down", "sgu_w_in", "sgu_ln_g", "sgu_ln_b", "sgu_w_spatial", "sgu_b_spatial", "sgu_w_out", "conv_w_pw1", "conv_w_dw", "conv_b_dw", "conv_ln_g", "conv_ln_b", "conv_w_pw2"]
    weights = dict(zip(names, [norm_g, ff_w_gate, ff_w_up, ff_w_down, sgu_w_in, sgu_ln_g, sgu_ln_b, sgu_w_spatial, sgu_b_spatial, sgu_w_out, conv_w_pw1, conv_w_dw, conv_b_dw, conv_ln_g, conv_ln_b, conv_w_pw2]))
    moments_m = dict(zip(names, [m_norm_g, m_ff_w_gate, m_ff_w_up, m_ff_w_down, m_sgu_w_in, m_sgu_ln_g, m_sgu_ln_b, m_sgu_w_spatial, m_sgu_b_spatial, m_sgu_w_out, m_conv_w_pw1, m_conv_w_dw, m_conv_b_dw, m_conv_ln_g, m_conv_ln_b, m_conv_w_pw2]))
    moments_v = dict(zip(names, [v_norm_g, v_ff_w_gate, v_ff_w_up, v_ff_w_down, v_sgu_w_in, v_sgu_ln_g, v_sgu_ln_b, v_sgu_w_spatial, v_sgu_b_spatial, v_sgu_w_out, v_conv_w_pw1, v_conv_w_dw, v_conv_b_dw, v_conv_ln_g, v_conv_ln_b, v_conv_w_pw2]))

    _, T, D = x.shape
    depth = norm_g.shape[0]
    n_conv = conv_w_dw.shape[0]
    n_sgu = sgu_w_in.shape[0]
    S = sgu_ln_g.shape[1]
    lanes = norm_g.shape[2]
    subs = _sublayers(depth)
    n_sub = len(subs)

    cx, cy, cc = (lax.axis_index(a) for a in MESH_AXES)
    my_block = 4 * cx + 2 * cy + cc

    blocks = [_row_blocks(sub, weights) for sub in subs]
    dw_pad = jnp.pad(conv_w_dw, ((0, 0), (0, HALO - CONV_W), (0, 0)))
    small_parts = [_pad8(p) for p in (norm_g.reshape(-1, lanes), dw_pad.reshape(-1, lanes), conv_b_dw, conv_ln_g, conv_ln_b)]
    small_rows = [p.shape[0] for p in small_parts]
    small = jnp.concatenate(small_parts, axis=0)

    def gathered(block_names, outs):
        return {nm: o.reshape(N_DEV * o.shape[1], o.shape[2]) for nm, o in zip(block_names, outs)}

    first_names = [nm for nm in blocks[0] if nm != "down"]
    first = exchange([("gather", blocks[0][nm]) for nm in first_names] + [("gather", small)], name="gather_first")
    W = [None] * n_sub
    W[0] = gathered(first_names, first[:-1])
    gsmall = first[-1]

    small_full = jnp.transpose(gsmall, (1, 0, 2)).reshape(gsmall.shape[1], N_DEV * lanes)
    so = [0]
    for r in small_rows:
        so.append(so[-1] + r)
    norm_full = small_full[so[0] : so[0] + depth * norm_g.shape[1]].reshape(depth, -1, D)
    dw_full = small_full[so[1] : so[1] + n_conv * HALO].reshape(n_conv, HALO, D)
    bdw_full, clng_full, clnb_full = (small_full[so[k] : so[k] + n_conv] for k in (2, 3, 4))

    causal = jnp.tril(jnp.ones((CHUNK, CHUNK), dtype=bool))
    ws_all = jnp.where(causal[None, None], sgu_w_spatial, 0.0).astype(BF16)
    wst_all = jnp.swapaxes(ws_all, -1, -2)
    bias_full_all = jnp.repeat(jnp.swapaxes(sgu_b_spatial, -1, -2), S // GROUPS, axis=-1)

    tt = _tile(T, 512, CHUNK)
    tt_mix = _tile(T, 256, CHUNK)

    tk = _tile(T, 1024, CHUNK)

    def vec(v):
        return v.reshape(1, -1)

    def split_first(comm):
        return (comm[:1], comm[1:]) if comm else (None, None)

    def norms(kind, layer, idx):
        pre = 4 * idx if kind == "ffn" else 2
        return vec(norm_full[layer, pre]), vec(norm_full[layer, pre + 1])

    xs = x[0]
    saved = []
    for si, (kind, layer, idx) in enumerate(subs):
        w = W[si]
        g_pre, g_post = norms(kind, layer, idx)
        nxt = [("gather", a) for a in blocks[si + 1].values()] if si + 1 < n_sub else None
        nxt_first, nxt_rest = split_first(nxt)
        if kind == "ffn":
            own = [("gather", blocks[0]["down"])] if si == 0 else []
            (gate, up), got = rms_matmul_nt(xs, g_pre, [w["gate_t"], w["up_t"]], tt, "ffn_in", own + (nxt or []))
            if own:
                w.update(gathered(["down"], got[:1]))
                got = got[1:]
            x_new, o, a = matmul_rms_res(xs, (gate, up), w["down"], g_post, FFN_SCALE, tt, "ffn_out")
            saved.append((xs, gate, up, o, a))
        elif kind == "sgu":
            (zpre,), got = rms_matmul_nt(xs, g_pre, [w["in_t"]], tt, "sgu_in", nxt_first)
            (gated,), got_rest = sgu_mix_fwd(zpre, vec(sgu_ln_g[idx]), vec(sgu_ln_b[idx]), ws_all[idx], bias_full_all[idx], tt_mix, "sgu_mix", nxt_rest)
            got = list(got) + list(got_rest)
            x_new, mm = matmul_rms_res(xs, (gated,), w["out"], g_post, 1.0, tt, "sgu_out")
            saved.append((xs, zpre, gated, mm))
        else:
            (p,), got = rms_matmul_nt(xs, g_pre, [w["pw1_t"]], tt, "conv_pw1", nxt_first)
            (yc, ys), got_rest = conv_mid_fwd(p, dw_full[idx], vec(bdw_full[idx]), vec(clng_full[idx]), vec(clnb_full[idx]), tt, "conv_mid", nxt_rest)
            got = list(got) + list(got_rest)
            x_new, mm = matmul_rms_res(xs, (ys,), w["pw2"], g_post, 1.0, tt, "conv_pw2")
            saved.append((xs, p, yc, ys, mm))
        if nxt:
            W[si + 1] = gathered(blocks[si + 1].keys(), got)
        xs = x_new

    dx, sq = loss_head(xs, loss_target[0], tt, name="loss_head")
    loss = lax.psum(0.5 * jnp.sum(sq) / D, MESH_AXES)

    d_norm = [[None] * norm_full.shape[1] for _ in range(depth)]
    d_sgu = [None] * n_sgu
    d_conv = [None] * n_conv
    slots = [None] * n_sub
    pending = None

    def scatter_of(p):
        return [("scatter", dwm.reshape(N_DEV, dwm.shape[0] // N_DEV, D)) for dwm in p[1].values()] if p else None

    def pack_small():
        def sum8(v):
            return jnp.sum(v, axis=0)

        g_norm = jnp.stack([jnp.stack([sum8(d) for d in row]) for row in d_norm])
        g_dw = jnp.stack([jnp.sum(d[0].reshape(HALO, 8, D), axis=1) for d in d_conv])
        g_bdw, g_clng, g_clnb = (jnp.stack([sum8(d[k]) for d in d_conv]) for k in (1, 2, 3))
        g_slng, g_slnb = (jnp.stack([sum8(d[k]) for d in d_sgu]) for k in (2, 3))
        g_bsp = jnp.stack([jnp.sum(d[1].reshape(CHUNK, GROUPS, S // GROUPS), axis=-1).T for d in d_sgu])
        g_wsp = jnp.stack([jnp.where(causal[None], d[0], 0.0) for d in d_sgu])
        parts = [g_norm, g_dw, g_bdw, g_clng, g_clnb, g_slng, g_slnb, g_bsp, g_wsp]
        return parts, [p.size // D for p in parts], jnp.concatenate([_pad8(p.reshape(-1, D)) for p in parts], axis=0)

    for si in reversed(range(n_sub)):
        kind, layer, idx = subs[si]
        w = W[si]
        g_pre, g_post = norms(kind, layer, idx)
        pre = 4 * idx if kind == "ffn" else 2
        sc_first, sc_rest = split_first(scatter_of(pending))
        if kind == "ffn":
            xs_in, gate, up, o, a = saved[si]
            (do, dg_post, dgate, dup), got = rms_bwd_matmul_nt(dx, o, g_post, w["down"], FFN_SCALE, (gate, up), tt, "ffn_out_bwd", scatter_of(pending))
            dw_down = wgrad_tn(a, do, tk, name="wgrad_ffn_out")
            tail = None
            if si == 0:
                d_norm[layer][pre], d_norm[layer][pre + 1] = jnp.zeros_like(dg_post), dg_post
                sparts, srows, sgrad = pack_small()
                tail = scatter_of((si, {"down": dw_down})) + [("gather", sgrad)]
            (dx, h, dg_pre), got_tail = matmul_rms_bwd(dx, xs_in, g_pre, [(dgate, w["gate_t"]), (dup, w["up_t"])], tt, "ffn_in_bwd", tail)
            dws_now = {"gate_t": wgrad_tn(dgate, h, tk, name="wgrad_ffn_in"), "up_t": wgrad_tn(dup, h, tk, name="wgrad_ffn_in"), "down": dw_down}
        elif kind == "sgu":
            xs_in, zpre, gated, mm = saved[si]
            (dm, dg_post, dgated), got = rms_bwd_matmul_nt(dx, mm, g_post, w["out"], 1.0, None, tt, "sgu_out_bwd", sc_first)
            (dzpre, dws, dbias, dlng, dlnb), got_rest = sgu_mix_bwd(dgated, zpre, vec(sgu_ln_g[idx]), vec(sgu_ln_b[idx]), ws_all[idx], wst_all[idx], bias_full_all[idx], tt_mix, "sgu_mix_bwd", sc_rest)
            got = list(got) + list(got_rest)
            (dx, h, dg_pre), _ = matmul_rms_bwd(dx, xs_in, g_pre, [(dzpre, w["in_t"])], tt, "sgu_in_bwd")
            dws_now = {"in_t": wgrad_tn(dzpre, h, tt, name="wgrad_sgu_in"), "out": wgrad_tn(gated, dm, tk, name="wgrad_sgu_out")}
            d_sgu[idx] = (dws, dbias, dlng, dlnb)
        else:
            xs_in, p, yc, ys, mm = saved[si]
            (dm, dg_post, dys), got = rms_bwd_matmul_nt(dx, mm, g_post, w["pw2"], 1.0, None, tt, "conv_pw2_bwd", sc_first)
            (dp, dwdw, dbdw, dlng, dlnb), got_rest = conv_mid_bwd(dys, yc, p, dw_full[idx], vec(clng_full[idx]), vec(clnb_full[idx]), tt, "conv_mid_bwd", sc_rest)
            got = list(got) + list(got_rest)
            (dx, h, dg_pre), _ = matmul_rms_bwd(dx, xs_in, g_pre, [(dp, w["pw1_t"])], tt, "conv_pw1_bwd")
            dws_now = {"pw1_t": wgrad_tn(dp, h, tk, name="wgrad_conv_pw1"), "pw2": wgrad_tn(ys, dm, tk, name="wgrad_conv_pw2")}
            d_conv[idx] = (dwdw, dbdw, dlng, dlnb)
        d_norm[layer][pre], d_norm[layer][pre + 1] = dg_pre, dg_post
        if pending:
            slots[pending[0]] = dict(zip(pending[1].keys(), got))
        pending = (si, dws_now)
    grad_x = dx[None]

    last_names = [nm for nm in pending[1] if nm != "down"]
    last = exchange(scatter_of((0, {nm: pending[1][nm] for nm in last_names})) + [("gather", dg_pre)], name="scatter_last")
    slots[0] = dict(zip(last_names, last[:-1]))
    slots[0]["down"] = got_tail[0]
    stotal = sum_slots(got_tail[1], name="sum_slots")
    g_first = jnp.sum(sum_slots(last[-1], name="sum_slots"), axis=0)

    gs = {(si, nm): sum_slots(s, name="sum_slots") for si in range(n_sub) for nm, s in slots[si].items()}
    grads = {}
    ffn_si = {(layer, idx): si for si, (kind, layer, idx) in enumerate(subs) if kind == "ffn"}
    sgu_si = {idx: si for si, (kind, layer, idx) in enumerate(subs) if kind == "sgu"}
    conv_si = {idx: si for si, (kind, layer, idx) in enumerate(subs) if kind == "conv"}
    grads["ff_w_gate"] = jnp.stack([jnp.stack([gs[(ffn_si[(l, f)], "gate_t")].T for f in range(2)]) for l in range(depth)])
    grads["ff_w_up"] = jnp.stack([jnp.stack([gs[(ffn_si[(l, f)], "up_t")].T for f in range(2)]) for l in range(depth)])
    grads["ff_w_down"] = jnp.stack([jnp.stack([gs[(ffn_si[(l, f)], "down")] for f in range(2)]) for l in range(depth)])
    grads["sgu_w_in"] = jnp.stack([gs[(sgu_si[j], "in_t")].T for j in range(n_sgu)])
    grads["sgu_w_out"] = jnp.stack([gs[(sgu_si[j], "out")] for j in range(n_sgu)])
    grads["conv_w_pw1"] = jnp.stack([gs[(conv_si[j], "pw1_t")].T for j in range(n_conv)])
    grads["conv_w_pw2"] = jnp.stack([gs[(conv_si[j], "pw2")] for j in range(n_conv)])

    so = [0]
    for r in srows:
        so.append(so[-1] + r + (-r) % 8)
    sp = [stotal[so[k] : so[k] + srows[k]].reshape(sparts[k].shape) for k in range(len(sparts))]

    def my_lanes(v):
        return lax.dynamic_slice_in_dim(v, my_block * lanes, lanes, axis=-1)

    grads["norm_g"] = my_lanes(sp[0].at[0, 0].set(g_first))
    grads["conv_w_dw"] = my_lanes(sp[1])[:, :CONV_W]
    grads["conv_b_dw"] = my_lanes(sp[2])
    grads["conv_ln_g"] = my_lanes(sp[3])
    grads["conv_ln_b"] = my_lanes(sp[4])
    grads["sgu_ln_g"], grads["sgu_ln_b"], grads["sgu_b_spatial"], grads["sgu_w_spatial"] = sp[5], sp[6], sp[7], sp[8]

    deltas, new_m, new_v = {}, {}, {}
    for nm in names:
        w = weights[nm]
        two_d = (-1, w.shape[-1])
        d, m2, v2 = adamw(w.reshape(two_d), grads[nm].reshape(two_d), moments_m[nm].reshape(two_d), moments_v[nm].reshape(two_d), name="adamw")
        deltas[nm], new_m[nm], new_v[nm] = d.reshape(w.shape), m2.reshape(w.shape), v2.reshape(w.shape)

    return (loss, grad_x, *[grads[n] for n in names], *[deltas[n] for n in names], *[new_m[n] for n in names], *[new_v[n] for n in names])
```

```python
import jax
import jax.numpy as jnp
from jax import lax
from jax.experimental import pallas as pl
from jax.experimental.pallas import tpu as pltpu

F32 = jnp.float32
BF16 = jnp.bfloat16

EPS = 1e-6
FFN_SCALE = 0.5
N_MIXERS = 2
CHUNK = 128
GROUPS = 8
CONV_W = 31
HALO = 32
GELU_C0 = 0.7978845608028654
GELU_C1 = 0.044715
ADAM_LR, ADAM_B1, ADAM_B2, ADAM_EPS, ADAM_WD, ADAM_STEP = 0.001, 0.9, 0.999, 1e-08, 0.01, 10

MESH_AXES = ("x", "y", "c")
N_DEV = 8
VMEM_LIMIT_V7X = 56 * 1024 * 1024
WGRAD_ACC_BYTES = 16 * 1024 * 1024
MXU_COLS_V7X = 256
SUBLANES, LANES = 8, 128
CONV_ROWS = 32

NT = (((1,), (1,)), ((), ()))
TN = (((0,), (0,)), ((), ()))

HBM_SPEC = pl.BlockSpec(memory_space=pl.ANY)
MESH_ID = pl.DeviceIdType.MESH


def _pcall(body, **kw):
    return pl.pallas_call(body, **kw)


def _cparams(n_axes):
    return pltpu.CompilerParams(dimension_semantics=("arbitrary",) * n_axes, vmem_limit_bytes=VMEM_LIMIT_V7X)


def _tile(n, pref, align):
    if n <= pref:
        return n
    t = (pref // align) * align
    while t > align and n % t:
        t -= align
    assert n % t == 0, (n, pref, align)
    return t


def _rstd(x):
    return lax.rsqrt(jnp.mean(x * x, axis=-1, keepdims=True) + EPS)


def _sum8(v):
    t, d = v.shape
    return v.reshape(t // 8, 8, d).sum(axis=0)


def _sigmoid(x):
    return jax.nn.sigmoid(x)


def _gelu_tanh(z):
    return jnp.tanh(GELU_C0 * (z + GELU_C1 * z * z * z))


def _gelu_grad(z, t):
    return 0.5 * (1.0 + t) + 0.5 * z * (1.0 - t * t) * (GELU_C0 * (1.0 + 3.0 * GELU_C1 * z * z))


def _resident(shape):
    return pl.BlockSpec(shape, lambda i: (0,) * len(shape), pipeline_mode=pl.Buffered(1))


def _exchange_io(comm):
    n = len(comm)
    out_shape = [jax.ShapeDtypeStruct(((N_DEV,) + a.shape) if kind == "gather" else a.shape, a.dtype) for kind, a in comm]
    scratch = [pltpu.SemaphoreType.DMA((n, N_DEV - 1)), pltpu.SemaphoreType.DMA((n, N_DEV - 1)), pltpu.SemaphoreType.DMA((n,))]
    return [HBM_SPEC] * n, [HBM_SPEC] * n, out_shape, scratch


def _exchange_copies(kinds, in_refs, out_refs, send_sems, recv_sems, local_sems, with_arrivals=True):
    x, y, c = (lax.axis_index(a) for a in MESH_AXES)
    me = 4 * x + 2 * y + c
    local, sends, arrivals = [], [], []
    for a, kind in enumerate(kinds):
        src, dst = in_refs[a], out_refs[a]
        gather = kind == "gather"
        local.append(pltpu.make_async_copy(src if gather else src.at[me], dst.at[me], local_sems.at[a]))
        for k in range(N_DEV - 1):
            mask = k + 1
            px = 1 - x if mask & 4 else x
            py = 1 - y if mask & 2 else y
            pc = 1 - c if mask & 1 else c
            peer = 4 * px + 2 * py + pc
            block = src if gather else src.at[peer]
            for into, lst in ((me, sends), (peer, arrivals)):
                if lst is arrivals and not with_arrivals:
                    continue
                lst.append(
                    pltpu.make_async_remote_copy(
                        src_ref=block, dst_ref=dst.at[into], send_sem=send_sems.at[a, k], recv_sem=recv_sems.at[a, k], device_id=(px, py, pc), device_id_type=MESH_ID
                    )
                )
    return local, sends, arrivals


def _exchange_start(copies):
    local, sends, _ = copies
    for cp in local + sends:
        cp.start()


def _exchange_finish(copies):
    local, sends, arrivals = copies
    for cp in arrivals:
        cp.wait_recv()
    for cp in sends:
        cp.wait_send()
    for cp in local:
        cp.wait()


def exchange(comm, name):
    kinds = [k for k, _ in comm]
    n = len(comm)
    in_specs, out_specs, out_shape, scratch = _exchange_io(comm)

    def body(*refs):
        copies = _exchange_copies(kinds, refs[:n], refs[n : 2 * n], *refs[2 * n :])
        _exchange_start(copies)
        _exchange_finish(copies)

    return _pcall(body, name=name, in_specs=in_specs, out_specs=out_specs, out_shape=out_shape, scratch_shapes=scratch)(*[a for _, a in comm])


def _token_call(body, name, n_tiles, in_specs, inputs, out_specs, out_shape, scratch, comm):
    n_in, n_out, n_scr = len(inputs), len(out_shape), len(scratch)
    comm = comm or []
    nc = len(comm)
    full_body = body
    if nc:
        kinds = [k for k, _ in comm]
        c_in, c_out, c_shape, c_scr = _exchange_io(comm)
        in_specs, out_specs, out_shape, scratch = in_specs + c_in, out_specs + c_out, out_shape + c_shape, scratch + c_scr

        def full_body(*refs):
            ins, cins = refs[:n_in], refs[n_in : n_in + nc]
            o0 = n_in + nc
            outs, couts = refs[o0 : o0 + n_out], refs[o0 + n_out : o0 + n_out + nc]
            s0 = o0 + n_out + nc
            scr, sems = refs[s0 : s0 + n_scr], refs[s0 + n_scr :]

            @pl.when(pl.program_id(0) == 0)
            def _():
                _exchange_start(_exchange_copies(kinds, cins, couts, *sems, with_arrivals=False))

            body(*ins, *outs, *scr)

            @pl.when(pl.program_id(0) == n_tiles - 1)
            def _():
                _exchange_finish(_exchange_copies(kinds, cins, couts, *sems))

    res = _pcall(
        full_body, name=name, grid=(n_tiles,), in_specs=in_specs, out_specs=out_specs, out_shape=out_shape, scratch_shapes=scratch, compiler_params=_cparams(1)
    )(*inputs, *[a for _, a in comm])
    return res[:n_out], res[n_out:]


def rms_matmul_nt(x, g, wts, tt, name, comm=None, swiglu=False):
    T, D = x.shape
    N = wts[0].shape[0]
    nw = len(wts)
    n_out = 3 if swiglu else nw
    nc = _tile(N, MXU_COLS_V7X if swiglu else 1408, 128)

    def body(*refs):
        x_ref, g_ref = refs[0], refs[1]
        w_refs, o_refs = refs[2 : 2 + nw], refs[2 + nw : 2 + nw + n_out]
        xv = x_ref[...]
        h = (xv * _rstd(xv) * g_ref[...]).astype(BF16)
        if swiglu:
            silu_ref, dact_ref, act_ref = o_refs
            for j in range(N // nc):
                cols = slice(j * nc, (j + 1) * nc)
                gate = lax.dot_general(h, w_refs[0][cols, :], NT, preferred_element_type=F32)
                up = lax.dot_general(h, w_refs[1][cols, :], NT, preferred_element_type=F32)
                s = _sigmoid(gate)
                t = gate * s
                silu_ref[:, cols] = t.astype(BF16)
                act_ref[:, cols] = (t * up).astype(BF16)
                dact_ref[:, cols] = (up * (s + t - t * s)).astype(BF16)
            return
        for w_ref, o_ref in zip(w_refs, o_refs):
            for j in range(N // nc):
                cols = slice(j * nc, (j + 1) * nc)
                o_ref[:, cols] = lax.dot_general(h, w_ref[cols, :], NT, preferred_element_type=F32).astype(BF16)

    return _token_call(
        body,
        name,
        T // tt,
        [pl.BlockSpec((tt, D), lambda i: (i, 0)), _resident((1, D))] + [_resident((N, D))] * nw,
        [x, g, *wts],
        [pl.BlockSpec((tt, N), lambda i: (i, 0))] * n_out,
        [jax.ShapeDtypeStruct((T, N), BF16)] * n_out,
        [],
        comm,
    )


def matmul_rms_res(x, a, w, g, scale, tt, name):
    T, D = x.shape
    K = w.shape[0]

    def body(x_ref, a_ref, w_ref, g_ref, xo_ref, m_ref):
        m = jnp.dot(a_ref[...], w_ref[...], preferred_element_type=F32)
        m_ref[...] = m
        xo_ref[...] = x_ref[...] + scale * (m * _rstd(m) * g_ref[...])

    row = pl.BlockSpec((tt, D), lambda i: (i, 0))
    outs, _ = _token_call(
        body,
        name,
        T // tt,
        [row, pl.BlockSpec((tt, K), lambda i: (i, 0)), _resident((K, D)), _resident((1, D))],
        [x, a, w, g],
        [row, row],
        [jax.ShapeDtypeStruct((T, D), F32)] * 2,
        [],
        None,
    )
    return outs


def rms_bwd_matmul_nt(dxo, m, g, w, scale, gate_up, tt, name, comm=None):
    T, D = dxo.shape
    K = w.shape[0]
    act = gate_up is not None
    ec = _tile(K, MXU_COLS_V7X if act else 1024, 128)

    def body(*refs):
        if act:
            dxo_ref, m_ref, g_ref, w_ref, silu_ref, dact_ref, dm_ref, dg_ref, dgate_ref, dup_ref = refs
        else:
            dxo_ref, m_ref, g_ref, w_ref, dm_ref, dg_ref, da_ref = refs

        @pl.when(pl.program_id(0) == 0)
        def _():
            dg_ref[...] = jnp.zeros_like(dg_ref)

        dy = scale * dxo_ref[...]
        mv = m_ref[...]
        r = _rstd(mv)
        mh = mv * r
        dg_ref[...] += _sum8(dy * mh)
        dmh = dy * g_ref[...]
        dm = (r * (dmh - mh * jnp.mean(dmh * mh, axis=-1, keepdims=True))).astype(BF16)
        dm_ref[...] = dm
        for j in range(K // ec):
            cols = slice(j * ec, (j + 1) * ec)
            da = lax.dot_general(dm, w_ref[cols, :], NT, preferred_element_type=F32)
            if act:
                dup_ref[:, cols] = (da * silu_ref[:, cols].astype(F32)).astype(BF16)
                dgate_ref[:, cols] = (da * dact_ref[:, cols].astype(F32)).astype(BF16)
            else:
                da_ref[:, cols] = da.astype(BF16)

    row = pl.BlockSpec((tt, D), lambda i: (i, 0))
    col = pl.BlockSpec((tt, K), lambda i: (i, 0))
    n_col = 2 if act else 1
    return _token_call(
        body,
        name,
        T // tt,
        [row, row, _resident((1, D)), _resident((K, D))] + ([col, col] if act else []),
        [dxo, m, g, w, *(gate_up or ())],
        [row, pl.BlockSpec((8, D), lambda i: (0, 0))] + [col] * n_col,
        [jax.ShapeDtypeStruct((T, D), BF16), jax.ShapeDtypeStruct((8, D), F32)] + [jax.ShapeDtypeStruct((T, K), BF16)] * n_col,
        [],
        comm,
    )


def matmul_rms_bwd(dxo, x, g, pairs, tt, name, comm=None):
    T, D = x.shape
    N = pairs[0][0].shape[1]
    npairs = len(pairs)

    def body(*refs):
        dxo_ref, x_ref, g_ref = refs[:3]
        dz_refs = refs[3 : 3 + npairs]
        wt_refs = refs[3 + npairs : 3 + 2 * npairs]
        dx_ref, h_ref, dg_ref = refs[3 + 2 * npairs :]

        @pl.when(pl.program_id(0) == 0)
        def _():
            dg_ref[...] = jnp.zeros_like(dg_ref)

        dh = None
        for dz_ref, wt_ref in zip(dz_refs, wt_refs):
            part = jnp.dot(dz_ref[...], wt_ref[...], preferred_element_type=F32)
            dh = part if dh is None else dh + part
        xv = x_ref[...]
        gv = g_ref[...]
        r = _rstd(xv)
        xh = xv * r
        h_ref[...] = (xh * gv).astype(BF16)
        dg_ref[...] += _sum8(dh * xh)
        dxh = dh * gv
        dx_ref[...] = dxo_ref[...] + r * (dxh - xh * jnp.mean(dxh * xh, axis=-1, keepdims=True))

    row = pl.BlockSpec((tt, D), lambda i: (i, 0))
    return _token_call(
        body,
        name,
        T // tt,
        [row, row, _resident((1, D))] + [pl.BlockSpec((tt, N), lambda i: (i, 0))] * npairs + [_resident((N, D))] * npairs,
        [dxo, x, g, *[p[0] for p in pairs], *[p[1] for p in pairs]],
        [row, row, pl.BlockSpec((8, D), lambda i: (0, 0))],
        [jax.ShapeDtypeStruct((T, D), F32), jax.ShapeDtypeStruct((T, D), BF16), jax.ShapeDtypeStruct((8, D), F32)],
        [],
        comm,
    )


def wgrad_tn(a, b, tk, name):
    T, M = a.shape
    N = b.shape[1]
    tm = M if M * N * 4 <= WGRAD_ACC_BYTES else M // 2
    nt = T // tk

    def body(a_ref, b_ref, o_ref, acc_ref):
        k = pl.program_id(1)

        @pl.when(k == 0)
        def _():
            acc_ref[...] = jnp.zeros_like(acc_ref)

        acc_ref[...] += lax.dot_general(a_ref[...], b_ref[...], TN, preferred_element_type=F32)

        @pl.when(k == nt - 1)
        def _():
            o_ref[...] = acc_ref[...].astype(BF16)

    return _pcall(
        body,
        name=name,
        grid=(M // tm, nt),
        in_specs=[pl.BlockSpec((tk, tm), lambda i, k: (k, i)), pl.BlockSpec((tk, N), lambda i, k: (k, 0))],
        out_specs=pl.BlockSpec((tm, N), lambda i, k: (i, 0)),
        out_shape=jax.ShapeDtypeStruct((M, N), BF16),
        scratch_shapes=[pltpu.VMEM((tm, N), F32)],
        compiler_params=_cparams(2),
    )(a, b)


def _sgu_recompute(z_ref, lng_ref, lnb_ref, S):
    z = z_ref[...].astype(F32)
    zu, zv = z[:, :S], z[:, S:]
    tu, tv = _gelu_tanh(zu), _gelu_tanh(zv)
    u = 0.5 * zu * (1.0 + tu)
    v = 0.5 * zv * (1.0 + tv)
    vc = v - jnp.mean(v, axis=-1, keepdims=True)
    rstd = lax.rsqrt(jnp.mean(vc * vc, axis=-1, keepdims=True) + EPS)
    vhat = vc * rstd
    vln = (vhat * lng_ref[...] + lnb_ref[...]).astype(BF16)
    return zu, zv, tu, tv, u, vhat, rstd, vln


def sgu_mix_fwd(zpre, lng, lnb, ws, bias_full, tt, name, comm=None):
    T, S2 = zpre.shape
    S = S2 // 2
    dg = S // GROUPS

    def body(z_ref, lng_ref, lnb_ref, ws_ref, bias_ref, o_ref):
        _, _, _, _, u, _, _, vln = _sgu_recompute(z_ref, lng_ref, lnb_ref, S)
        for c in range(tt // CHUNK):
            rows = slice(c * CHUNK, (c + 1) * CHUNK)
            for gi in range(GROUPS):
                cols = slice(gi * dg, (gi + 1) * dg)
                mixed = jnp.dot(ws_ref[gi], vln[rows, cols], preferred_element_type=F32) + bias_ref[:, cols]
                o_ref[rows, cols] = (u[rows, cols] * mixed).astype(BF16)

    vec = pl.BlockSpec((1, S), lambda i: (0, 0))
    return _token_call(
        body,
        name,
        T // tt,
        [
            pl.BlockSpec((tt, S2), lambda i: (i, 0)),
            vec,
            vec,
            pl.BlockSpec((GROUPS, CHUNK, CHUNK), lambda i: (0, 0, 0)),
            pl.BlockSpec((CHUNK, S), lambda i: (0, 0)),
        ],
        [zpre, lng, lnb, ws, bias_full],
        [pl.BlockSpec((tt, S), lambda i: (i, 0))],
        [jax.ShapeDtypeStruct((T, S), BF16)],
        [],
        comm,
    )


def sgu_mix_bwd(dgated, zpre, lng, lnb, ws, ws_t, bias_full, tt, name, comm=None):
    T, S2 = zpre.shape
    S = S2 // 2
    dg = S // GROUPS

    def body(dgt_ref, z_ref, lng_ref, lnb_ref, ws_ref, wst_ref, bias_ref, dz_ref, dws_ref, dbias_ref, dlng_ref, dlnb_ref, du_ref, dvln_ref):
        @pl.when(pl.program_id(0) == 0)
        def _():
            dws_ref[...] = jnp.zeros_like(dws_ref)
            dbias_ref[...] = jnp.zeros_like(dbias_ref)
            dlng_ref[...] = jnp.zeros_like(dlng_ref)
            dlnb_ref[...] = jnp.zeros_like(dlnb_ref)

        zu, zv, tu, tv, u, vhat, rstd, vln = _sgu_recompute(z_ref, lng_ref, lnb_ref, S)
        for c in range(tt // CHUNK):
            rows = slice(c * CHUNK, (c + 1) * CHUNK)
            for gi in range(GROUPS):
                cols = slice(gi * dg, (gi + 1) * dg)
                v_cg = vln[rows, cols]
                mixed = jnp.dot(ws_ref[gi], v_cg, preferred_element_type=F32) + bias_ref[:, cols]
                dgt = dgt_ref[rows, cols].astype(F32)
                du_ref[rows, cols] = dgt * mixed
                dmx = dgt * u[rows, cols]
                dbias_ref[:, cols] += dmx
                dmx16 = dmx.astype(BF16)
                dws_ref[gi] += lax.dot_general(dmx16, v_cg, NT, preferred_element_type=F32)
                dvln_ref[rows, cols] = jnp.dot(wst_ref[gi], dmx16, preferred_element_type=F32)

        dvln = dvln_ref[...]
        dlng_ref[...] += _sum8(dvln * vhat)
        dlnb_ref[...] += _sum8(dvln)
        dvhat = dvln * lng_ref[...]
        dv = rstd * (dvhat - jnp.mean(dvhat, axis=-1, keepdims=True) - vhat * jnp.mean(dvhat * vhat, axis=-1, keepdims=True))
        dz_ref[:, :S] = (du_ref[...] * _gelu_grad(zu, tu)).astype(BF16)
        dz_ref[:, S:] = (dv * _gelu_grad(zv, tv)).astype(BF16)

    vec = pl.BlockSpec((1, S), lambda i: (0, 0))
    wsp = pl.BlockSpec((GROUPS, CHUNK, CHUNK), lambda i: (0, 0, 0))
    full = pl.BlockSpec((CHUNK, S), lambda i: (0, 0))
    acc8 = pl.BlockSpec((8, S), lambda i: (0, 0))
    return _token_call(
        body,
        name,
        T // tt,
        [pl.BlockSpec((tt, S), lambda i: (i, 0)), pl.BlockSpec((tt, S2), lambda i: (i, 0)), vec, vec, wsp, wsp, full],
        [dgated, zpre, lng, lnb, ws, ws_t, bias_full],
        [pl.BlockSpec((tt, S2), lambda i: (i, 0)), wsp, full, acc8, acc8],
        [
            jax.ShapeDtypeStruct((T, S2), BF16),
            jax.ShapeDtypeStruct((GROUPS, CHUNK, CHUNK), F32),
            jax.ShapeDtypeStruct((CHUNK, S), F32),
            jax.ShapeDtypeStruct((8, S), F32),
            jax.ShapeDtypeStruct((8, S), F32),
        ],
        [pltpu.VMEM((tt, S), F32), pltpu.VMEM((tt, S), F32)],
        comm,
    )


def _shifted_planes(sh, tt):
    n = tt + HALO - SUBLANES
    for s in range(1, SUBLANES):
        sh[s, 0:n, :] = sh[0, pl.ds(s, n), :]


def conv_mid_fwd(p, wdw, bdw, lng, lnb, tt, name, comm=None):
    T, C2 = p.shape
    C = C2 // 2

    def body(p_ref, w_ref, b_ref, lng_ref, lnb_ref, yc_ref, ys_ref, sh):
        @pl.when(pl.program_id(0) == 0)
        def _():
            sh[0, 0:HALO, :] = jnp.zeros((HALO, C), F32)

        @pl.when(pl.program_id(0) > 0)
        def _():
            sh[0, 0:HALO, :] = sh[0, tt : tt + HALO, :]

        pv = p_ref[...].astype(F32)
        sh[0, HALO:, :] = pv[:, :C] * _sigmoid(pv[:, C:])
        _shifted_planes(sh, tt)
        for lb in range(C // LANES):
            lanes = slice(lb * LANES, (lb + 1) * LANES)
            wk = [jnp.broadcast_to(w_ref[k : k + 1, lanes], (SUBLANES, LANES)) for k in range(CONV_W)]
            bias = jnp.broadcast_to(b_ref[:, lanes], (SUBLANES, LANES))

            def rows(i, carry):
                r0 = pl.multiple_of(i * CONV_ROWS, CONV_ROWS)
                accs = [[bias, jnp.zeros((SUBLANES, LANES), F32)] for _ in range(CONV_ROWS // SUBLANES)]
                for k in range(CONV_W):
                    o = HALO - (CONV_W - 1) + k
                    for j, acc in enumerate(accs):
                        blk = sh[o % SUBLANES, pl.ds(r0 + (o // SUBLANES + j) * SUBLANES, SUBLANES), lanes]
                        acc[k % 2] = acc[k % 2] + wk[k] * blk
                for j, acc in enumerate(accs):
                    yc_ref[pl.ds(r0 + j * SUBLANES, SUBLANES), lanes] = acc[0] + acc[1]
                return carry

            lax.fori_loop(0, tt // CONV_ROWS, rows, 0)
        acc = yc_ref[...]
        yc = acc - jnp.mean(acc, axis=-1, keepdims=True)
        yn = yc * lax.rsqrt(jnp.mean(yc * yc, axis=-1, keepdims=True) + EPS) * lng_ref[...] + lnb_ref[...]
        ys_ref[...] = (yn * _sigmoid(yn)).astype(BF16)

    vec = pl.BlockSpec((1, C), lambda i: (0, 0))
    row = pl.BlockSpec((tt, C), lambda i: (i, 0))
    return _token_call(
        body,
        name,
        T // tt,
        [pl.BlockSpec((tt, C2), lambda i: (i, 0)), pl.BlockSpec((HALO, C), lambda i: (0, 0)), vec, vec, vec],
        [p, wdw, bdw, lng, lnb],
        [row, row],
        [jax.ShapeDtypeStruct((T, C), F32), jax.ShapeDtypeStruct((T, C), BF16)],
        [pltpu.VMEM((SUBLANES, tt + HALO, C), F32)],
        comm,
    )


def conv_mid_bwd(dys, yc, p, wdw, lng, lnb, tt, name, comm=None):
    T, C2 = p.shape
    C = C2 // 2
    n = T // tt

    def body(dys_ref, yc_ref, p_ref, w_ref, lng_ref, lnb_ref, dp_ref, dw_ref, db_ref, dlng_ref, dlnb_ref, sh, y_s, dy_s):
        @pl.when(pl.program_id(0) == 0)
        def _():
            sh[0, tt : tt + HALO, :] = jnp.zeros((HALO, C), F32)
            dw_ref[...] = jnp.zeros_like(dw_ref)
            db_ref[...] = jnp.zeros_like(db_ref)
            dlng_ref[...] = jnp.zeros_like(dlng_ref)
            dlnb_ref[...] = jnp.zeros_like(dlnb_ref)

        @pl.when(pl.program_id(0) > 0)
        def _():
            sh[0, tt : tt + HALO, :] = sh[0, 0:HALO, :]

        ycv = yc_ref[...]
        ycc = ycv - jnp.mean(ycv, axis=-1, keepdims=True)
        rstd = lax.rsqrt(jnp.mean(ycc * ycc, axis=-1, keepdims=True) + EPS)
        yhat = ycc * rstd
        lng_v = lng_ref[...]
        yn = yhat * lng_v + lnb_ref[...]
        sg = _sigmoid(yn)
        dyn = dys_ref[...].astype(F32) * (sg * (1.0 + yn * (1.0 - sg)))
        dlng_ref[...] += _sum8(dyn * yhat)
        dlnb_ref[...] += _sum8(dyn)
        dyh = dyn * lng_v
        dyc = rstd * (dyh - jnp.mean(dyh, axis=-1, keepdims=True) - yhat * jnp.mean(dyh * yhat, axis=-1, keepdims=True))
        db_ref[...] += _sum8(dyc)
        sh[0, 0:tt, :] = dyc
        _shifted_planes(sh, tt)

        pv = p_ref[...].astype(F32)
        a = pv[:, :C]
        sgate = _sigmoid(pv[:, C:])
        y_s[...] = a * sgate
        for lb in range(C // LANES):
            lanes = slice(lb * LANES, (lb + 1) * LANES)
            for k0, k1 in ((0, CONV_W // 2), (CONV_W // 2, CONV_W)):
                wk = [jnp.broadcast_to(w_ref[k : k + 1, lanes], (SUBLANES, LANES)) for k in range(k0, k1)]

                def rows(i, dw_acc, k0=k0, k1=k1, wk=wk, lanes=lanes):
                    r0 = pl.multiple_of(i * CONV_ROWS, CONV_ROWS)
                    dw_acc = list(dw_acc)
                    zero = jnp.zeros((SUBLANES, LANES), F32)
                    ybs, accs = [], []
                    for j in range(CONV_ROWS // SUBLANES):
                        at = pl.ds(r0 + j * SUBLANES, SUBLANES)
                        ybs.append(y_s[at, lanes])
                        accs.append([zero if k0 == 0 else dy_s[at, lanes], zero])
                    for n_k, k in enumerate(range(k0, k1)):
                        o = CONV_W - 1 - k
                        for j, acc in enumerate(accs):
                            blk = sh[o % SUBLANES, pl.ds(r0 + (o // SUBLANES + j) * SUBLANES, SUBLANES), lanes]
                            acc[n_k % 2] = acc[n_k % 2] + wk[n_k] * blk
                            dw_acc[n_k] = dw_acc[n_k] + ybs[j] * blk
                    for j, acc in enumerate(accs):
                        dy_s[pl.ds(r0 + j * SUBLANES, SUBLANES), lanes] = acc[0] + acc[1]
                    return tuple(dw_acc)

                dw_acc = lax.fori_loop(0, tt // CONV_ROWS, rows, tuple(jnp.zeros((SUBLANES, LANES), F32) for _ in range(k0, k1)))
                for n_k, k in enumerate(range(k0, k1)):
                    dw_ref[SUBLANES * k : SUBLANES * (k + 1), lanes] += dw_acc[n_k]
        dy = dy_s[...]
        dp_ref[:, :C] = (dy * sgate).astype(BF16)
        dp_ref[:, C:] = (dy * a * sgate * (1.0 - sgate)).astype(BF16)

    vec = pl.BlockSpec((1, C), lambda i: (0, 0))
    row = pl.BlockSpec((tt, C), lambda i: (n - 1 - i, 0))
    row2 = pl.BlockSpec((tt, C2), lambda i: (n - 1 - i, 0))
    acc8 = pl.BlockSpec((8, C), lambda i: (0, 0))
    return _token_call(
        body,
        name,
        n,
        [row, row, row2, pl.BlockSpec((HALO, C), lambda i: (0, 0)), vec, vec],
        [dys, yc, p, wdw, lng, lnb],
        [row2, pl.BlockSpec((HALO * 8, C), lambda i: (0, 0)), acc8, acc8, acc8],
        [
            jax.ShapeDtypeStruct((T, C2), BF16),
            jax.ShapeDtypeStruct((HALO * 8, C), F32),
            jax.ShapeDtypeStruct((8, C), F32),
            jax.ShapeDtypeStruct((8, C), F32),
            jax.ShapeDtypeStruct((8, C), F32),
        ],
        [pltpu.VMEM((SUBLANES, tt + HALO, C), F32), pltpu.VMEM((tt, C), F32), pltpu.VMEM((tt, C), F32)],
        comm,
    )


def loss_head(y, target, tt, name):
    T, D = y.shape

    def body(y_ref, t_ref, dy_ref, sq_ref):
        @pl.when(pl.program_id(0) == 0)
        def _():
            sq_ref[...] = jnp.zeros_like(sq_ref)

        err = y_ref[...] - t_ref[...]
        dy_ref[...] = err * (1.0 / D)
        sq_ref[...] += _sum8(err * err)

    row = pl.BlockSpec((tt, D), lambda i: (i, 0))
    return _pcall(
        body,
        name=name,
        grid=(T // tt,),
        in_specs=[row, row],
        out_specs=[row, pl.BlockSpec((8, D), lambda i: (0, 0))],
        out_shape=[jax.ShapeDtypeStruct((T, D), F32), jax.ShapeDtypeStruct((8, D), F32)],
        compiler_params=_cparams(1),
    )(y, target)


def sum_slots(slots, name):
    _, r, w = slots.shape

    def body(s_ref, o_ref):
        total = s_ref[0].astype(F32)
        for p in range(1, N_DEV):
            total = total + s_ref[p].astype(F32)
        o_ref[...] = total

    return _pcall(body, name=name, out_shape=jax.ShapeDtypeStruct((r, w), F32), compiler_params=pltpu.CompilerParams(vmem_limit_bytes=VMEM_LIMIT_V7X))(slots)


def adamw(w, g, m, v, name):
    R, C = w.shape
    tr = _tile(R, 512, 8)
    c1 = 1.0 - ADAM_B1**ADAM_STEP
    c2 = 1.0 - ADAM_B2**ADAM_STEP

    def body(w_ref, g_ref, m_ref, v_ref, d_ref, mo_ref, vo_ref):
        gv = g_ref[...]
        m2 = ADAM_B1 * m_ref[...] + (1.0 - ADAM_B1) * gv
        v2 = ADAM_B2 * v_ref[...] + (1.0 - ADAM_B2) * (gv * gv)
        mo_ref[...] = m2
        vo_ref[...] = v2
        d_ref[...] = -ADAM_LR * ((m2 / c1) / (jnp.sqrt(v2 / c2) + ADAM_EPS) + ADAM_WD * w_ref[...])

    blk = pl.BlockSpec((tr, C), lambda i: (i, 0))
    return _pcall(
        body,
        name=name,
        grid=(R // tr,),
        in_specs=[blk] * 4,
        out_specs=[blk] * 3,
        out_shape=[jax.ShapeDtypeStruct((R, C), F32)] * 3,
        compiler_params=_cparams(1),
    )(w, g, m, v)


def _sublayers(depth):
    out = []
    for layer in range(depth):
        out.append(("ffn", layer, 0))
        out.append(("sgu" if layer % N_MIXERS == 0 else "conv", layer, layer // N_MIXERS))
        out.append(("ffn", layer, 1))
    return out


def _row_blocks(sub, shards):
    kind, layer, idx = sub
    if kind == "ffn":
        blk = {"gate_t": shards["ff_w_gate"][layer, idx].T, "up_t": shards["ff_w_up"][layer, idx].T, "down": shards["ff_w_down"][layer, idx]}
    elif kind == "sgu":
        blk = {"in_t": shards["sgu_w_in"][idx].T, "out": shards["sgu_w_out"][idx]}
    else:
        blk = {"pw1_t": shards["conv_w_pw1"][idx].T, "pw2": shards["conv_w_pw2"][idx]}
    return {nm: arr.astype(BF16) for nm, arr in blk.items()}


def _pad8(a):
    return jnp.pad(a, ((0, (-a.shape[0]) % 8), (0, 0)))


def kernel(x, norm_g, ff_w_gate, ff_w_up, ff_w_down, sgu_w_in, sgu_ln_g, sgu_ln_b, sgu_w_spatial, sgu_b_spatial, sgu_w_out, conv_w_pw1, conv_w_dw, conv_b_dw, conv_ln_g, conv_ln_b, conv_w_pw2, loss_target, m_norm_g, m_ff_w_gate, m_ff_w_up, m_ff_w_down, m_sgu_w_in, m_sgu_ln_g, m_sgu_ln_b, m_sgu_w_spatial, m_sgu_b_spatial, m_sgu_w_out, m_conv_w_pw1, m_conv_w_dw, m_conv_b_dw, m_conv_ln_g, m_conv_ln_b, m_conv_w_pw2, v_norm_g, v_ff_w_gate, v_ff_w_up, v_ff_w_down, v_sgu_w_in, v_sgu_ln_g, v_sgu_ln_b, v_sgu_w_spatial, v_sgu_b_spatial, v_sgu_w_out, v_conv_w_pw1, v_conv_w_dw, v_conv_b_dw, v_conv_ln_g, v_conv_ln_b, v_conv_w_pw2):
    names = ["norm_g", "ff_w_gate", "ff_w_up", "ff_w_down", "sgu_w_in", "sgu_ln_g", "sgu_ln_b", "sgu_w_spatial", "sgu_b_spatial", "sgu_w_out", "conv_w_pw1", "conv_w_dw", "conv_b_dw", "conv_ln_g", "conv_ln_b", "conv_w_pw2"]
    weights = dict(zip(names, [norm_g, ff_w_gate, ff_w_up, ff_w_down, sgu_w_in, sgu_ln_g, sgu_ln_b, sgu_w_spatial, sgu_b_spatial, sgu_w_out, conv_w_pw1, conv_w_dw, conv_b_dw, conv_ln_g, conv_ln_b, conv_w_pw2]))
    moments_m = dict(zip(names, [m_norm_g, m_ff_w_gate, m_ff_w_up, m_ff_w_down, m_sgu_w_in, m_sgu_ln_g, m_sgu_ln_b, m_sgu_w_spatial, m_sgu_b_spatial, m_sgu_w_out, m_conv_w_pw1, m_conv_w_dw, m_conv_b_dw, m_conv_ln_g, m_conv_ln_b, m_conv_w_pw2]))
    moments_v = dict(zip(names, [v_norm_g, v_ff_w_gate, v_ff_w_up, v_ff_w_down, v_sgu_w_in, v_sgu_ln_g, v_sgu_ln_b, v_sgu_w_spatial, v_sgu_b_spatial, v_sgu_w_out, v_conv_w_pw1, v_conv_w_dw, v_conv_b_dw, v_conv_ln_g, v_conv_ln_b, v_conv_w_pw2]))

    _, T, D = x.shape
    depth = norm_g.shape[0]
    n_conv = conv_w_dw.shape[0]
    n_sgu = sgu_w_in.shape[0]
    S = sgu_ln_g.shape[1]
    lanes = norm_g.shape[2]
    subs = _sublayers(depth)
    n_sub = len(subs)

    cx, cy, cc = (lax.axis_index(a) for a in MESH_AXES)
    my_block = 4 * cx + 2 * cy + cc

    blocks = [_row_blocks(sub, weights) for sub in subs]
    dw_pad = jnp.pad(conv_w_dw, ((0, 0), (0, HALO - CONV_W), (0, 0)))
    small_parts = [_pad8(p) for p in (norm_g.reshape(-1, lanes), dw_pad.reshape(-1, lanes), conv_b_dw, conv_ln_g, conv_ln_b)]
    small_rows = [p.shape[0] for p in small_parts]
    small = jnp.concatenate(small_parts, axis=0)

    def gathered(block_names, outs):
        return {nm: o.reshape(N_DEV * o.shape[1], o.shape[2]) for nm, o in zip(block_names, outs)}

    first_names = [nm for nm in blocks[0] if nm != "down"]
    first = exchange([("gather", blocks[0][nm]) for nm in first_names] + [("gather", small)], name="gather_first")
    W = [None] * n_sub
    W[0] = gathered(first_names, first[:-1])
    gsmall = first[-1]

    small_full = jnp.transpose(gsmall, (1, 0, 2)).reshape(gsmall.shape[1], N_DEV * lanes)
    so = [0]
    for r in small_rows:
        so.append(so[-1] + r)
    norm_full = small_full[so[0] : so[0] + depth * norm_g.shape[1]].reshape(depth, -1, D)
    dw_full = small_full[so[1] : so[1] + n_conv * HALO].reshape(n_conv, HALO, D)
    bdw_full, clng_full, clnb_full = (small_full[so[k] : so[k] + n_conv] for k in (2, 3, 4))

    causal = jnp.tril(jnp.ones((CHUNK, CHUNK), dtype=bool))
    ws_all = jnp.where(causal[None, None], sgu_w_spatial, 0.0).astype(BF16)
    wst_all = jnp.swapaxes(ws_all, -1, -2)
    bias_full_all = jnp.repeat(jnp.swapaxes(sgu_b_spatial, -1, -2), S // GROUPS, axis=-1)

    tt = _tile(T, 512, CHUNK)
    tt_mix = _tile(T, 256, CHUNK)

    tk = _tile(T, 1024, CHUNK)

    def vec(v):
        return v.reshape(1, -1)

    def split_first(comm):
        return (comm[:1], comm[1:]) if comm else (None, None)

    def norms(kind, layer, idx):
        pre = 4 * idx if kind == "ffn" else 2
        return vec(norm_full[layer, pre]), vec(norm_full[layer, pre + 1])

    xs = x[0]
    saved = []
    for si, (kind, layer, idx) in enumerate(subs):
        w = W[si]
        g_pre, g_post = norms(kind, layer, idx)
        nxt = [("gather", a) for a in blocks[si + 1].values()] if si + 1 < n_sub else None
        nxt_first, nxt_rest = split_first(nxt)
        if kind == "ffn":
            own = [("gather", blocks[0]["down"])] if si == 0 else []
            (silu, dact, a), got = rms_matmul_nt(xs, g_pre, [w["gate_t"], w["up_t"]], tt, "ffn_in", own + (nxt or []), swiglu=True)
            if own:
                w.update(gathered(["down"], got[:1]))
                got = got[1:]
            x_new, o = matmul_rms_res(xs, a, w["down"], g_post, FFN_SCALE, tt, "ffn_out")
            saved.append((xs, silu, dact, o, a))
        elif kind == "sgu":
            (zpre,), got = rms_matmul_nt(xs, g_pre, [w["in_t"]], tt, "sgu_in", nxt_first)
            (gated,), got_rest = sgu_mix_fwd(zpre, vec(sgu_ln_g[idx]), vec(sgu_ln_b[idx]), ws_all[idx], bias_full_all[idx], tt_mix, "sgu_mix", nxt_rest)
            got = list(got) + list(got_rest)
            x_new, mm = matmul_rms_res(xs, gated, w["out"], g_post, 1.0, tt, "sgu_out")
            saved.append((xs, zpre, gated, mm))
        else:
            (p,), got = rms_matmul_nt(xs, g_pre, [w["pw1_t"]], tt, "conv_pw1", nxt_first)
            (yc, ys), got_rest = conv_mid_fwd(p, dw_full[idx], vec(bdw_full[idx]), vec(clng_full[idx]), vec(clnb_full[idx]), tt, "conv_mid", nxt_rest)
            got = list(got) + list(got_rest)
            x_new, mm = matmul_rms_res(xs, ys, w["pw2"], g_post, 1.0, tt, "conv_pw2")
            saved.append((xs, p, yc, ys, mm))
        if nxt:
            W[si + 1] = gathered(blocks[si + 1].keys(), got)
        xs = x_new

    dx, sq = loss_head(xs, loss_target[0], tt, name="loss_head")
    loss = lax.psum(0.5 * jnp.sum(sq) / D, MESH_AXES)

    d_norm = [[None] * norm_full.shape[1] for _ in range(depth)]
    d_sgu = [None] * n_sgu
    d_conv = [None] * n_conv
    slots = [None] * n_sub
    pending = None

    def scatter_of(p):
        return [("scatter", dwm.reshape(N_DEV, dwm.shape[0] // N_DEV, D)) for dwm in p[1].values()] if p else None

    def pack_small():
        def sum8(v):
            return jnp.sum(v, axis=0)

        g_norm = jnp.stack([jnp.stack([sum8(d) for d in row]) for row in d_norm])
        g_dw = jnp.stack([jnp.sum(d[0].reshape(HALO, 8, D), axis=1) for d in d_conv])
        g_bdw, g_clng, g_clnb = (jnp.stack([sum8(d[k]) for d in d_conv]) for k in (1, 2, 3))
        g_slng, g_slnb = (jnp.stack([sum8(d[k]) for d in d_sgu]) for k in (2, 3))
        g_bsp = jnp.stack([jnp.sum(d[1].reshape(CHUNK, GROUPS, S // GROUPS), axis=-1).T for d in d_sgu])
        g_wsp = jnp.stack([jnp.where(causal[None], d[0], 0.0) for d in d_sgu])
        parts = [g_norm, g_dw, g_bdw, g_clng, g_clnb, g_slng, g_slnb, g_bsp, g_wsp]
        return parts, [p.size // D for p in parts], jnp.concatenate([_pad8(p.reshape(-1, D)) for p in parts], axis=0)

    for si in reversed(range(n_sub)):
        kind, layer, idx = subs[si]
        w = W[si]
        g_pre, g_post = norms(kind, layer, idx)
        pre = 4 * idx if kind == "ffn" else 2
        sc_first, sc_rest = split_first(scatter_of(pending))
        if kind == "ffn":
            xs_in, silu, dact, o, a = saved[si]
            (do, dg_post, dgate, dup), got = rms_bwd_matmul_nt(dx, o, g_post, w["down"], FFN_SCALE, (silu, dact), tt, "ffn_out_bwd", scatter_of(pending))
            dw_down = wgrad_tn(a, do, tk, name="wgrad_ffn_out")
            tail = None
            if si == 0:
                d_norm[layer][pre], d_norm[layer][pre + 1] = jnp.zeros_like(dg_post), dg_post
                sparts, srows, sgrad = pack_small()
                tail = scatter_of((si, {"down": dw_down})) + [("gather", sgrad)]
            (dx, h, dg_pre), got_tail = matmul_rms_bwd(dx, xs_in, g_pre, [(dgate, w["gate_t"]), (dup, w["up_t"])], tt, "ffn_in_bwd", tail)
            dws_now = {"gate_t": wgrad_tn(dgate, h, tk, name="wgrad_ffn_in"), "up_t": wgrad_tn(dup, h, tk, name="wgrad_ffn_in"), "down": dw_down}
        elif kind == "sgu":
            xs_in, zpre, gated, mm = saved[si]
            (dm, dg_post, dgated), got = rms_bwd_matmul_nt(dx, mm, g_post, w["out"], 1.0, None, tt, "sgu_out_bwd", sc_first)
            (dzpre, dws, dbias, dlng, dlnb), got_rest = sgu_mix_bwd(dgated, zpre, vec(sgu_ln_g[idx]), vec(sgu_ln_b[idx]), ws_all[idx], wst_all[idx], bias_full_all[idx], tt_mix, "sgu_mix_bwd", sc_rest)
            got = list(got) + list(got_rest)
            (dx, h, dg_pre), _ = matmul_rms_bwd(dx, xs_in, g_pre, [(dzpre, w["in_t"])], tt, "sgu_in_bwd")
            dws_now = {"in_t": wgrad_tn(dzpre, h, tt, name="wgrad_sgu_in"), "out": wgrad_tn(gated, dm, tk, name="wgrad_sgu_out")}
            d_sgu[idx] = (dws, dbias, dlng, dlnb)
        else:
            xs_in, p, yc, ys, mm = saved[si]
            (dm, dg_post, dys), got = rms_bwd_matmul_nt(dx, mm, g_post, w["pw2"], 1.0, None, tt, "conv_pw2_bwd", sc_first)
            (dp, dwdw, dbdw, dlng, dlnb), got_rest = conv_mid_bwd(dys, yc, p, dw_full[idx], vec(clng_full[idx]), vec(clnb_full[idx]), tt, "conv_mid_bwd", sc_rest)
            got = list(got) + list(got_rest)
            (dx, h, dg_pre), _ = matmul_rms_bwd(dx, xs_in, g_pre, [(dp, w["pw1_t"])], tt, "conv_pw1_bwd")
            dws_now = {"pw1_t": wgrad_tn(dp, h, tk, name="wgrad_conv_pw1"), "pw2": wgrad_tn(ys, dm, tk, name="wgrad_conv_pw2")}
            d_conv[idx] = (dwdw, dbdw, dlng, dlnb)
        d_norm[layer][pre], d_norm[layer][pre + 1] = dg_pre, dg_post
        if pending:
            slots[pending[0]] = dict(zip(pending[1].keys(), got))
        pending = (si, dws_now)
    grad_x = dx[None]

    last_names = [nm for nm in pending[1] if nm != "down"]
    last = exchange(scatter_of((0, {nm: pending[1][nm] for nm in last_names})) + [("gather", dg_pre)], name="scatter_last")
    slots[0] = dict(zip(last_names, last[:-1]))
    slots[0]["down"] = got_tail[0]
    stotal = sum_slots(got_tail[1], name="sum_slots")
    g_first = jnp.sum(sum_slots(last[-1], name="sum_slots"), axis=0)

    gs = {(si, nm): sum_slots(s, name="sum_slots") for si in range(n_sub) for nm, s in slots[si].items()}
    grads = {}
    ffn_si = {(layer, idx): si for si, (kind, layer, idx) in enumerate(subs) if kind == "ffn"}
    sgu_si = {idx: si for si, (kind, layer, idx) in enumerate(subs) if kind == "sgu"}
    conv_si = {idx: si for si, (kind, layer, idx) in enumerate(subs) if kind == "conv"}
    grads["ff_w_gate"] = jnp.stack([jnp.stack([gs[(ffn_si[(l, f)], "gate_t")].T for f in range(2)]) for l in range(depth)])
    grads["ff_w_up"] = jnp.stack([jnp.stack([gs[(ffn_si[(l, f)], "up_t")].T for f in range(2)]) for l in range(depth)])
    grads["ff_w_down"] = jnp.stack([jnp.stack([gs[(ffn_si[(l, f)], "down")] for f in range(2)]) for l in range(depth)])
    grads["sgu_w_in"] = jnp.stack([gs[(sgu_si[j], "in_t")].T for j in range(n_sgu)])
    grads["sgu_w_out"] = jnp.stack([gs[(sgu_si[j], "out")] for j in range(n_sgu)])
    grads["conv_w_pw1"] = jnp.stack([gs[(conv_si[j], "pw1_t")].T for j in range(n_conv)])
    grads["conv_w_pw2"] = jnp.stack([gs[(conv_si[j], "pw2")] for j in range(n_conv)])

    so = [0]
    for r in srows:
        so.append(so[-1] + r + (-r) % 8)
    sp = [stotal[so[k] : so[k] + srows[k]].reshape(sparts[k].shape) for k in range(len(sparts))]

    def my_lanes(v):
        return lax.dynamic_slice_in_dim(v, my_block * lanes, lanes, axis=-1)

    grads["norm_g"] = my_lanes(sp[0].at[0, 0].set(g_first))
    grads["conv_w_dw"] = my_lanes(sp[1])[:, :CONV_W]
    grads["conv_b_dw"] = my_lanes(sp[2])
    grads["conv_ln_g"] = my_lanes(sp[3])
    grads["conv_ln_b"] = my_lanes(sp[4])
    grads["sgu_ln_g"], grads["sgu_ln_b"], grads["sgu_b_spatial"], grads["sgu_w_spatial"] = sp[5], sp[6], sp[7], sp[8]

    deltas, new_m, new_v = {}, {}, {}
    for nm in names:
        w = weights[nm]
        two_d = (-1, w.shape[-1])
        d, m2, v2 = adamw(w.reshape(two_d), grads[nm].reshape(two_d), moments_m[nm].reshape(two_d), moments_v[nm].reshape(two_d), name="adamw")
        deltas[nm], new_m[nm], new_v[nm] = d.reshape(w.shape), m2.reshape(w.shape), v2.reshape(w.shape)

    return (loss, grad_x, *[grads[n] for n in names], *[deltas[n] for n in names], *[new_m[n] for n in names], *[new_v[n] for n in names])
```

```python
import jax
import jax.numpy as jnp
from jax import lax
from jax.experimental import pallas as pl
from jax.experimental.pallas import tpu as pltpu

F32 = jnp.float32
BF16 = jnp.bfloat16

EPS = 1e-6
FFN_SCALE = 0.5
N_MIXERS = 2
CHUNK = 128
GROUPS = 8
CONV_W = 31
HALO = 32
GELU_C0 = 0.7978845608028654
GELU_C1 = 0.044715
ADAM_LR, ADAM_B1, ADAM_B2, ADAM_EPS, ADAM_WD, ADAM_STEP = 0.001, 0.9, 0.999, 1e-08, 0.01, 10

MESH_AXES = ("x", "y", "c")
N_DEV = 8
VMEM_LIMIT_V7X = 56 * 1024 * 1024
WGRAD_ACC_BYTES = 16 * 1024 * 1024
MXU_COLS_V7X = 256
SUBLANES, LANES = 8, 128
CONV_ROWS = 32

NT = (((1,), (1,)), ((), ()))
TN = (((0,), (0,)), ((), ()))

HBM_SPEC = pl.BlockSpec(memory_space=pl.ANY)
MESH_ID = pl.DeviceIdType.MESH


def _pcall(body, **kw):
    return pl.pallas_call(body, **kw)


def _cparams(n_axes):
    return pltpu.CompilerParams(dimension_semantics=("arbitrary",) * n_axes, vmem_limit_bytes=VMEM_LIMIT_V7X)


def _tile(n, pref, align):
    if n <= pref:
        return n
    t = (pref // align) * align
    while t > align and n % t:
        t -= align
    assert n % t == 0, (n, pref, align)
    return t


def _rstd(x):
    return lax.rsqrt(jnp.mean(x * x, axis=-1, keepdims=True) + EPS)


def _sum8(v):
    t, d = v.shape
    return v.reshape(t // 8, 8, d).sum(axis=0)


def _sigmoid(x):
    return jax.nn.sigmoid(x)


def _gelu_tanh(z):
    return jnp.tanh(GELU_C0 * (z + GELU_C1 * z * z * z))


def _gelu_grad(z, t):
    return 0.5 * (1.0 + t) + 0.5 * z * (1.0 - t * t) * (GELU_C0 * (1.0 + 3.0 * GELU_C1 * z * z))


def _resident(shape):
    return pl.BlockSpec(shape, lambda i: (0,) * len(shape), pipeline_mode=pl.Buffered(1))


def _exchange_io(comm):
    n = len(comm)
    out_shape = [jax.ShapeDtypeStruct(((N_DEV,) + a.shape) if kind == "gather" else a.shape, a.dtype) for kind, a in comm]
    scratch = [pltpu.SemaphoreType.DMA((n, N_DEV - 1)), pltpu.SemaphoreType.DMA((n, N_DEV - 1)), pltpu.SemaphoreType.DMA((n,))]
    return [HBM_SPEC] * n, [HBM_SPEC] * n, out_shape, scratch


def _exchange_copies(kinds, in_refs, out_refs, send_sems, recv_sems, local_sems, with_arrivals=True):
    x, y, c = (lax.axis_index(a) for a in MESH_AXES)
    me = 4 * x + 2 * y + c
    local, sends, arrivals = [], [], []
    for a, kind in enumerate(kinds):
        src, dst = in_refs[a], out_refs[a]
        gather = kind == "gather"
        local.append(pltpu.make_async_copy(src if gather else src.at[me], dst.at[me], local_sems.at[a]))
        for k in range(N_DEV - 1):
            mask = k + 1
            px = 1 - x if mask & 4 else x
            py = 1 - y if mask & 2 else y
            pc = 1 - c if mask & 1 else c
            peer = 4 * px + 2 * py + pc
            block = src if gather else src.at[peer]
            for into, lst in ((me, sends), (peer, arrivals)):
                if lst is arrivals and not with_arrivals:
                    continue
                lst.append(
                    pltpu.make_async_remote_copy(
                        src_ref=block, dst_ref=dst.at[into], send_sem=send_sems.at[a, k], recv_sem=recv_sems.at[a, k], device_id=(px, py, pc), device_id_type=MESH_ID
                    )
                )
    return local, sends, arrivals


def _exchange_start(copies):
    local, sends, _ = copies
    for cp in local + sends:
        cp.start()


def _exchange_finish(copies):
    local, sends, arrivals = copies
    for cp in arrivals:
        cp.wait_recv()
    for cp in sends:
        cp.wait_send()
    for cp in local:
        cp.wait()


def exchange(comm, name):
    kinds = [k for k, _ in comm]
    n = len(comm)
    in_specs, out_specs, out_shape, scratch = _exchange_io(comm)

    def body(*refs):
        copies = _exchange_copies(kinds, refs[:n], refs[n : 2 * n], *refs[2 * n :])
        _exchange_start(copies)
        _exchange_finish(copies)

    return _pcall(body, name=name, in_specs=in_specs, out_specs=out_specs, out_shape=out_shape, scratch_shapes=scratch)(*[a for _, a in comm])


def gather_two_level(packs, name):
    n = len(packs)

    def body(*refs):
        in_refs, out_refs = refs[:n], refs[n : 2 * n]
        send_sems, recv_sems, local_sems = refs[2 * n :]
        x, y, c = (lax.axis_index(a) for a in MESH_AXES)
        chips = [(1 - x, y), (x, 1 - y), (1 - x, 1 - y)]
        me, sibling = (x, y, c), (x, y, 1 - c)

        def slot(a, px, py, pc):
            return out_refs[a].at[4 * px + 2 * py + pc]

        def copy(a, k, block, to, from_input=False):
            return pltpu.make_async_remote_copy(
                src_ref=in_refs[a] if from_input else slot(a, *block),
                dst_ref=slot(a, *block),
                send_sem=send_sems.at[a, k],
                recv_sem=recv_sems.at[a, k],
                device_id=to,
                device_id_type=MESH_ID,
            )

        mine = [pltpu.make_async_copy(in_refs[a], slot(a, *me), local_sems.at[a]) for a in range(n)]
        for cp in mine:
            cp.start()
        first = []
        for a in range(n):
            first.append(copy(a, 0, me, sibling, from_input=True))
            first += [copy(a, 1 + j, me, (*chip, c), from_input=True) for j, chip in enumerate(chips)]
        for cp in first:
            cp.start()
        passed = []
        for j, chip in enumerate(chips):
            for a in range(n):
                copy(a, 1 + j, (*chip, c), me).wait_recv()
                fwd = copy(a, 4 + j, (*chip, c), sibling)
                fwd.start()
                passed.append(fwd)
        for a in range(n):
            copy(a, 0, sibling, me).wait_recv()
            for j, chip in enumerate(chips):
                copy(a, 4 + j, (*chip, 1 - c), me).wait_recv()
        for cp in first + passed:
            cp.wait_send()
        for cp in mine:
            cp.wait()

    return _pcall(
        body,
        name=name,
        in_specs=[HBM_SPEC] * n,
        out_specs=[HBM_SPEC] * n,
        out_shape=[jax.ShapeDtypeStruct((N_DEV,) + p.shape, p.dtype) for p in packs],
        scratch_shapes=[pltpu.SemaphoreType.DMA((n, N_DEV - 1)), pltpu.SemaphoreType.DMA((n, N_DEV - 1)), pltpu.SemaphoreType.DMA((n,))],
    )(*packs)


def _token_call(body, name, n_tiles, in_specs, inputs, out_specs, out_shape, scratch, comm):
    n_in, n_out, n_scr = len(inputs), len(out_shape), len(scratch)
    comm = comm or []
    nc = len(comm)
    full_body = body
    if nc:
        kinds = [k for k, _ in comm]
        c_in, c_out, c_shape, c_scr = _exchange_io(comm)
        in_specs, out_specs, out_shape, scratch = in_specs + c_in, out_specs + c_out, out_shape + c_shape, scratch + c_scr

        def full_body(*refs):
            ins, cins = refs[:n_in], refs[n_in : n_in + nc]
            o0 = n_in + nc
            outs, couts = refs[o0 : o0 + n_out], refs[o0 + n_out : o0 + n_out + nc]
            s0 = o0 + n_out + nc
            scr, sems = refs[s0 : s0 + n_scr], refs[s0 + n_scr :]

            @pl.when(pl.program_id(0) == 0)
            def _():
                _exchange_start(_exchange_copies(kinds, cins, couts, *sems, with_arrivals=False))

            body(*ins, *outs, *scr)

            @pl.when(pl.program_id(0) == n_tiles - 1)
            def _():
                _exchange_finish(_exchange_copies(kinds, cins, couts, *sems))

    res = _pcall(
        full_body, name=name, grid=(n_tiles,), in_specs=in_specs, out_specs=out_specs, out_shape=out_shape, scratch_shapes=scratch, compiler_params=_cparams(1)
    )(*inputs, *[a for _, a in comm])
    return res[:n_out], res[n_out:]


def rms_matmul_nt(x, g, wts, tt, name, comm=None, swiglu=False):
    T, D = x.shape
    N = wts[0].shape[0]
    nw = len(wts)
    n_out = 3 if swiglu else nw
    nc = _tile(N, MXU_COLS_V7X if swiglu else 1408, 128)

    def body(*refs):
        x_ref, g_ref = refs[0], refs[1]
        w_refs, o_refs = refs[2 : 2 + nw], refs[2 + nw : 2 + nw + n_out]
        xv = x_ref[...]
        h = (xv * _rstd(xv) * g_ref[...]).astype(BF16)
        if swiglu:
            silu_ref, dact_ref, act_ref = o_refs
            for j in range(N // nc):
                cols = slice(j * nc, (j + 1) * nc)
                gate = lax.dot_general(h, w_refs[0][cols, :], NT, preferred_element_type=F32)
                up = lax.dot_general(h, w_refs[1][cols, :], NT, preferred_element_type=F32)
                s = _sigmoid(gate)
                t = gate * s
                silu_ref[:, cols] = t.astype(BF16)
                act_ref[:, cols] = (t * up).astype(BF16)
                dact_ref[:, cols] = (up * (s + t - t * s)).astype(BF16)
            return
        for w_ref, o_ref in zip(w_refs, o_refs):
            for j in range(N // nc):
                cols = slice(j * nc, (j + 1) * nc)
                o_ref[:, cols] = lax.dot_general(h, w_ref[cols, :], NT, preferred_element_type=F32).astype(BF16)

    return _token_call(
        body,
        name,
        T // tt,
        [pl.BlockSpec((tt, D), lambda i: (i, 0)), _resident((1, D))] + [_resident((N, D))] * nw,
        [x, g, *wts],
        [pl.BlockSpec((tt, N), lambda i: (i, 0))] * n_out,
        [jax.ShapeDtypeStruct((T, N), BF16)] * n_out,
        [],
        comm,
    )


def matmul_rms_res(x, a, w, g, scale, tt, name, target=None):
    T, D = x.shape
    K = w.shape[0]
    loss = target is not None

    def body(*refs):
        if loss:
            x_ref, a_ref, w_ref, g_ref, t_ref, dy_ref, m_ref, sq_ref = refs

            @pl.when(pl.program_id(0) == 0)
            def _():
                sq_ref[...] = jnp.zeros_like(sq_ref)

        else:
            x_ref, a_ref, w_ref, g_ref, y_ref, m_ref = refs
        m = jnp.dot(a_ref[...], w_ref[...], preferred_element_type=F32)
        m_ref[...] = m.astype(BF16)
        y = x_ref[...] + scale * (m * _rstd(m) * g_ref[...])
        if loss:
            err = y - t_ref[...]
            dy_ref[...] = err * (1.0 / D)
            sq_ref[...] += _sum8(err * err)
        else:
            y_ref[...] = y

    row = pl.BlockSpec((tt, D), lambda i: (i, 0))
    outs, _ = _token_call(
        body,
        name,
        T // tt,
        [row, pl.BlockSpec((tt, K), lambda i: (i, 0)), _resident((K, D)), _resident((1, D))] + ([row] if loss else []),
        [x, a, w, g] + ([target] if loss else []),
        [row, row] + ([pl.BlockSpec((8, D), lambda i: (0, 0))] if loss else []),
        [jax.ShapeDtypeStruct((T, D), F32), jax.ShapeDtypeStruct((T, D), BF16)] + ([jax.ShapeDtypeStruct((8, D), F32)] if loss else []),
        [],
        None,
    )
    return outs


def rms_bwd_matmul_nt(dxo, m, g, w, scale, gate_up, tt, name, comm=None):
    T, D = dxo.shape
    K = w.shape[0]
    act = gate_up is not None
    ec = _tile(K, MXU_COLS_V7X if act else 1024, 128)

    def body(*refs):
        if act:
            dxo_ref, m_ref, g_ref, w_ref, silu_ref, dact_ref, dm_ref, dg_ref, dgate_ref, dup_ref = refs
        else:
            dxo_ref, m_ref, g_ref, w_ref, dm_ref, dg_ref, da_ref = refs

        @pl.when(pl.program_id(0) == 0)
        def _():
            dg_ref[...] = jnp.zeros_like(dg_ref)

        dy = scale * dxo_ref[...]
        mv = m_ref[...].astype(F32)
        r = _rstd(mv)
        mh = mv * r
        dg_ref[...] += _sum8(dy * mh)
        dmh = dy * g_ref[...]
        dm = (r * (dmh - mh * jnp.mean(dmh * mh, axis=-1, keepdims=True))).astype(BF16)
        dm_ref[...] = dm
        for j in range(K // ec):
            cols = slice(j * ec, (j + 1) * ec)
            da = lax.dot_general(dm, w_ref[cols, :], NT, preferred_element_type=F32)
            if act:
                dup_ref[:, cols] = (da * silu_ref[:, cols].astype(F32)).astype(BF16)
                dgate_ref[:, cols] = (da * dact_ref[:, cols].astype(F32)).astype(BF16)
            else:
                da_ref[:, cols] = da.astype(BF16)

    row = pl.BlockSpec((tt, D), lambda i: (i, 0))
    col = pl.BlockSpec((tt, K), lambda i: (i, 0))
    n_col = 2 if act else 1
    return _token_call(
        body,
        name,
        T // tt,
        [row, row, _resident((1, D)), _resident((K, D))] + ([col, col] if act else []),
        [dxo, m, g, w, *(gate_up or ())],
        [row, pl.BlockSpec((8, D), lambda i: (0, 0))] + [col] * n_col,
        [jax.ShapeDtypeStruct((T, D), BF16), jax.ShapeDtypeStruct((8, D), F32)] + [jax.ShapeDtypeStruct((T, K), BF16)] * n_col,
        [],
        comm,
    )


def matmul_rms_bwd(dxo, x, g, pairs, tt, name, comm=None):
    T, D = x.shape
    N = pairs[0][0].shape[1]
    npairs = len(pairs)

    def body(*refs):
        dxo_ref, x_ref, g_ref = refs[:3]
        dz_refs = refs[3 : 3 + npairs]
        wt_refs = refs[3 + npairs : 3 + 2 * npairs]
        dx_ref, h_ref, dg_ref = refs[3 + 2 * npairs :]

        @pl.when(pl.program_id(0) == 0)
        def _():
            dg_ref[...] = jnp.zeros_like(dg_ref)

        dh = None
        for dz_ref, wt_ref in zip(dz_refs, wt_refs):
            part = jnp.dot(dz_ref[...], wt_ref[...], preferred_element_type=F32)
            dh = part if dh is None else dh + part
        xv = x_ref[...]
        gv = g_ref[...]
        r = _rstd(xv)
        xh = xv * r
        h_ref[...] = (xh * gv).astype(BF16)
        dg_ref[...] += _sum8(dh * xh)
        dxh = dh * gv
        dx_ref[...] = dxo_ref[...] + r * (dxh - xh * jnp.mean(dxh * xh, axis=-1, keepdims=True))

    row = pl.BlockSpec((tt, D), lambda i: (i, 0))
    return _token_call(
        body,
        name,
        T // tt,
        [row, row, _resident((1, D))] + [pl.BlockSpec((tt, N), lambda i: (i, 0))] * npairs + [_resident((N, D))] * npairs,
        [dxo, x, g, *[p[0] for p in pairs], *[p[1] for p in pairs]],
        [row, row, pl.BlockSpec((8, D), lambda i: (0, 0))],
        [jax.ShapeDtypeStruct((T, D), F32), jax.ShapeDtypeStruct((T, D), BF16), jax.ShapeDtypeStruct((8, D), F32)],
        [],
        comm,
    )


def wgrad_tn(a, b, tk, name):
    T, M = a.shape
    N = b.shape[1]
    tm = M if M * N * 4 <= WGRAD_ACC_BYTES else M // 2
    nt = T // tk

    def body(a_ref, b_ref, o_ref, acc_ref):
        k = pl.program_id(1)

        @pl.when(k == 0)
        def _():
            acc_ref[...] = jnp.zeros_like(acc_ref)

        acc_ref[...] += lax.dot_general(a_ref[...], b_ref[...], TN, preferred_element_type=F32)

        @pl.when(k == nt - 1)
        def _():
            o_ref[...] = acc_ref[...].astype(BF16)

    return _pcall(
        body,
        name=name,
        grid=(M // tm, nt),
        in_specs=[pl.BlockSpec((tk, tm), lambda i, k: (k, i)), pl.BlockSpec((tk, N), lambda i, k: (k, 0))],
        out_specs=pl.BlockSpec((tm, N), lambda i, k: (i, 0)),
        out_shape=jax.ShapeDtypeStruct((M, N), BF16),
        scratch_shapes=[pltpu.VMEM((tm, N), F32)],
        compiler_params=_cparams(2),
    )(a, b)


def _sgu_recompute(z_ref, lng_ref, lnb_ref, S):
    z = z_ref[...].astype(F32)
    zu, zv = z[:, :S], z[:, S:]
    tu, tv = _gelu_tanh(zu), _gelu_tanh(zv)
    u = 0.5 * zu * (1.0 + tu)
    v = 0.5 * zv * (1.0 + tv)
    vc = v - jnp.mean(v, axis=-1, keepdims=True)
    rstd = lax.rsqrt(jnp.mean(vc * vc, axis=-1, keepdims=True) + EPS)
    vhat = vc * rstd
    vln = (vhat * lng_ref[...] + lnb_ref[...]).astype(BF16)
    return zu, zv, tu, tv, u, vhat, rstd, vln


def sgu_mix_fwd(zpre, lng, lnb, ws, bias_full, tt, name, comm=None):
    T, S2 = zpre.shape
    S = S2 // 2
    dg = S // GROUPS

    def body(z_ref, lng_ref, lnb_ref, ws_ref, bias_ref, o_ref):
        _, _, _, _, u, _, _, vln = _sgu_recompute(z_ref, lng_ref, lnb_ref, S)
        for c in range(tt // CHUNK):
            rows = slice(c * CHUNK, (c + 1) * CHUNK)
            for gi in range(GROUPS):
                cols = slice(gi * dg, (gi + 1) * dg)
                mixed = jnp.dot(ws_ref[gi], vln[rows, cols], preferred_element_type=F32) + bias_ref[:, cols]
                o_ref[rows, cols] = (u[rows, cols] * mixed).astype(BF16)

    vec = pl.BlockSpec((1, S), lambda i: (0, 0))
    return _token_call(
        body,
        name,
        T // tt,
        [
            pl.BlockSpec((tt, S2), lambda i: (i, 0)),
            vec,
            vec,
            pl.BlockSpec((GROUPS, CHUNK, CHUNK), lambda i: (0, 0, 0)),
            pl.BlockSpec((CHUNK, S), lambda i: (0, 0)),
        ],
        [zpre, lng, lnb, ws, bias_full],
        [pl.BlockSpec((tt, S), lambda i: (i, 0))],
        [jax.ShapeDtypeStruct((T, S), BF16)],
        [],
        comm,
    )


def sgu_mix_bwd(dgated, zpre, lng, lnb, ws, ws_t, bias_full, tt, name, comm=None):
    T, S2 = zpre.shape
    S = S2 // 2
    dg = S // GROUPS

    def body(dgt_ref, z_ref, lng_ref, lnb_ref, ws_ref, wst_ref, bias_ref, dz_ref, dws_ref, dbias_ref, dlng_ref, dlnb_ref, du_ref, dvln_ref):
        @pl.when(pl.program_id(0) == 0)
        def _():
            dws_ref[...] = jnp.zeros_like(dws_ref)
            dbias_ref[...] = jnp.zeros_like(dbias_ref)
            dlng_ref[...] = jnp.zeros_like(dlng_ref)
            dlnb_ref[...] = jnp.zeros_like(dlnb_ref)

        zu, zv, tu, tv, u, vhat, rstd, vln = _sgu_recompute(z_ref, lng_ref, lnb_ref, S)
        for c in range(tt // CHUNK):
            rows = slice(c * CHUNK, (c + 1) * CHUNK)
            for gi in range(GROUPS):
                cols = slice(gi * dg, (gi + 1) * dg)
                v_cg = vln[rows, cols]
                mixed = jnp.dot(ws_ref[gi], v_cg, preferred_element_type=F32) + bias_ref[:, cols]
                dgt = dgt_ref[rows, cols].astype(F32)
                du_ref[rows, cols] = dgt * mixed
                dmx = dgt * u[rows, cols]
                dbias_ref[:, cols] += dmx
                dmx16 = dmx.astype(BF16)
                dws_ref[gi] += lax.dot_general(dmx16, v_cg, NT, preferred_element_type=F32)
                dvln_ref[rows, cols] = jnp.dot(wst_ref[gi], dmx16, preferred_element_type=F32)

        dvln = dvln_ref[...]
        dlng_ref[...] += _sum8(dvln * vhat)
        dlnb_ref[...] += _sum8(dvln)
        dvhat = dvln * lng_ref[...]
        dv = rstd * (dvhat - jnp.mean(dvhat, axis=-1, keepdims=True) - vhat * jnp.mean(dvhat * vhat, axis=-1, keepdims=True))
        dz_ref[:, :S] = (du_ref[...] * _gelu_grad(zu, tu)).astype(BF16)
        dz_ref[:, S:] = (dv * _gelu_grad(zv, tv)).astype(BF16)

    vec = pl.BlockSpec((1, S), lambda i: (0, 0))
    wsp = pl.BlockSpec((GROUPS, CHUNK, CHUNK), lambda i: (0, 0, 0))
    full = pl.BlockSpec((CHUNK, S), lambda i: (0, 0))
    acc8 = pl.BlockSpec((8, S), lambda i: (0, 0))
    return _token_call(
        body,
        name,
        T // tt,
        [pl.BlockSpec((tt, S), lambda i: (i, 0)), pl.BlockSpec((tt, S2), lambda i: (i, 0)), vec, vec, wsp, wsp, full],
        [dgated, zpre, lng, lnb, ws, ws_t, bias_full],
        [pl.BlockSpec((tt, S2), lambda i: (i, 0)), wsp, full, acc8, acc8],
        [
            jax.ShapeDtypeStruct((T, S2), BF16),
            jax.ShapeDtypeStruct((GROUPS, CHUNK, CHUNK), F32),
            jax.ShapeDtypeStruct((CHUNK, S), F32),
            jax.ShapeDtypeStruct((8, S), F32),
            jax.ShapeDtypeStruct((8, S), F32),
        ],
        [pltpu.VMEM((tt, S), F32), pltpu.VMEM((tt, S), F32)],
        comm,
    )


def _shifted_planes(sh, tt):
    n = tt + HALO - SUBLANES
    for s in range(1, SUBLANES):
        sh[s, 0:n, :] = sh[0, pl.ds(s, n), :]


def conv_mid_fwd(p, wdw, bdw, lng, lnb, tt, name, comm=None):
    T, C2 = p.shape
    C = C2 // 2

    def body(p_ref, w_ref, b_ref, lng_ref, lnb_ref, yc_ref, ys_ref, sh):
        @pl.when(pl.program_id(0) == 0)
        def _():
            sh[0, 0:HALO, :] = jnp.zeros((HALO, C), F32)

        @pl.when(pl.program_id(0) > 0)
        def _():
            sh[0, 0:HALO, :] = sh[0, tt : tt + HALO, :]

        pv = p_ref[...].astype(F32)
        sh[0, HALO:, :] = pv[:, :C] * _sigmoid(pv[:, C:])
        _shifted_planes(sh, tt)
        for lb in range(C // LANES):
            lanes = slice(lb * LANES, (lb + 1) * LANES)
            wk = [jnp.broadcast_to(w_ref[k : k + 1, lanes], (SUBLANES, LANES)) for k in range(CONV_W)]
            bias = jnp.broadcast_to(b_ref[:, lanes], (SUBLANES, LANES))

            def rows(i, carry):
                r0 = pl.multiple_of(i * CONV_ROWS, CONV_ROWS)
                accs = [[bias, jnp.zeros((SUBLANES, LANES), F32)] for _ in range(CONV_ROWS // SUBLANES)]
                for k in range(CONV_W):
                    o = HALO - (CONV_W - 1) + k
                    for j, acc in enumerate(accs):
                        blk = sh[o % SUBLANES, pl.ds(r0 + (o // SUBLANES + j) * SUBLANES, SUBLANES), lanes]
                        acc[k % 2] = acc[k % 2] + wk[k] * blk
                for j, acc in enumerate(accs):
                    yc_ref[pl.ds(r0 + j * SUBLANES, SUBLANES), lanes] = acc[0] + acc[1]
                return carry

            lax.fori_loop(0, tt // CONV_ROWS, rows, 0)
        acc = yc_ref[...]
        yc = acc - jnp.mean(acc, axis=-1, keepdims=True)
        yn = yc * lax.rsqrt(jnp.mean(yc * yc, axis=-1, keepdims=True) + EPS) * lng_ref[...] + lnb_ref[...]
        ys_ref[...] = (yn * _sigmoid(yn)).astype(BF16)

    vec = pl.BlockSpec((1, C), lambda i: (0, 0))
    row = pl.BlockSpec((tt, C), lambda i: (i, 0))
    return _token_call(
        body,
        name,
        T // tt,
        [pl.BlockSpec((tt, C2), lambda i: (i, 0)), pl.BlockSpec((HALO, C), lambda i: (0, 0)), vec, vec, vec],
        [p, wdw, bdw, lng, lnb],
        [row, row],
        [jax.ShapeDtypeStruct((T, C), F32), jax.ShapeDtypeStruct((T, C), BF16)],
        [pltpu.VMEM((SUBLANES, tt + HALO, C), F32)],
        comm,
    )


def conv_mid_bwd(dys, yc, p, wdw, lng, lnb, tt, name, comm=None):
    T, C2 = p.shape
    C = C2 // 2
    n = T // tt

    def body(dys_ref, yc_ref, p_ref, w_ref, lng_ref, lnb_ref, dp_ref, dw_ref, db_ref, dlng_ref, dlnb_ref, sh, y_s, dy_s):
        @pl.when(pl.program_id(0) == 0)
        def _():
            sh[0, tt : tt + HALO, :] = jnp.zeros((HALO, C), F32)
            dw_ref[...] = jnp.zeros_like(dw_ref)
            db_ref[...] = jnp.zeros_like(db_ref)
            dlng_ref[...] = jnp.zeros_like(dlng_ref)
            dlnb_ref[...] = jnp.zeros_like(dlnb_ref)

        @pl.when(pl.program_id(0) > 0)
        def _():
            sh[0, tt : tt + HALO, :] = sh[0, 0:HALO, :]

        ycv = yc_ref[...]
        ycc = ycv - jnp.mean(ycv, axis=-1, keepdims=True)
        rstd = lax.rsqrt(jnp.mean(ycc * ycc, axis=-1, keepdims=True) + EPS)
        yhat = ycc * rstd
        lng_v = lng_ref[...]
        yn = yhat * lng_v + lnb_ref[...]
        sg = _sigmoid(yn)
        dyn = dys_ref[...].astype(F32) * (sg * (1.0 + yn * (1.0 - sg)))
        dlng_ref[...] += _sum8(dyn * yhat)
        dlnb_ref[...] += _sum8(dyn)
        dyh = dyn * lng_v
        dyc = rstd * (dyh - jnp.mean(dyh, axis=-1, keepdims=True) - yhat * jnp.mean(dyh * yhat, axis=-1, keepdims=True))
        db_ref[...] += _sum8(dyc)
        sh[0, 0:tt, :] = dyc
        _shifted_planes(sh, tt)

        pv = p_ref[...].astype(F32)
        a = pv[:, :C]
        sgate = _sigmoid(pv[:, C:])
        y_s[...] = a * sgate
        for lb in range(C // LANES):
            lanes = slice(lb * LANES, (lb + 1) * LANES)
            for k0, k1 in ((0, CONV_W // 2), (CONV_W // 2, CONV_W)):
                wk = [jnp.broadcast_to(w_ref[k : k + 1, lanes], (SUBLANES, LANES)) for k in range(k0, k1)]

                def rows(i, dw_acc, k0=k0, k1=k1, wk=wk, lanes=lanes):
                    r0 = pl.multiple_of(i * CONV_ROWS, CONV_ROWS)
                    dw_acc = list(dw_acc)
                    zero = jnp.zeros((SUBLANES, LANES), F32)
                    ybs, accs = [], []
                    for j in range(CONV_ROWS // SUBLANES):
                        at = pl.ds(r0 + j * SUBLANES, SUBLANES)
                        ybs.append(y_s[at, lanes])
                        accs.append([zero if k0 == 0 else dy_s[at, lanes], zero])
                    for n_k, k in enumerate(range(k0, k1)):
                        o = CONV_W - 1 - k
                        for j, acc in enumerate(accs):
                            blk = sh[o % SUBLANES, pl.ds(r0 + (o // SUBLANES + j) * SUBLANES, SUBLANES), lanes]
                            acc[n_k % 2] = acc[n_k % 2] + wk[n_k] * blk
                            dw_acc[n_k] = dw_acc[n_k] + ybs[j] * blk
                    for j, acc in enumerate(accs):
                        dy_s[pl.ds(r0 + j * SUBLANES, SUBLANES), lanes] = acc[0] + acc[1]
                    return tuple(dw_acc)

                dw_acc = lax.fori_loop(0, tt // CONV_ROWS, rows, tuple(jnp.zeros((SUBLANES, LANES), F32) for _ in range(k0, k1)))
                for n_k, k in enumerate(range(k0, k1)):
                    dw_ref[SUBLANES * k : SUBLANES * (k + 1), lanes] += dw_acc[n_k]
        dy = dy_s[...]
        dp_ref[:, :C] = (dy * sgate).astype(BF16)
        dp_ref[:, C:] = (dy * a * sgate * (1.0 - sgate)).astype(BF16)

    vec = pl.BlockSpec((1, C), lambda i: (0, 0))
    row = pl.BlockSpec((tt, C), lambda i: (n - 1 - i, 0))
    row2 = pl.BlockSpec((tt, C2), lambda i: (n - 1 - i, 0))
    acc8 = pl.BlockSpec((8, C), lambda i: (0, 0))
    return _token_call(
        body,
        name,
        n,
        [row, row, row2, pl.BlockSpec((HALO, C), lambda i: (0, 0)), vec, vec],
        [dys, yc, p, wdw, lng, lnb],
        [row2, pl.BlockSpec((HALO * 8, C), lambda i: (0, 0)), acc8, acc8, acc8],
        [
            jax.ShapeDtypeStruct((T, C2), BF16),
            jax.ShapeDtypeStruct((HALO * 8, C), F32),
            jax.ShapeDtypeStruct((8, C), F32),
            jax.ShapeDtypeStruct((8, C), F32),
            jax.ShapeDtypeStruct((8, C), F32),
        ],
        [pltpu.VMEM((SUBLANES, tt + HALO, C), F32), pltpu.VMEM((tt, C), F32), pltpu.VMEM((tt, C), F32)],
        comm,
    )


def sum_slots(slots, name):
    _, r, w = slots.shape

    def body(s_ref, o_ref):
        total = s_ref[0].astype(F32)
        for p in range(1, N_DEV):
            total = total + s_ref[p].astype(F32)
        o_ref[...] = total

    return _pcall(body, name=name, out_shape=jax.ShapeDtypeStruct((r, w), F32), compiler_params=pltpu.CompilerParams(vmem_limit_bytes=VMEM_LIMIT_V7X))(slots)


def adamw(w, g, m, v, name):
    R, C = w.shape
    tr = _tile(R, 512, 8)
    c1 = 1.0 - ADAM_B1**ADAM_STEP
    c2 = 1.0 - ADAM_B2**ADAM_STEP

    def body(w_ref, g_ref, m_ref, v_ref, d_ref, mo_ref, vo_ref):
        gv = g_ref[...]
        m2 = ADAM_B1 * m_ref[...] + (1.0 - ADAM_B1) * gv
        v2 = ADAM_B2 * v_ref[...] + (1.0 - ADAM_B2) * (gv * gv)
        mo_ref[...] = m2
        vo_ref[...] = v2
        d_ref[...] = -ADAM_LR * ((m2 / c1) / (jnp.sqrt(v2 / c2) + ADAM_EPS) + ADAM_WD * w_ref[...])

    blk = pl.BlockSpec((tr, C), lambda i: (i, 0))
    return _pcall(
        body,
        name=name,
        grid=(R // tr,),
        in_specs=[blk] * 4,
        out_specs=[blk] * 3,
        out_shape=[jax.ShapeDtypeStruct((R, C), F32)] * 3,
        compiler_params=_cparams(1),
    )(w, g, m, v)


def _sublayers(depth):
    out = []
    for layer in range(depth):
        out.append(("ffn", layer, 0))
        out.append(("sgu" if layer % N_MIXERS == 0 else "conv", layer, layer // N_MIXERS))
        out.append(("ffn", layer, 1))
    return out


def _row_blocks(sub, shards):
    kind, layer, idx = sub
    if kind == "ffn":
        blk = {"gate_t": shards["ff_w_gate"][layer, idx].T, "up_t": shards["ff_w_up"][layer, idx].T, "down": shards["ff_w_down"][layer, idx]}
    elif kind == "sgu":
        blk = {"in_t": shards["sgu_w_in"][idx].T, "out": shards["sgu_w_out"][idx]}
    else:
        blk = {"pw1_t": shards["conv_w_pw1"][idx].T, "pw2": shards["conv_w_pw2"][idx]}
    return {nm: arr.astype(BF16) for nm, arr in blk.items()}


def _pad8(a):
    return jnp.pad(a, ((0, (-a.shape[0]) % 8), (0, 0)))


def kernel(x, norm_g, ff_w_gate, ff_w_up, ff_w_down, sgu_w_in, sgu_ln_g, sgu_ln_b, sgu_w_spatial, sgu_b_spatial, sgu_w_out, conv_w_pw1, conv_w_dw, conv_b_dw, conv_ln_g, conv_ln_b, conv_w_pw2, loss_target, m_norm_g, m_ff_w_gate, m_ff_w_up, m_ff_w_down, m_sgu_w_in, m_sgu_ln_g, m_sgu_ln_b, m_sgu_w_spatial, m_sgu_b_spatial, m_sgu_w_out, m_conv_w_pw1, m_conv_w_dw, m_conv_b_dw, m_conv_ln_g, m_conv_ln_b, m_conv_w_pw2, v_norm_g, v_ff_w_gate, v_ff_w_up, v_ff_w_down, v_sgu_w_in, v_sgu_ln_g, v_sgu_ln_b, v_sgu_w_spatial, v_sgu_b_spatial, v_sgu_w_out, v_conv_w_pw1, v_conv_w_dw, v_conv_b_dw, v_conv_ln_g, v_conv_ln_b, v_conv_w_pw2):
    names = ["norm_g", "ff_w_gate", "ff_w_up", "ff_w_down", "sgu_w_in", "sgu_ln_g", "sgu_ln_b", "sgu_w_spatial", "sgu_b_spatial", "sgu_w_out", "conv_w_pw1", "conv_w_dw", "conv_b_dw", "conv_ln_g", "conv_ln_b", "conv_w_pw2"]
    weights = dict(zip(names, [norm_g, ff_w_gate, ff_w_up, ff_w_down, sgu_w_in, sgu_ln_g, sgu_ln_b, sgu_w_spatial, sgu_b_spatial, sgu_w_out, conv_w_pw1, conv_w_dw, conv_b_dw, conv_ln_g, conv_ln_b, conv_w_pw2]))
    moments_m = dict(zip(names, [m_norm_g, m_ff_w_gate, m_ff_w_up, m_ff_w_down, m_sgu_w_in, m_sgu_ln_g, m_sgu_ln_b, m_sgu_w_spatial, m_sgu_b_spatial, m_sgu_w_out, m_conv_w_pw1, m_conv_w_dw, m_conv_b_dw, m_conv_ln_g, m_conv_ln_b, m_conv_w_pw2]))
    moments_v = dict(zip(names, [v_norm_g, v_ff_w_gate, v_ff_w_up, v_ff_w_down, v_sgu_w_in, v_sgu_ln_g, v_sgu_ln_b, v_sgu_w_spatial, v_sgu_b_spatial, v_sgu_w_out, v_conv_w_pw1, v_conv_w_dw, v_conv_b_dw, v_conv_ln_g, v_conv_ln_b, v_conv_w_pw2]))

    _, T, D = x.shape
    depth = norm_g.shape[0]
    n_conv = conv_w_dw.shape[0]
    n_sgu = sgu_w_in.shape[0]
    S = sgu_ln_g.shape[1]
    lanes = norm_g.shape[2]
    subs = _sublayers(depth)
    n_sub = len(subs)

    cx, cy, cc = (lax.axis_index(a) for a in MESH_AXES)
    my_block = 4 * cx + 2 * cy + cc

    blocks = [_row_blocks(sub, weights) for sub in subs]
    dw_pad = jnp.pad(conv_w_dw, ((0, 0), (0, HALO - CONV_W), (0, 0)))
    small_parts = [_pad8(p) for p in (norm_g.reshape(-1, lanes), dw_pad.reshape(-1, lanes), conv_b_dw, conv_ln_g, conv_ln_b)]
    small_rows = [p.shape[0] for p in small_parts]
    small = jnp.concatenate(small_parts, axis=0)

    def gathered(block_names, outs):
        return {nm: o.reshape(N_DEV * o.shape[1], o.shape[2]) for nm, o in zip(block_names, outs)}

    first_names = [nm for nm in blocks[0] if nm != "down"]
    first = gather_two_level([blocks[0][nm] for nm in first_names] + [small], name="gather_first")
    W = [None] * n_sub
    W[0] = gathered(first_names, first[:-1])
    gsmall = first[-1]

    small_full = jnp.transpose(gsmall, (1, 0, 2)).reshape(gsmall.shape[1], N_DEV * lanes)
    so = [0]
    for r in small_rows:
        so.append(so[-1] + r)
    norm_full = small_full[so[0] : so[0] + depth * norm_g.shape[1]].reshape(depth, -1, D)
    dw_full = small_full[so[1] : so[1] + n_conv * HALO].reshape(n_conv, HALO, D)
    bdw_full, clng_full, clnb_full = (small_full[so[k] : so[k] + n_conv] for k in (2, 3, 4))

    causal = jnp.tril(jnp.ones((CHUNK, CHUNK), dtype=bool))
    ws_all = jnp.where(causal[None, None], sgu_w_spatial, 0.0).astype(BF16)
    wst_all = jnp.swapaxes(ws_all, -1, -2)
    bias_full_all = jnp.repeat(jnp.swapaxes(sgu_b_spatial, -1, -2), S // GROUPS, axis=-1)

    tt = _tile(T, 512, CHUNK)
    tt_mix = _tile(T, 256, CHUNK)

    tk = _tile(T, 1024, CHUNK)

    def vec(v):
        return v.reshape(1, -1)

    def split_first(comm):
        return (comm[:1], comm[1:]) if comm else (None, None)

    def norms(kind, layer, idx):
        pre = 4 * idx if kind == "ffn" else 2
        return vec(norm_full[layer, pre]), vec(norm_full[layer, pre + 1])

    xs = x[0]
    saved = []
    for si, (kind, layer, idx) in enumerate(subs):
        w = W[si]
        g_pre, g_post = norms(kind, layer, idx)
        nxt = [("gather", a) for a in blocks[si + 1].values()] if si + 1 < n_sub else None
        nxt_first, nxt_rest = split_first(nxt)
        if kind == "ffn":
            own = [("gather", blocks[0]["down"])] if si == 0 else []
            (silu, dact, a), got = rms_matmul_nt(xs, g_pre, [w["gate_t"], w["up_t"]], tt, "ffn_in", own + (nxt or []), swiglu=True)
            if own:
                w.update(gathered(["down"], got[:1]))
                got = got[1:]
            if si == n_sub - 1:
                x_new, o, sq = matmul_rms_res(xs, a, w["down"], g_post, FFN_SCALE, tt, "ffn_out_loss", loss_target[0])
            else:
                x_new, o = matmul_rms_res(xs, a, w["down"], g_post, FFN_SCALE, tt, "ffn_out")
            saved.append((xs, silu, dact, o, a))
        elif kind == "sgu":
            (zpre,), got = rms_matmul_nt(xs, g_pre, [w["in_t"]], tt, "sgu_in", nxt_first)
            (gated,), got_rest = sgu_mix_fwd(zpre, vec(sgu_ln_g[idx]), vec(sgu_ln_b[idx]), ws_all[idx], bias_full_all[idx], tt_mix, "sgu_mix", nxt_rest)
            got = list(got) + list(got_rest)
            x_new, mm = matmul_rms_res(xs, gated, w["out"], g_post, 1.0, tt, "sgu_out")
            saved.append((xs, zpre, gated, mm))
        else:
            (p,), got = rms_matmul_nt(xs, g_pre, [w["pw1_t"]], tt, "conv_pw1", nxt_first)
            (yc, ys), got_rest = conv_mid_fwd(p, dw_full[idx], vec(bdw_full[idx]), vec(clng_full[idx]), vec(clnb_full[idx]), tt, "conv_mid", nxt_rest)
            got = list(got) + list(got_rest)
            x_new, mm = matmul_rms_res(xs, ys, w["pw2"], g_post, 1.0, tt, "conv_pw2")
            saved.append((xs, p, yc, ys, mm))
        if nxt:
            W[si + 1] = gathered(blocks[si + 1].keys(), got)
        xs = x_new

    dx = xs
    loss = lax.psum(0.5 * jnp.sum(sq) / D, MESH_AXES)

    d_norm = [[None] * norm_full.shape[1] for _ in range(depth)]
    d_sgu = [None] * n_sgu
    d_conv = [None] * n_conv
    slots = [None] * n_sub
    pending = None

    def scatter_of(p):
        return [("scatter", dwm.reshape(N_DEV, dwm.shape[0] // N_DEV, D)) for dwm in p[1].values()] if p else None

    def pack_small():
        def sum8(v):
            return jnp.sum(v, axis=0)

        g_norm = jnp.stack([jnp.stack([sum8(d) for d in row]) for row in d_norm])
        g_dw = jnp.stack([jnp.sum(d[0].reshape(HALO, 8, D), axis=1) for d in d_conv])
        g_bdw, g_clng, g_clnb = (jnp.stack([sum8(d[k]) for d in d_conv]) for k in (1, 2, 3))
        g_slng, g_slnb = (jnp.stack([sum8(d[k]) for d in d_sgu]) for k in (2, 3))
        g_bsp = jnp.stack([jnp.sum(d[1].reshape(CHUNK, GROUPS, S // GROUPS), axis=-1).T for d in d_sgu])
        g_wsp = jnp.stack([jnp.where(causal[None], d[0], 0.0) for d in d_sgu])
        parts = [g_norm, g_dw, g_bdw, g_clng, g_clnb, g_slng, g_slnb, g_bsp, g_wsp]
        return parts, [p.size // D for p in parts], jnp.concatenate([_pad8(p.reshape(-1, D)) for p in parts], axis=0)

    for si in reversed(range(n_sub)):
        kind, layer, idx = subs[si]
        w = W[si]
        g_pre, g_post = norms(kind, layer, idx)
        pre = 4 * idx if kind == "ffn" else 2
        sc_first, sc_rest = split_first(scatter_of(pending))
        if kind == "ffn":
            xs_in, silu, dact, o, a = saved[si]
            (do, dg_post, dgate, dup), got = rms_bwd_matmul_nt(dx, o, g_post, w["down"], FFN_SCALE, (silu, dact), tt, "ffn_out_bwd", scatter_of(pending))
            dw_down = wgrad_tn(a, do, tk, name="wgrad_ffn_out")
            tail = None
            if si == 0:
                d_norm[layer][pre], d_norm[layer][pre + 1] = jnp.zeros_like(dg_post), dg_post
                sparts, srows, sgrad = pack_small()
                tail = scatter_of((si, {"down": dw_down})) + [("gather", sgrad)]
            (dx, h, dg_pre), got_tail = matmul_rms_bwd(dx, xs_in, g_pre, [(dgate, w["gate_t"]), (dup, w["up_t"])], tt, "ffn_in_bwd", tail)
            dws_now = {"gate_t": wgrad_tn(dgate, h, tk, name="wgrad_ffn_in"), "up_t": wgrad_tn(dup, h, tk, name="wgrad_ffn_in"), "down": dw_down}
        elif kind == "sgu":
            xs_in, zpre, gated, mm = saved[si]
            (dm, dg_post, dgated), got = rms_bwd_matmul_nt(dx, mm, g_post, w["out"], 1.0, None, tt, "sgu_out_bwd", sc_first)
            (dzpre, dws, dbias, dlng, dlnb), got_rest = sgu_mix_bwd(dgated, zpre, vec(sgu_ln_g[idx]), vec(sgu_ln_b[idx]), ws_all[idx], wst_all[idx], bias_full_all[idx], tt_mix, "sgu_mix_bwd", sc_rest)
            got = list(got) + list(got_rest)
            (dx, h, dg_pre), _ = matmul_rms_bwd(dx, xs_in, g_pre, [(dzpre, w["in_t"])], tt, "sgu_in_bwd")
            dws_now = {"in_t": wgrad_tn(dzpre, h, tt, name="wgrad_sgu_in"), "out": wgrad_tn(gated, dm, tk, name="wgrad_sgu_out")}
            d_sgu[idx] = (dws, dbias, dlng, dlnb)
        else:
            xs_in, p, yc, ys, mm = saved[si]
            (dm, dg_post, dys), got = rms_bwd_matmul_nt(dx, mm, g_post, w["pw2"], 1.0, None, tt, "conv_pw2_bwd", sc_first)
            (dp, dwdw, dbdw, dlng, dlnb), got_rest = conv_mid_bwd(dys, yc, p, dw_full[idx], vec(clng_full[idx]), vec(clnb_full[idx]), tt, "conv_mid_bwd", sc_rest)
            got = list(got) + list(got_rest)
            (dx, h, dg_pre), _ = matmul_rms_bwd(dx, xs_in, g_pre, [(dp, w["pw1_t"])], tt, "conv_pw1_bwd")
            dws_now = {"pw1_t": wgrad_tn(dp, h, tk, name="wgrad_conv_pw1"), "pw2": wgrad_tn(ys, dm, tk, name="wgrad_conv_pw2")}
            d_conv[idx] = (dwdw, dbdw, dlng, dlnb)
        d_norm[layer][pre], d_norm[layer][pre + 1] = dg_pre, dg_post
        if pending:
            slots[pending[0]] = dict(zip(pending[1].keys(), got))
        pending = (si, dws_now)
    grad_x = dx[None]

    last_names = [nm for nm in pending[1] if nm != "down"]
    last = exchange(scatter_of((0, {nm: pending[1][nm] for nm in last_names})) + [("gather", dg_pre)], name="scatter_last")
    slots[0] = dict(zip(last_names, last[:-1]))
    slots[0]["down"] = got_tail[0]
    stotal = sum_slots(got_tail[1], name="sum_slots")
    g_first = jnp.sum(sum_slots(last[-1], name="sum_slots"), axis=0)

    gs = {(si, nm): sum_slots(s, name="sum_slots") for si in range(n_sub) for nm, s in slots[si].items()}
    grads = {}
    ffn_si = {(layer, idx): si for si, (kind, layer, idx) in enumerate(subs) if kind == "ffn"}
    sgu_si = {idx: si for si, (kind, layer, idx) in enumerate(subs) if kind == "sgu"}
    conv_si = {idx: si for si, (kind, layer, idx) in enumerate(subs) if kind == "conv"}
    grads["ff_w_gate"] = jnp.stack([jnp.stack([gs[(ffn_si[(l, f)], "gate_t")].T for f in range(2)]) for l in range(depth)])
    grads["ff_w_up"] = jnp.stack([jnp.stack([gs[(ffn_si[(l, f)], "up_t")].T for f in range(2)]) for l in range(depth)])
    grads["ff_w_down"] = jnp.stack([jnp.stack([gs[(ffn_si[(l, f)], "down")] for f in range(2)]) for l in range(depth)])
    grads["sgu_w_in"] = jnp.stack([gs[(sgu_si[j], "in_t")].T for j in range(n_sgu)])
    grads["sgu_w_out"] = jnp.stack([gs[(sgu_si[j], "out")] for j in range(n_sgu)])
    grads["conv_w_pw1"] = jnp.stack([gs[(conv_si[j], "pw1_t")].T for j in range(n_conv)])
    grads["conv_w_pw2"] = jnp.stack([gs[(conv_si[j], "pw2")] for j in range(n_conv)])

    so = [0]
    for r in srows:
        so.append(so[-1] + r + (-r) % 8)
    sp = [stotal[so[k] : so[k] + srows[k]].reshape(sparts[k].shape) for k in range(len(sparts))]

    def my_lanes(v):
        return lax.dynamic_slice_in_dim(v, my_block * lanes, lanes, axis=-1)

    grads["norm_g"] = my_lanes(sp[0].at[0, 0].set(g_first))
    grads["conv_w_dw"] = my_lanes(sp[1])[:, :CONV_W]
    grads["conv_b_dw"] = my_lanes(sp[2])
    grads["conv_ln_g"] = my_lanes(sp[3])
    grads["conv_ln_b"] = my_lanes(sp[4])
    grads["sgu_ln_g"], grads["sgu_ln_b"], grads["sgu_b_spatial"], grads["sgu_w_spatial"] = sp[5], sp[6], sp[7], sp[8]

    deltas, new_m, new_v = {}, {}, {}
    for nm in names:
        w = weights[nm]
        two_d = (-1, w.shape[-1])
        d, m2, v2 = adamw(w.reshape(two_d), grads[nm].reshape(two_d), moments_m[nm].reshape(two_d), moments_v[nm].reshape(two_d), name="adamw")
        deltas[nm], new_m[nm], new_v[nm] = d.reshape(w.shape), m2.reshape(w.shape), v2.reshape(w.shape)

    return (loss, grad_x, *[grads[n] for n in names], *[deltas[n] for n in names], *[new_m[n] for n in names], *[new_v[n] for n in names])
```

```python
import jax
import jax.numpy as jnp
from jax import lax
from jax.experimental import pallas as pl
from jax.experimental.pallas import tpu as pltpu

F32 = jnp.float32
BF16 = jnp.bfloat16

EPS = 1e-6
FFN_SCALE = 0.5
N_MIXERS = 2
CHUNK = 128
GROUPS = 8
CONV_W = 31
HALO = 32
GELU_C0 = 0.7978845608028654
GELU_C1 = 0.044715
ADAM_LR, ADAM_B1, ADAM_B2, ADAM_EPS, ADAM_WD, ADAM_STEP = 0.001, 0.9, 0.999, 1e-08, 0.01, 10

MESH_AXES = ("x", "y", "c")
N_DEV = 8
VMEM_LIMIT_V7X = 56 * 1024 * 1024
WGRAD_ACC_BYTES = 16 * 1024 * 1024
MXU_COLS_V7X = 256
SUBLANES, LANES = 8, 128
CONV_ROWS = 32

NT = (((1,), (1,)), ((), ()))
TN = (((0,), (0,)), ((), ()))

HBM_SPEC = pl.BlockSpec(memory_space=pl.ANY)
MESH_ID = pl.DeviceIdType.MESH


def _pcall(body, **kw):
    return pl.pallas_call(body, **kw)


def _cparams(n_axes):
    return pltpu.CompilerParams(dimension_semantics=("arbitrary",) * n_axes, vmem_limit_bytes=VMEM_LIMIT_V7X)


def _tile(n, pref, align):
    if n <= pref:
        return n
    t = (pref // align) * align
    while t > align and n % t:
        t -= align
    assert n % t == 0, (n, pref, align)
    return t


def _rstd(x):
    return lax.rsqrt(jnp.mean(x * x, axis=-1, keepdims=True) + EPS)


def _sum8(v):
    t, d = v.shape
    return v.reshape(t // 8, 8, d).sum(axis=0)


def _sigmoid(x):
    return jax.nn.sigmoid(x)


def _gelu_parts(z):
    zz = z * z
    t = jnp.tanh(z * (GELU_C0 + (GELU_C0 * GELU_C1) * zz))
    return zz, t, 0.5 * t + 0.5


def _gelu_grad(z, parts):
    zz, t, cdf = parts
    return cdf + z * (1.0 - t * t) * ((0.5 * GELU_C0) + (1.5 * GELU_C0 * GELU_C1) * zz)


def _resident(shape):
    return pl.BlockSpec(shape, lambda i: (0,) * len(shape), pipeline_mode=pl.Buffered(1))


def _exchange_io(comm):
    n = len(comm)
    out_shape = [jax.ShapeDtypeStruct(((N_DEV,) + a.shape) if kind == "gather" else a.shape, a.dtype) for kind, a in comm]
    scratch = [pltpu.SemaphoreType.DMA((n, N_DEV - 1)), pltpu.SemaphoreType.DMA((n, N_DEV - 1)), pltpu.SemaphoreType.DMA((n,))]
    return [HBM_SPEC] * n, [HBM_SPEC] * n, out_shape, scratch


def _exchange_copies(kinds, in_refs, out_refs, send_sems, recv_sems, local_sems, with_arrivals=True):
    x, y, c = (lax.axis_index(a) for a in MESH_AXES)
    me = 4 * x + 2 * y + c
    local, sends, arrivals = [], [], []
    for a, kind in enumerate(kinds):
        src, dst = in_refs[a], out_refs[a]
        gather = kind == "gather"
        local.append(pltpu.make_async_copy(src if gather else src.at[me], dst.at[me], local_sems.at[a]))
        for k in range(N_DEV - 1):
            mask = k + 1
            px = 1 - x if mask & 4 else x
            py = 1 - y if mask & 2 else y
            pc = 1 - c if mask & 1 else c
            peer = 4 * px + 2 * py + pc
            block = src if gather else src.at[peer]
            for into, lst in ((me, sends), (peer, arrivals)):
                if lst is arrivals and not with_arrivals:
                    continue
                lst.append(
                    pltpu.make_async_remote_copy(
                        src_ref=block, dst_ref=dst.at[into], send_sem=send_sems.at[a, k], recv_sem=recv_sems.at[a, k], device_id=(px, py, pc), device_id_type=MESH_ID
                    )
                )
    return local, sends, arrivals


def _exchange_start(copies):
    local, sends, _ = copies
    for cp in local + sends:
        cp.start()


def _exchange_finish(copies):
    local, sends, arrivals = copies
    for cp in arrivals:
        cp.wait_recv()
    for cp in sends:
        cp.wait_send()
    for cp in local:
        cp.wait()


def exchange(comm, name):
    kinds = [k for k, _ in comm]
    n = len(comm)
    in_specs, out_specs, out_shape, scratch = _exchange_io(comm)

    def body(*refs):
        copies = _exchange_copies(kinds, refs[:n], refs[n : 2 * n], *refs[2 * n :])
        _exchange_start(copies)
        _exchange_finish(copies)

    return _pcall(body, name=name, in_specs=in_specs, out_specs=out_specs, out_shape=out_shape, scratch_shapes=scratch)(*[a for _, a in comm])


def gather_two_level(packs, name):
    n = len(packs)

    def body(*refs):
        in_refs, out_refs = refs[:n], refs[n : 2 * n]
        send_sems, recv_sems, local_sems = refs[2 * n :]
        x, y, c = (lax.axis_index(a) for a in MESH_AXES)
        chips = [(1 - x, y), (x, 1 - y), (1 - x, 1 - y)]
        me, sibling = (x, y, c), (x, y, 1 - c)

        def slot(a, px, py, pc):
            return out_refs[a].at[4 * px + 2 * py + pc]

        def copy(a, k, block, to, from_input=False):
            return pltpu.make_async_remote_copy(
                src_ref=in_refs[a] if from_input else slot(a, *block),
                dst_ref=slot(a, *block),
                send_sem=send_sems.at[a, k],
                recv_sem=recv_sems.at[a, k],
                device_id=to,
                device_id_type=MESH_ID,
            )

        mine = [pltpu.make_async_copy(in_refs[a], slot(a, *me), local_sems.at[a]) for a in range(n)]
        for cp in mine:
            cp.start()
        first = []
        for a in range(n):
            first.append(copy(a, 0, me, sibling, from_input=True))
            first += [copy(a, 1 + j, me, (*chip, c), from_input=True) for j, chip in enumerate(chips)]
        for cp in first:
            cp.start()
        passed = []
        for j, chip in enumerate(chips):
            for a in range(n):
                copy(a, 1 + j, (*chip, c), me).wait_recv()
                fwd = copy(a, 4 + j, (*chip, c), sibling)
                fwd.start()
                passed.append(fwd)
        for a in range(n):
            copy(a, 0, sibling, me).wait_recv()
            for j, chip in enumerate(chips):
                copy(a, 4 + j, (*chip, 1 - c), me).wait_recv()
        for cp in first + passed:
            cp.wait_send()
        for cp in mine:
            cp.wait()

    return _pcall(
        body,
        name=name,
        in_specs=[HBM_SPEC] * n,
        out_specs=[HBM_SPEC] * n,
        out_shape=[jax.ShapeDtypeStruct((N_DEV,) + p.shape, p.dtype) for p in packs],
        scratch_shapes=[pltpu.SemaphoreType.DMA((n, N_DEV - 1)), pltpu.SemaphoreType.DMA((n, N_DEV - 1)), pltpu.SemaphoreType.DMA((n,))],
    )(*packs)


def _token_call(body, name, n_tiles, in_specs, inputs, out_specs, out_shape, scratch, comm):
    n_in, n_out, n_scr = len(inputs), len(out_shape), len(scratch)
    comm = comm or []
    nc = len(comm)
    full_body = body
    if nc:
        kinds = [k for k, _ in comm]
        c_in, c_out, c_shape, c_scr = _exchange_io(comm)
        in_specs, out_specs, out_shape, scratch = in_specs + c_in, out_specs + c_out, out_shape + c_shape, scratch + c_scr

        def full_body(*refs):
            ins, cins = refs[:n_in], refs[n_in : n_in + nc]
            o0 = n_in + nc
            outs, couts = refs[o0 : o0 + n_out], refs[o0 + n_out : o0 + n_out + nc]
            s0 = o0 + n_out + nc
            scr, sems = refs[s0 : s0 + n_scr], refs[s0 + n_scr :]

            @pl.when(pl.program_id(0) == 0)
            def _():
                _exchange_start(_exchange_copies(kinds, cins, couts, *sems, with_arrivals=False))

            body(*ins, *outs, *scr)

            @pl.when(pl.program_id(0) == n_tiles - 1)
            def _():
                _exchange_finish(_exchange_copies(kinds, cins, couts, *sems))

    res = _pcall(
        full_body, name=name, grid=(n_tiles,), in_specs=in_specs, out_specs=out_specs, out_shape=out_shape, scratch_shapes=scratch, compiler_params=_cparams(1)
    )(*inputs, *[a for _, a in comm])
    return res[:n_out], res[n_out:]


def rms_matmul_nt(x, g, wts, tt, name, comm=None, swiglu=False):
    T, D = x.shape
    N = wts[0].shape[0]
    nw = len(wts)
    n_out = 3 if swiglu else nw
    nc = _tile(N, MXU_COLS_V7X if swiglu else 1408, 128)

    def body(*refs):
        x_ref, g_ref = refs[0], refs[1]
        w_refs, o_refs = refs[2 : 2 + nw], refs[2 + nw : 2 + nw + n_out]
        xv = x_ref[...]
        h = (xv * _rstd(xv) * g_ref[...]).astype(BF16)
        if swiglu:
            silu_ref, dact_ref, act_ref = o_refs
            for j in range(N // nc):
                cols = slice(j * nc, (j + 1) * nc)
                gate = lax.dot_general(h, w_refs[0][cols, :], NT, preferred_element_type=F32)
                up = lax.dot_general(h, w_refs[1][cols, :], NT, preferred_element_type=F32)
                s = _sigmoid(gate)
                t = gate * s
                silu_ref[:, cols] = t.astype(BF16)
                act_ref[:, cols] = (t * up).astype(BF16)
                dact_ref[:, cols] = (up * (s + t - t * s)).astype(BF16)
            return
        for w_ref, o_ref in zip(w_refs, o_refs):
            for j in range(N // nc):
                cols = slice(j * nc, (j + 1) * nc)
                o_ref[:, cols] = lax.dot_general(h, w_ref[cols, :], NT, preferred_element_type=F32).astype(BF16)

    return _token_call(
        body,
        name,
        T // tt,
        [pl.BlockSpec((tt, D), lambda i: (i, 0)), _resident((1, D))] + [_resident((N, D))] * nw,
        [x, g, *wts],
        [pl.BlockSpec((tt, N), lambda i: (i, 0))] * n_out,
        [jax.ShapeDtypeStruct((T, N), BF16)] * n_out,
        [],
        comm,
    )


def matmul_rms_res(x, a, w, g, scale, tt, name, target=None):
    T, D = x.shape
    K = w.shape[0]
    loss = target is not None

    def body(*refs):
        if loss:
            x_ref, a_ref, w_ref, g_ref, t_ref, dy_ref, m_ref, sq_ref = refs

            @pl.when(pl.program_id(0) == 0)
            def _():
                sq_ref[...] = jnp.zeros_like(sq_ref)

        else:
            x_ref, a_ref, w_ref, g_ref, y_ref, m_ref = refs
        m = jnp.dot(a_ref[...], w_ref[...], preferred_element_type=F32)
        m_ref[...] = m.astype(BF16)
        y = x_ref[...] + scale * (m * _rstd(m) * g_ref[...])
        if loss:
            err = y - t_ref[...]
            dy_ref[...] = err * (1.0 / D)
            sq_ref[...] += _sum8(err * err)
        else:
            y_ref[...] = y

    row = pl.BlockSpec((tt, D), lambda i: (i, 0))
    outs, _ = _token_call(
        body,
        name,
        T // tt,
        [row, pl.BlockSpec((tt, K), lambda i: (i, 0)), _resident((K, D)), _resident((1, D))] + ([row] if loss else []),
        [x, a, w, g] + ([target] if loss else []),
        [row, row] + ([pl.BlockSpec((8, D), lambda i: (0, 0))] if loss else []),
        [jax.ShapeDtypeStruct((T, D), F32), jax.ShapeDtypeStruct((T, D), BF16)] + ([jax.ShapeDtypeStruct((8, D), F32)] if loss else []),
        [],
        None,
    )
    return outs


def rms_bwd_matmul_nt(dxo, m, g, w, scale, gate_up, tt, name, comm=None):
    T, D = dxo.shape
    K = w.shape[0]
    act = gate_up is not None
    ec = _tile(K, MXU_COLS_V7X if act else 1024, 128)

    def body(*refs):
        if act:
            dxo_ref, m_ref, g_ref, w_ref, silu_ref, dact_ref, dm_ref, dg_ref, dgate_ref, dup_ref = refs
        else:
            dxo_ref, m_ref, g_ref, w_ref, dm_ref, dg_ref, da_ref = refs

        @pl.when(pl.program_id(0) == 0)
        def _():
            dg_ref[...] = jnp.zeros_like(dg_ref)

        dy = scale * dxo_ref[...]
        mv = m_ref[...].astype(F32)
        r = _rstd(mv)
        mh = mv * r
        dg_ref[...] += _sum8(dy * mh)
        dmh = dy * g_ref[...]
        dm = (r * (dmh - mh * jnp.mean(dmh * mh, axis=-1, keepdims=True))).astype(BF16)
        dm_ref[...] = dm
        for j in range(K // ec):
            cols = slice(j * ec, (j + 1) * ec)
            da = lax.dot_general(dm, w_ref[cols, :], NT, preferred_element_type=F32)
            if act:
                dup_ref[:, cols] = (da * silu_ref[:, cols].astype(F32)).astype(BF16)
                dgate_ref[:, cols] = (da * dact_ref[:, cols].astype(F32)).astype(BF16)
            else:
                da_ref[:, cols] = da.astype(BF16)

    row = pl.BlockSpec((tt, D), lambda i: (i, 0))
    col = pl.BlockSpec((tt, K), lambda i: (i, 0))
    n_col = 2 if act else 1
    return _token_call(
        body,
        name,
        T // tt,
        [row, row, _resident((1, D)), _resident((K, D))] + ([col, col] if act else []),
        [dxo, m, g, w, *(gate_up or ())],
        [row, pl.BlockSpec((8, D), lambda i: (0, 0))] + [col] * n_col,
        [jax.ShapeDtypeStruct((T, D), BF16), jax.ShapeDtypeStruct((8, D), F32)] + [jax.ShapeDtypeStruct((T, K), BF16)] * n_col,
        [],
        comm,
    )


def matmul_rms_bwd(dxo, x, g, pairs, tt, name, comm=None):
    T, D = x.shape
    N = pairs[0][0].shape[1]
    npairs = len(pairs)

    def body(*refs):
        dxo_ref, x_ref, g_ref = refs[:3]
        dz_refs = refs[3 : 3 + npairs]
        wt_refs = refs[3 + npairs : 3 + 2 * npairs]
        dx_ref, h_ref, dg_ref = refs[3 + 2 * npairs :]

        @pl.when(pl.program_id(0) == 0)
        def _():
            dg_ref[...] = jnp.zeros_like(dg_ref)

        dh = None
        for dz_ref, wt_ref in zip(dz_refs, wt_refs):
            part = jnp.dot(dz_ref[...], wt_ref[...], preferred_element_type=F32)
            dh = part if dh is None else dh + part
        xv = x_ref[...]
        gv = g_ref[...]
        r = _rstd(xv)
        xh = xv * r
        h_ref[...] = (xh * gv).astype(BF16)
        dg_ref[...] += _sum8(dh * xh)
        dxh = dh * gv
        dx_ref[...] = dxo_ref[...] + r * (dxh - xh * jnp.mean(dxh * xh, axis=-1, keepdims=True))

    row = pl.BlockSpec((tt, D), lambda i: (i, 0))
    return _token_call(
        body,
        name,
        T // tt,
        [row, row, _resident((1, D))] + [pl.BlockSpec((tt, N), lambda i: (i, 0))] * npairs + [_resident((N, D))] * npairs,
        [dxo, x, g, *[p[0] for p in pairs], *[p[1] for p in pairs]],
        [row, row, pl.BlockSpec((8, D), lambda i: (0, 0))],
        [jax.ShapeDtypeStruct((T, D), F32), jax.ShapeDtypeStruct((T, D), BF16), jax.ShapeDtypeStruct((8, D), F32)],
        [],
        comm,
    )


def wgrad_tn(a, b, tk, name, comm=None):
    T, M = a.shape
    N = b.shape[1]
    assert M * N * 4 <= WGRAD_ACC_BYTES, (M, N)
    nt = T // tk

    def body(a_ref, b_ref, o_ref, acc_ref):
        k = pl.program_id(0)

        @pl.when(k == 0)
        def _():
            acc_ref[...] = jnp.zeros_like(acc_ref)

        acc_ref[...] += lax.dot_general(a_ref[...], b_ref[...], TN, preferred_element_type=F32)

        @pl.when(k == nt - 1)
        def _():
            o_ref[...] = acc_ref[...].astype(BF16)

    (out,), got = _token_call(
        body,
        name,
        nt,
        [pl.BlockSpec((tk, M), lambda k: (k, 0)), pl.BlockSpec((tk, N), lambda k: (k, 0))],
        [a, b],
        [pl.BlockSpec((M, N), lambda k: (0, 0))],
        [jax.ShapeDtypeStruct((M, N), BF16)],
        [pltpu.VMEM((M, N), F32)],
        comm,
    )
    return (out, got) if comm else out


def _sgu_recompute(z_ref, lng_ref, lnb_ref, S):
    z = z_ref[...].astype(F32)
    zu, zv = z[:, :S], z[:, S:]
    pu, pv = _gelu_parts(zu), _gelu_parts(zv)
    u = zu * pu[2]
    v = zv * pv[2]
    vc = v - jnp.mean(v, axis=-1, keepdims=True)
    rstd = lax.rsqrt(jnp.mean(vc * vc, axis=-1, keepdims=True) + EPS)
    vhat = vc * rstd
    vln = (vhat * lng_ref[...] + lnb_ref[...]).astype(BF16)
    return zu, zv, pu, pv, u, vhat, rstd, vln


def sgu_mix_fwd(zpre, lng, lnb, ws, bias_full, tt, name, comm=None):
    T, S2 = zpre.shape
    S = S2 // 2
    dg = S // GROUPS

    def body(z_ref, lng_ref, lnb_ref, ws_ref, bias_ref, o_ref):
        _, _, _, _, u, _, _, vln = _sgu_recompute(z_ref, lng_ref, lnb_ref, S)
        for c in range(tt // CHUNK):
            rows = slice(c * CHUNK, (c + 1) * CHUNK)
            for gi in range(GROUPS):
                cols = slice(gi * dg, (gi + 1) * dg)
                mixed = jnp.dot(ws_ref[gi], vln[rows, cols], preferred_element_type=F32) + bias_ref[:, cols]
                o_ref[rows, cols] = (u[rows, cols] * mixed).astype(BF16)

    vec = pl.BlockSpec((1, S), lambda i: (0, 0))
    return _token_call(
        body,
        name,
        T // tt,
        [
            pl.BlockSpec((tt, S2), lambda i: (i, 0)),
            vec,
            vec,
            pl.BlockSpec((GROUPS, CHUNK, CHUNK), lambda i: (0, 0, 0)),
            pl.BlockSpec((CHUNK, S), lambda i: (0, 0)),
        ],
        [zpre, lng, lnb, ws, bias_full],
        [pl.BlockSpec((tt, S), lambda i: (i, 0))],
        [jax.ShapeDtypeStruct((T, S), BF16)],
        [],
        comm,
    )


def sgu_mix_bwd(dgated, zpre, lng, lnb, ws, ws_t, bias_full, tt, name, comm=None):
    T, S2 = zpre.shape
    S = S2 // 2
    dg = S // GROUPS

    def body(dgt_ref, z_ref, lng_ref, lnb_ref, ws_ref, wst_ref, bias_ref, dz_ref, dws_ref, dbias_ref, dlng_ref, dlnb_ref, du_ref, dvln_ref):
        @pl.when(pl.program_id(0) == 0)
        def _():
            dws_ref[...] = jnp.zeros_like(dws_ref)
            dbias_ref[...] = jnp.zeros_like(dbias_ref)
            dlng_ref[...] = jnp.zeros_like(dlng_ref)
            dlnb_ref[...] = jnp.zeros_like(dlnb_ref)

        zu, zv, pu, pv, u, vhat, rstd, vln = _sgu_recompute(z_ref, lng_ref, lnb_ref, S)
        for c in range(tt // CHUNK):
            rows = slice(c * CHUNK, (c + 1) * CHUNK)
            for gi in range(GROUPS):
                cols = slice(gi * dg, (gi + 1) * dg)
                v_cg = vln[rows, cols]
                mixed = jnp.dot(ws_ref[gi], v_cg, preferred_element_type=F32) + bias_ref[:, cols]
                dgt = dgt_ref[rows, cols].astype(F32)
                du_ref[rows, cols] = dgt * mixed
                dmx = dgt * u[rows, cols]
                dbias_ref[:, cols] += dmx
                dmx16 = dmx.astype(BF16)
                dws_ref[gi] += lax.dot_general(dmx16, v_cg, NT, preferred_element_type=F32)
                dvln_ref[rows, cols] = jnp.dot(wst_ref[gi], dmx16, preferred_element_type=F32)

        dvln = dvln_ref[...]
        dlng_ref[...] += _sum8(dvln * vhat)
        dlnb_ref[...] += _sum8(dvln)
        dvhat = dvln * lng_ref[...]
        dv = rstd * (dvhat - jnp.mean(dvhat, axis=-1, keepdims=True) - vhat * jnp.mean(dvhat * vhat, axis=-1, keepdims=True))
        dz_ref[:, :S] = (du_ref[...] * _gelu_grad(zu, pu)).astype(BF16)
        dz_ref[:, S:] = (dv * _gelu_grad(zv, pv)).astype(BF16)

    vec = pl.BlockSpec((1, S), lambda i: (0, 0))
    wsp = pl.BlockSpec((GROUPS, CHUNK, CHUNK), lambda i: (0, 0, 0))
    full = pl.BlockSpec((CHUNK, S), lambda i: (0, 0))
    acc8 = pl.BlockSpec((8, S), lambda i: (0, 0))
    return _token_call(
        body,
        name,
        T // tt,
        [pl.BlockSpec((tt, S), lambda i: (i, 0)), pl.BlockSpec((tt, S2), lambda i: (i, 0)), vec, vec, wsp, wsp, full],
        [dgated, zpre, lng, lnb, ws, ws_t, bias_full],
        [pl.BlockSpec((tt, S2), lambda i: (i, 0)), wsp, full, acc8, acc8],
        [
            jax.ShapeDtypeStruct((T, S2), BF16),
            jax.ShapeDtypeStruct((GROUPS, CHUNK, CHUNK), F32),
            jax.ShapeDtypeStruct((CHUNK, S), F32),
            jax.ShapeDtypeStruct((8, S), F32),
            jax.ShapeDtypeStruct((8, S), F32),
        ],
        [pltpu.VMEM((tt, S), F32), pltpu.VMEM((tt, S), F32)],
        comm,
    )


def _shifted_planes(sh, tt):
    n = tt + HALO - SUBLANES
    for s in range(1, SUBLANES):
        sh[s, 0:n, :] = sh[0, pl.ds(s, n), :]


def conv_mid_fwd(p, wdw, bdw, lng, lnb, tt, name, comm=None):
    T, C2 = p.shape
    C = C2 // 2

    def body(p_ref, w_ref, b_ref, lng_ref, lnb_ref, yc_ref, ys_ref, sh):
        @pl.when(pl.program_id(0) == 0)
        def _():
            sh[0, 0:HALO, :] = jnp.zeros((HALO, C), F32)

        @pl.when(pl.program_id(0) > 0)
        def _():
            sh[0, 0:HALO, :] = sh[0, tt : tt + HALO, :]

        pv = p_ref[...].astype(F32)
        sh[0, HALO:, :] = pv[:, :C] * _sigmoid(pv[:, C:])
        _shifted_planes(sh, tt)
        for lb in range(C // LANES):
            lanes = slice(lb * LANES, (lb + 1) * LANES)
            wk = [jnp.broadcast_to(w_ref[k : k + 1, lanes], (SUBLANES, LANES)) for k in range(CONV_W)]
            bias = jnp.broadcast_to(b_ref[:, lanes], (SUBLANES, LANES))

            def rows(i, carry):
                r0 = pl.multiple_of(i * CONV_ROWS, CONV_ROWS)
                accs = [[bias, jnp.zeros((SUBLANES, LANES), F32)] for _ in range(CONV_ROWS // SUBLANES)]
                for k in range(CONV_W):
                    o = HALO - (CONV_W - 1) + k
                    for j, acc in enumerate(accs):
                        blk = sh[o % SUBLANES, pl.ds(r0 + (o // SUBLANES + j) * SUBLANES, SUBLANES), lanes]
                        acc[k % 2] = acc[k % 2] + wk[k] * blk
                for j, acc in enumerate(accs):
                    yc_ref[pl.ds(r0 + j * SUBLANES, SUBLANES), lanes] = acc[0] + acc[1]
                return carry

            lax.fori_loop(0, tt // CONV_ROWS, rows, 0)
        acc = yc_ref[...]
        yc = acc - jnp.mean(acc, axis=-1, keepdims=True)
        yn = yc * lax.rsqrt(jnp.mean(yc * yc, axis=-1, keepdims=True) + EPS) * lng_ref[...] + lnb_ref[...]
        ys_ref[...] = (yn * _sigmoid(yn)).astype(BF16)

    vec = pl.BlockSpec((1, C), lambda i: (0, 0))
    row = pl.BlockSpec((tt, C), lambda i: (i, 0))
    return _token_call(
        body,
        name,
        T // tt,
        [pl.BlockSpec((tt, C2), lambda i: (i, 0)), pl.BlockSpec((HALO, C), lambda i: (0, 0)), vec, vec, vec],
        [p, wdw, bdw, lng, lnb],
        [row, row],
        [jax.ShapeDtypeStruct((T, C), F32), jax.ShapeDtypeStruct((T, C), BF16)],
        [pltpu.VMEM((SUBLANES, tt + HALO, C), F32)],
        comm,
    )


def conv_mid_bwd(dys, yc, p, wdw, lng, lnb, tt, name, comm=None):
    T, C2 = p.shape
    C = C2 // 2
    n = T // tt

    def body(dys_ref, yc_ref, p_ref, w_ref, lng_ref, lnb_ref, dp_ref, dw_ref, db_ref, dlng_ref, dlnb_ref, sh, y_s, dy_s):
        @pl.when(pl.program_id(0) == 0)
        def _():
            sh[0, tt : tt + HALO, :] = jnp.zeros((HALO, C), F32)
            dw_ref[...] = jnp.zeros_like(dw_ref)
            db_ref[...] = jnp.zeros_like(db_ref)
            dlng_ref[...] = jnp.zeros_like(dlng_ref)
            dlnb_ref[...] = jnp.zeros_like(dlnb_ref)

        @pl.when(pl.program_id(0) > 0)
        def _():
            sh[0, tt : tt + HALO, :] = sh[0, 0:HALO, :]

        ycv = yc_ref[...]
        ycc = ycv - jnp.mean(ycv, axis=-1, keepdims=True)
        rstd = lax.rsqrt(jnp.mean(ycc * ycc, axis=-1, keepdims=True) + EPS)
        yhat = ycc * rstd
        lng_v = lng_ref[...]
        yn = yhat * lng_v + lnb_ref[...]
        sg = _sigmoid(yn)
        dyn = dys_ref[...].astype(F32) * (sg * (1.0 + yn * (1.0 - sg)))
        dlng_ref[...] += _sum8(dyn * yhat)
        dlnb_ref[...] += _sum8(dyn)
        dyh = dyn * lng_v
        dyc = rstd * (dyh - jnp.mean(dyh, axis=-1, keepdims=True) - yhat * jnp.mean(dyh * yhat, axis=-1, keepdims=True))
        db_ref[...] += _sum8(dyc)
        sh[0, 0:tt, :] = dyc
        _shifted_planes(sh, tt)

        pv = p_ref[...].astype(F32)
        a = pv[:, :C]
        sgate = _sigmoid(pv[:, C:])
        y_s[...] = a * sgate
        for lb in range(C // LANES):
            lanes = slice(lb * LANES, (lb + 1) * LANES)
            for k0, k1 in ((0, CONV_W // 2), (CONV_W // 2, CONV_W)):
                wk = [jnp.broadcast_to(w_ref[k : k + 1, lanes], (SUBLANES, LANES)) for k in range(k0, k1)]

                def rows(i, dw_acc, k0=k0, k1=k1, wk=wk, lanes=lanes):
                    r0 = pl.multiple_of(i * CONV_ROWS, CONV_ROWS)
                    dw_acc = list(dw_acc)
                    zero = jnp.zeros((SUBLANES, LANES), F32)
                    ybs, accs = [], []
                    for j in range(CONV_ROWS // SUBLANES):
                        at = pl.ds(r0 + j * SUBLANES, SUBLANES)
                        ybs.append(y_s[at, lanes])
                        accs.append([zero if k0 == 0 else dy_s[at, lanes], zero])
                    for n_k, k in enumerate(range(k0, k1)):
                        o = CONV_W - 1 - k
                        for j, acc in enumerate(accs):
                            blk = sh[o % SUBLANES, pl.ds(r0 + (o // SUBLANES + j) * SUBLANES, SUBLANES), lanes]
                            acc[n_k % 2] = acc[n_k % 2] + wk[n_k] * blk
                            dw_acc[n_k] = dw_acc[n_k] + ybs[j] * blk
                    for j, acc in enumerate(accs):
                        dy_s[pl.ds(r0 + j * SUBLANES, SUBLANES), lanes] = acc[0] + acc[1]
                    return tuple(dw_acc)

                dw_acc = lax.fori_loop(0, tt // CONV_ROWS, rows, tuple(jnp.zeros((SUBLANES, LANES), F32) for _ in range(k0, k1)))
                for n_k, k in enumerate(range(k0, k1)):
                    dw_ref[SUBLANES * k : SUBLANES * (k + 1), lanes] += dw_acc[n_k]
        dy = dy_s[...]
        dp_ref[:, :C] = (dy * sgate).astype(BF16)
        dp_ref[:, C:] = (dy * a * sgate * (1.0 - sgate)).astype(BF16)

    vec = pl.BlockSpec((1, C), lambda i: (0, 0))
    row = pl.BlockSpec((tt, C), lambda i: (n - 1 - i, 0))
    row2 = pl.BlockSpec((tt, C2), lambda i: (n - 1 - i, 0))
    acc8 = pl.BlockSpec((8, C), lambda i: (0, 0))
    return _token_call(
        body,
        name,
        n,
        [row, row, row2, pl.BlockSpec((HALO, C), lambda i: (0, 0)), vec, vec],
        [dys, yc, p, wdw, lng, lnb],
        [row2, pl.BlockSpec((HALO * 8, C), lambda i: (0, 0)), acc8, acc8, acc8],
        [
            jax.ShapeDtypeStruct((T, C2), BF16),
            jax.ShapeDtypeStruct((HALO * 8, C), F32),
            jax.ShapeDtypeStruct((8, C), F32),
            jax.ShapeDtypeStruct((8, C), F32),
            jax.ShapeDtypeStruct((8, C), F32),
        ],
        [pltpu.VMEM((SUBLANES, tt + HALO, C), F32), pltpu.VMEM((tt, C), F32), pltpu.VMEM((tt, C), F32)],
        comm,
    )


def sum_slots(slots, name):
    _, r, w = slots.shape

    def body(s_ref, o_ref):
        total = s_ref[0].astype(F32)
        for p in range(1, N_DEV):
            total = total + s_ref[p].astype(F32)
        o_ref[...] = total

    return _pcall(body, name=name, out_shape=jax.ShapeDtypeStruct((r, w), F32), compiler_params=pltpu.CompilerParams(vmem_limit_bytes=VMEM_LIMIT_V7X))(slots)


def adamw(w, g, m, v, name):
    R, C = w.shape
    tr = _tile(R, 512, 8)
    c1 = 1.0 - ADAM_B1**ADAM_STEP
    c2 = 1.0 - ADAM_B2**ADAM_STEP

    def body(w_ref, g_ref, m_ref, v_ref, d_ref, mo_ref, vo_ref):
        gv = g_ref[...]
        m2 = ADAM_B1 * m_ref[...] + (1.0 - ADAM_B1) * gv
        v2 = ADAM_B2 * v_ref[...] + (1.0 - ADAM_B2) * (gv * gv)
        mo_ref[...] = m2
        vo_ref[...] = v2
        d_ref[...] = -ADAM_LR * ((m2 / c1) / (jnp.sqrt(v2 / c2) + ADAM_EPS) + ADAM_WD * w_ref[...])

    blk = pl.BlockSpec((tr, C), lambda i: (i, 0))
    return _pcall(
        body,
        name=name,
        grid=(R // tr,),
        in_specs=[blk] * 4,
        out_specs=[blk] * 3,
        out_shape=[jax.ShapeDtypeStruct((R, C), F32)] * 3,
        compiler_params=_cparams(1),
    )(w, g, m, v)


def _sublayers(depth):
    out = []
    for layer in range(depth):
        out.append(("ffn", layer, 0))
        out.append(("sgu" if layer % N_MIXERS == 0 else "conv", layer, layer // N_MIXERS))
        out.append(("ffn", layer, 1))
    return out


def _row_blocks(sub, shards):
    kind, layer, idx = sub
    if kind == "ffn":
        blk = {"gate_t": shards["ff_w_gate"][layer, idx].T, "up_t": shards["ff_w_up"][layer, idx].T, "down": shards["ff_w_down"][layer, idx]}
    elif kind == "sgu":
        blk = {"in_t": shards["sgu_w_in"][idx].T, "out": shards["sgu_w_out"][idx]}
    else:
        blk = {"pw1_t": shards["conv_w_pw1"][idx].T, "pw2": shards["conv_w_pw2"][idx]}
    return {nm: arr.astype(BF16) for nm, arr in blk.items()}


def _pad8(a):
    return jnp.pad(a, ((0, (-a.shape[0]) % 8), (0, 0)))


def kernel(x, norm_g, ff_w_gate, ff_w_up, ff_w_down, sgu_w_in, sgu_ln_g, sgu_ln_b, sgu_w_spatial, sgu_b_spatial, sgu_w_out, conv_w_pw1, conv_w_dw, conv_b_dw, conv_ln_g, conv_ln_b, conv_w_pw2, loss_target, m_norm_g, m_ff_w_gate, m_ff_w_up, m_ff_w_down, m_sgu_w_in, m_sgu_ln_g, m_sgu_ln_b, m_sgu_w_spatial, m_sgu_b_spatial, m_sgu_w_out, m_conv_w_pw1, m_conv_w_dw, m_conv_b_dw, m_conv_ln_g, m_conv_ln_b, m_conv_w_pw2, v_norm_g, v_ff_w_gate, v_ff_w_up, v_ff_w_down, v_sgu_w_in, v_sgu_ln_g, v_sgu_ln_b, v_sgu_w_spatial, v_sgu_b_spatial, v_sgu_w_out, v_conv_w_pw1, v_conv_w_dw, v_conv_b_dw, v_conv_ln_g, v_conv_ln_b, v_conv_w_pw2):
    names = ["norm_g", "ff_w_gate", "ff_w_up", "ff_w_down", "sgu_w_in", "sgu_ln_g", "sgu_ln_b", "sgu_w_spatial", "sgu_b_spatial", "sgu_w_out", "conv_w_pw1", "conv_w_dw", "conv_b_dw", "conv_ln_g", "conv_ln_b", "conv_w_pw2"]
    weights = dict(zip(names, [norm_g, ff_w_gate, ff_w_up, ff_w_down, sgu_w_in, sgu_ln_g, sgu_ln_b, sgu_w_spatial, sgu_b_spatial, sgu_w_out, conv_w_pw1, conv_w_dw, conv_b_dw, conv_ln_g, conv_ln_b, conv_w_pw2]))
    moments_m = dict(zip(names, [m_norm_g, m_ff_w_gate, m_ff_w_up, m_ff_w_down, m_sgu_w_in, m_sgu_ln_g, m_sgu_ln_b, m_sgu_w_spatial, m_sgu_b_spatial, m_sgu_w_out, m_conv_w_pw1, m_conv_w_dw, m_conv_b_dw, m_conv_ln_g, m_conv_ln_b, m_conv_w_pw2]))
    moments_v = dict(zip(names, [v_norm_g, v_ff_w_gate, v_ff_w_up, v_ff_w_down, v_sgu_w_in, v_sgu_ln_g, v_sgu_ln_b, v_sgu_w_spatial, v_sgu_b_spatial, v_sgu_w_out, v_conv_w_pw1, v_conv_w_dw, v_conv_b_dw, v_conv_ln_g, v_conv_ln_b, v_conv_w_pw2]))

    _, T, D = x.shape
    depth = norm_g.shape[0]
    n_conv = conv_w_dw.shape[0]
    n_sgu = sgu_w_in.shape[0]
    S = sgu_ln_g.shape[1]
    lanes = norm_g.shape[2]
    subs = _sublayers(depth)
    n_sub = len(subs)

    cx, cy, cc = (lax.axis_index(a) for a in MESH_AXES)
    my_block = 4 * cx + 2 * cy + cc

    blocks = [_row_blocks(sub, weights) for sub in subs]
    dw_pad = jnp.pad(conv_w_dw, ((0, 0), (0, HALO - CONV_W), (0, 0)))
    small_parts = [_pad8(p) for p in (norm_g.reshape(-1, lanes), dw_pad.reshape(-1, lanes), conv_b_dw, conv_ln_g, conv_ln_b)]
    small_rows = [p.shape[0] for p in small_parts]
    small = jnp.concatenate(small_parts, axis=0)

    def gathered(block_names, outs):
        return {nm: o.reshape(N_DEV * o.shape[1], o.shape[2]) for nm, o in zip(block_names, outs)}

    first_names = [nm for nm in blocks[0] if nm != "down"]
    first = gather_two_level([blocks[0][nm] for nm in first_names] + [small], name="gather_first")
    W = [None] * n_sub
    W[0] = gathered(first_names, first[:-1])
    gsmall = first[-1]

    small_full = jnp.transpose(gsmall, (1, 0, 2)).reshape(gsmall.shape[1], N_DEV * lanes)
    so = [0]
    for r in small_rows:
        so.append(so[-1] + r)
    norm_full = small_full[so[0] : so[0] + depth * norm_g.shape[1]].reshape(depth, -1, D)
    dw_full = small_full[so[1] : so[1] + n_conv * HALO].reshape(n_conv, HALO, D)
    bdw_full, clng_full, clnb_full = (small_full[so[k] : so[k] + n_conv] for k in (2, 3, 4))

    causal = jnp.tril(jnp.ones((CHUNK, CHUNK), dtype=bool))
    ws_all = jnp.where(causal[None, None], sgu_w_spatial, 0.0).astype(BF16)
    wst_all = jnp.swapaxes(ws_all, -1, -2)
    bias_full_all = jnp.repeat(jnp.swapaxes(sgu_b_spatial, -1, -2), S // GROUPS, axis=-1)

    tt = _tile(T, 512, CHUNK)
    tt_mix = _tile(T, 256, CHUNK)

    tk = _tile(T, 1024, CHUNK)

    def vec(v):
        return v.reshape(1, -1)

    def split_first(comm):
        return (comm[:1], comm[1:]) if comm else (None, None)

    def norms(kind, layer, idx):
        pre = 4 * idx if kind == "ffn" else 2
        return vec(norm_full[layer, pre]), vec(norm_full[layer, pre + 1])

    xs = x[0]
    saved = []
    for si, (kind, layer, idx) in enumerate(subs):
        w = W[si]
        g_pre, g_post = norms(kind, layer, idx)
        nxt = [("gather", a) for a in blocks[si + 1].values()] if si + 1 < n_sub else None
        nxt_first, nxt_rest = split_first(nxt)
        if kind == "ffn":
            own = [("gather", blocks[0]["down"])] if si == 0 else []
            (silu, dact, a), got = rms_matmul_nt(xs, g_pre, [w["gate_t"], w["up_t"]], tt, "ffn_in", own + (nxt or []), swiglu=True)
            if own:
                w.update(gathered(["down"], got[:1]))
                got = got[1:]
            if si == n_sub - 1:
                x_new, o, sq = matmul_rms_res(xs, a, w["down"], g_post, FFN_SCALE, tt, "ffn_out_loss", loss_target[0])
            else:
                x_new, o = matmul_rms_res(xs, a, w["down"], g_post, FFN_SCALE, tt, "ffn_out")
            saved.append((xs, silu, dact, o, a))
        elif kind == "sgu":
            (zpre,), got = rms_matmul_nt(xs, g_pre, [w["in_t"]], tt, "sgu_in", nxt_first)
            (gated,), got_rest = sgu_mix_fwd(zpre, vec(sgu_ln_g[idx]), vec(sgu_ln_b[idx]), ws_all[idx], bias_full_all[idx], tt_mix, "sgu_mix", nxt_rest)
            got = list(got) + list(got_rest)
            x_new, mm = matmul_rms_res(xs, gated, w["out"], g_post, 1.0, tt, "sgu_out")
            saved.append((xs, zpre, gated, mm))
        else:
            (p,), got = rms_matmul_nt(xs, g_pre, [w["pw1_t"]], tt, "conv_pw1", nxt_first)
            (yc, ys), got_rest = conv_mid_fwd(p, dw_full[idx], vec(bdw_full[idx]), vec(clng_full[idx]), vec(clnb_full[idx]), tt, "conv_mid", nxt_rest)
            got = list(got) + list(got_rest)
            x_new, mm = matmul_rms_res(xs, ys, w["pw2"], g_post, 1.0, tt, "conv_pw2")
            saved.append((xs, p, yc, ys, mm))
        if nxt:
            W[si + 1] = gathered(blocks[si + 1].keys(), got)
        xs = x_new

    dx = xs
    loss = lax.psum(0.5 * jnp.sum(sq) / D, MESH_AXES)

    d_norm = [[None] * norm_full.shape[1] for _ in range(depth)]
    d_sgu = [None] * n_sgu
    d_conv = [None] * n_conv
    slots = [None] * n_sub
    pending = None

    def scatter_of(p):
        return [("scatter", dwm.reshape(N_DEV, dwm.shape[0] // N_DEV, D)) for dwm in p[1].values()] if p else None

    def pack_small():
        def sum8(v):
            return jnp.sum(v, axis=0)

        g_norm = jnp.stack([jnp.stack([sum8(d) for d in row]) for row in d_norm])
        g_dw = jnp.stack([jnp.sum(d[0].reshape(HALO, 8, D), axis=1) for d in d_conv])
        g_bdw, g_clng, g_clnb = (jnp.stack([sum8(d[k]) for d in d_conv]) for k in (1, 2, 3))
        g_slng, g_slnb = (jnp.stack([sum8(d[k]) for d in d_sgu]) for k in (2, 3))
        g_bsp = jnp.stack([jnp.sum(d[1].reshape(CHUNK, GROUPS, S // GROUPS), axis=-1).T for d in d_sgu])
        g_wsp = jnp.stack([jnp.where(causal[None], d[0], 0.0) for d in d_sgu])
        parts = [g_norm, g_dw, g_bdw, g_clng, g_clnb, g_slng, g_slnb, g_bsp, g_wsp]
        return parts, [p.size // D for p in parts], jnp.concatenate([_pad8(p.reshape(-1, D)) for p in parts], axis=0)

    for si in reversed(range(n_sub)):
        kind, layer, idx = subs[si]
        w = W[si]
        g_pre, g_post = norms(kind, layer, idx)
        pre = 4 * idx if kind == "ffn" else 2
        sc_first, sc_rest = split_first(scatter_of(pending))
        if kind == "ffn":
            xs_in, silu, dact, o, a = saved[si]
            (do, dg_post, dgate, dup), got = rms_bwd_matmul_nt(dx, o, g_post, w["down"], FFN_SCALE, (silu, dact), tt, "ffn_out_bwd", scatter_of(pending))
            dw_down = wgrad_tn(a, do, tk, name="wgrad_ffn_out")
            tail = None
            if si == 0:
                d_norm[layer][pre], d_norm[layer][pre + 1] = jnp.zeros_like(dg_post), dg_post
                sparts, srows, sgrad = pack_small()
                tail = scatter_of((si, {"down": dw_down})) + [("gather", sgrad)]
            (dx, h, dg_pre), got_tail = matmul_rms_bwd(dx, xs_in, g_pre, [(dgate, w["gate_t"]), (dup, w["up_t"])], tt, "ffn_in_bwd", tail)
            dw_up = wgrad_tn(dup, h, tk, name="wgrad_ffn_in")
            if si == 0:
                dw_gate, got_up = wgrad_tn(dgate, h, tk, "wgrad_ffn_in", scatter_of((si, {"up_t": dw_up})))
            else:
                dw_gate = wgrad_tn(dgate, h, tk, name="wgrad_ffn_in")
            dws_now = {"gate_t": dw_gate, "up_t": dw_up, "down": dw_down}
        elif kind == "sgu":
            xs_in, zpre, gated, mm = saved[si]
            (dm, dg_post, dgated), got = rms_bwd_matmul_nt(dx, mm, g_post, w["out"], 1.0, None, tt, "sgu_out_bwd", sc_first)
            (dzpre, dws, dbias, dlng, dlnb), got_rest = sgu_mix_bwd(dgated, zpre, vec(sgu_ln_g[idx]), vec(sgu_ln_b[idx]), ws_all[idx], wst_all[idx], bias_full_all[idx], tt_mix, "sgu_mix_bwd", sc_rest)
            got = list(got) + list(got_rest)
            (dx, h, dg_pre), _ = matmul_rms_bwd(dx, xs_in, g_pre, [(dzpre, w["in_t"])], tt, "sgu_in_bwd")
            dws_now = {"in_t": wgrad_tn(dzpre, h, tt, name="wgrad_sgu_in"), "out": wgrad_tn(gated, dm, tk, name="wgrad_sgu_out")}
            d_sgu[idx] = (dws, dbias, dlng, dlnb)
        else:
            xs_in, p, yc, ys, mm = saved[si]
            (dm, dg_post, dys), got = rms_bwd_matmul_nt(dx, mm, g_post, w["pw2"], 1.0, None, tt, "conv_pw2_bwd", sc_first)
            (dp, dwdw, dbdw, dlng, dlnb), got_rest = conv_mid_bwd(dys, yc, p, dw_full[idx], vec(clng_full[idx]), vec(clnb_full[idx]), tt, "conv_mid_bwd", sc_rest)
            got = list(got) + list(got_rest)
            (dx, h, dg_pre), _ = matmul_rms_bwd(dx, xs_in, g_pre, [(dp, w["pw1_t"])], tt, "conv_pw1_bwd")
            dws_now = {"pw1_t": wgrad_tn(dp, h, tk, name="wgrad_conv_pw1"), "pw2": wgrad_tn(ys, dm, tk, name="wgrad_conv_pw2")}
            d_conv[idx] = (dwdw, dbdw, dlng, dlnb)
        d_norm[layer][pre], d_norm[layer][pre + 1] = dg_pre, dg_post
        if pending:
            slots[pending[0]] = dict(zip(pending[1].keys(), got))
        pending = (si, dws_now)
    grad_x = dx[None]

    last = exchange(scatter_of((0, {"gate_t": pending[1]["gate_t"]})) + [("gather", dg_pre)], name="scatter_last")
    slots[0] = {"gate_t": last[0], "up_t": got_up[0], "down": got_tail[0]}
    stotal = sum_slots(got_tail[1], name="sum_slots")
    g_first = jnp.sum(sum_slots(last[-1], name="sum_slots"), axis=0)

    gs = {(si, nm): sum_slots(s, name="sum_slots") for si in range(n_sub) for nm, s in slots[si].items()}
    grads = {}
    ffn_si = {(layer, idx): si for si, (kind, layer, idx) in enumerate(subs) if kind == "ffn"}
    sgu_si = {idx: si for si, (kind, layer, idx) in enumerate(subs) if kind == "sgu"}
    conv_si = {idx: si for si, (kind, layer, idx) in enumerate(subs) if kind == "conv"}
    grads["ff_w_gate"] = jnp.stack([jnp.stack([gs[(ffn_si[(l, f)], "gate_t")].T for f in range(2)]) for l in range(depth)])
    grads["ff_w_up"] = jnp.stack([jnp.stack([gs[(ffn_si[(l, f)], "up_t")].T for f in range(2)]) for l in range(depth)])
    grads["ff_w_down"] = jnp.stack([jnp.stack([gs[(ffn_si[(l, f)], "down")] for f in range(2)]) for l in range(depth)])
    grads["sgu_w_in"] = jnp.stack([gs[(sgu_si[j], "in_t")].T for j in range(n_sgu)])
    grads["sgu_w_out"] = jnp.stack([gs[(sgu_si[j], "out")] for j in range(n_sgu)])
    grads["conv_w_pw1"] = jnp.stack([gs[(conv_si[j], "pw1_t")].T for j in range(n_conv)])
    grads["conv_w_pw2"] = jnp.stack([gs[(conv_si[j], "pw2")] for j in range(n_conv)])

    so = [0]
    for r in srows:
        so.append(so[-1] + r + (-r) % 8)
    sp = [stotal[so[k] : so[k] + srows[k]].reshape(sparts[k].shape) for k in range(len(sparts))]

    def my_lanes(v):
        return lax.dynamic_slice_in_dim(v, my_block * lanes, lanes, axis=-1)

    grads["norm_g"] = my_lanes(sp[0].at[0, 0].set(g_first))
    grads["conv_w_dw"] = my_lanes(sp[1])[:, :CONV_W]
    grads["conv_b_dw"] = my_lanes(sp[2])
    grads["conv_ln_g"] = my_lanes(sp[3])
    grads["conv_ln_b"] = my_lanes(sp[4])
    grads["sgu_ln_g"], grads["sgu_ln_b"], grads["sgu_b_spatial"], grads["sgu_w_spatial"] = sp[5], sp[6], sp[7], sp[8]

    deltas, new_m, new_v = {}, {}, {}
    for nm in names:
        w = weights[nm]
        two_d = (-1, w.shape[-1])
        d, m2, v2 = adamw(w.reshape(two_d), grads[nm].reshape(two_d), moments_m[nm].reshape(two_d), moments_v[nm].reshape(two_d), name="adamw")
        deltas[nm], new_m[nm], new_v[nm] = d.reshape(w.shape), m2.reshape(w.shape), v2.reshape(w.shape)

    return (loss, grad_x, *[grads[n] for n in names], *[deltas[n] for n in names], *[new_m[n] for n in names], *[new_v[n] for n in names])
```

```python
import jax
import jax.numpy as jnp
from jax import lax
from jax.experimental import pallas as pl
from jax.experimental.pallas import tpu as pltpu

F32 = jnp.float32
BF16 = jnp.bfloat16

EPS = 1e-6
FFN_SCALE = 0.5
N_MIXERS = 2
CHUNK = 128
GROUPS = 8
CONV_W = 31
HALO = 32
GELU_C0 = 0.7978845608028654
GELU_C1 = 0.044715
ADAM_LR, ADAM_B1, ADAM_B2, ADAM_EPS, ADAM_WD, ADAM_STEP = 0.001, 0.9, 0.999, 1e-08, 0.01, 10

MESH_AXES = ("x", "y", "c")
N_DEV = 8
VMEM_LIMIT_V7X = 56 * 1024 * 1024
WGRAD_ACC_BYTES = 16 * 1024 * 1024
MXU_COLS_V7X = 256
SUBLANES, LANES = 8, 128
CONV_ROWS = 32

NT = (((1,), (1,)), ((), ()))
TN = (((0,), (0,)), ((), ()))

HBM_SPEC = pl.BlockSpec(memory_space=pl.ANY)
MESH_ID = pl.DeviceIdType.MESH


def _pcall(body, **kw):
    return pl.pallas_call(body, **kw)


def _cparams(n_axes):
    return pltpu.CompilerParams(dimension_semantics=("arbitrary",) * n_axes, vmem_limit_bytes=VMEM_LIMIT_V7X)


def _tile(n, pref, align):
    if n <= pref:
        return n
    t = (pref // align) * align
    while t > align and n % t:
        t -= align
    assert n % t == 0, (n, pref, align)
    return t


def _rstd(x):
    return lax.rsqrt(jnp.mean(x * x, axis=-1, keepdims=True) + EPS)


def _sum8(v):
    t, d = v.shape
    return v.reshape(t // 8, 8, d).sum(axis=0)


def _sigmoid(x):
    return jax.nn.sigmoid(x)


def _gelu_parts(z):
    zz = z * z
    t = jnp.tanh(z * (GELU_C0 + (GELU_C0 * GELU_C1) * zz))
    return zz, t, 0.5 * t + 0.5


def _gelu_grad(z, parts):
    zz, t, cdf = parts
    return cdf + z * (1.0 - t * t) * ((0.5 * GELU_C0) + (1.5 * GELU_C0 * GELU_C1) * zz)


def _resident(shape):
    return pl.BlockSpec(shape, lambda i: (0,) * len(shape), pipeline_mode=pl.Buffered(1))


def _exchange_io(comm):
    n = len(comm)
    out_shape = [jax.ShapeDtypeStruct(((N_DEV,) + a.shape) if kind == "gather" else a.shape, a.dtype) for kind, a in comm]
    scratch = [pltpu.SemaphoreType.DMA((n, N_DEV - 1)), pltpu.SemaphoreType.DMA((n, N_DEV - 1)), pltpu.SemaphoreType.DMA((n,))]
    return [HBM_SPEC] * n, [HBM_SPEC] * n, out_shape, scratch


def _exchange_copies(kinds, in_refs, out_refs, send_sems, recv_sems, local_sems, with_arrivals=True):
    x, y, c = (lax.axis_index(a) for a in MESH_AXES)
    me = 4 * x + 2 * y + c
    local, sends, arrivals = [], [], []
    for a, kind in enumerate(kinds):
        src, dst = in_refs[a], out_refs[a]
        gather = kind == "gather"
        local.append(pltpu.make_async_copy(src if gather else src.at[me], dst.at[me], local_sems.at[a]))
        for k in range(N_DEV - 1):
            mask = k + 1
            px = 1 - x if mask & 4 else x
            py = 1 - y if mask & 2 else y
            pc = 1 - c if mask & 1 else c
            peer = 4 * px + 2 * py + pc
            block = src if gather else src.at[peer]
            for into, lst in ((me, sends), (peer, arrivals)):
                if lst is arrivals and not with_arrivals:
                    continue
                lst.append(
                    pltpu.make_async_remote_copy(
                        src_ref=block, dst_ref=dst.at[into], send_sem=send_sems.at[a, k], recv_sem=recv_sems.at[a, k], device_id=(px, py, pc), device_id_type=MESH_ID
                    )
                )
    return local, sends, arrivals


def _exchange_start(copies):
    local, sends, _ = copies
    for cp in local + sends:
        cp.start()


def _exchange_finish(copies):
    local, sends, arrivals = copies
    for cp in arrivals:
        cp.wait_recv()
    for cp in sends:
        cp.wait_send()
    for cp in local:
        cp.wait()


def exchange(comm, name):
    kinds = [k for k, _ in comm]
    n = len(comm)
    in_specs, out_specs, out_shape, scratch = _exchange_io(comm)

    def body(*refs):
        copies = _exchange_copies(kinds, refs[:n], refs[n : 2 * n], *refs[2 * n :])
        _exchange_start(copies)
        _exchange_finish(copies)

    return _pcall(body, name=name, in_specs=in_specs, out_specs=out_specs, out_shape=out_shape, scratch_shapes=scratch)(*[a for _, a in comm])


def gather_two_level(packs, name):
    n = len(packs)

    def body(*refs):
        in_refs, out_refs = refs[:n], refs[n : 2 * n]
        send_sems, recv_sems, local_sems = refs[2 * n :]
        x, y, c = (lax.axis_index(a) for a in MESH_AXES)
        chips = [(1 - x, y), (x, 1 - y), (1 - x, 1 - y)]
        me, sibling = (x, y, c), (x, y, 1 - c)

        def slot(a, px, py, pc):
            return out_refs[a].at[4 * px + 2 * py + pc]

        def copy(a, k, block, to, from_input=False):
            return pltpu.make_async_remote_copy(
                src_ref=in_refs[a] if from_input else slot(a, *block),
                dst_ref=slot(a, *block),
                send_sem=send_sems.at[a, k],
                recv_sem=recv_sems.at[a, k],
                device_id=to,
                device_id_type=MESH_ID,
            )

        mine = [pltpu.make_async_copy(in_refs[a], slot(a, *me), local_sems.at[a]) for a in range(n)]
        for cp in mine:
            cp.start()
        first = []
        for a in range(n):
            first.append(copy(a, 0, me, sibling, from_input=True))
            first += [copy(a, 1 + j, me, (*chip, c), from_input=True) for j, chip in enumerate(chips)]
        for cp in first:
            cp.start()
        passed = []
        for j, chip in enumerate(chips):
            for a in range(n):
                copy(a, 1 + j, (*chip, c), me).wait_recv()
                fwd = copy(a, 4 + j, (*chip, c), sibling)
                fwd.start()
                passed.append(fwd)
        for a in range(n):
            copy(a, 0, sibling, me).wait_recv()
            for j, chip in enumerate(chips):
                copy(a, 4 + j, (*chip, 1 - c), me).wait_recv()
        for cp in first + passed:
            cp.wait_send()
        for cp in mine:
            cp.wait()

    return _pcall(
        body,
        name=name,
        in_specs=[HBM_SPEC] * n,
        out_specs=[HBM_SPEC] * n,
        out_shape=[jax.ShapeDtypeStruct((N_DEV,) + p.shape, p.dtype) for p in packs],
        scratch_shapes=[pltpu.SemaphoreType.DMA((n, N_DEV - 1)), pltpu.SemaphoreType.DMA((n, N_DEV - 1)), pltpu.SemaphoreType.DMA((n,))],
    )(*packs)


def _token_call(body, name, n_tiles, in_specs, inputs, out_specs, out_shape, scratch, comm):
    n_in, n_out, n_scr = len(inputs), len(out_shape), len(scratch)
    comm = comm or []
    nc = len(comm)
    full_body = body
    if nc:
        kinds = [k for k, _ in comm]
        c_in, c_out, c_shape, c_scr = _exchange_io(comm)
        in_specs, out_specs, out_shape, scratch = in_specs + c_in, out_specs + c_out, out_shape + c_shape, scratch + c_scr

        def full_body(*refs):
            ins, cins = refs[:n_in], refs[n_in : n_in + nc]
            o0 = n_in + nc
            outs, couts = refs[o0 : o0 + n_out], refs[o0 + n_out : o0 + n_out + nc]
            s0 = o0 + n_out + nc
            scr, sems = refs[s0 : s0 + n_scr], refs[s0 + n_scr :]

            @pl.when(pl.program_id(0) == 0)
            def _():
                _exchange_start(_exchange_copies(kinds, cins, couts, *sems, with_arrivals=False))

            body(*ins, *outs, *scr)

            @pl.when(pl.program_id(0) == n_tiles - 1)
            def _():
                _exchange_finish(_exchange_copies(kinds, cins, couts, *sems))

    res = _pcall(
        full_body, name=name, grid=(n_tiles,), in_specs=in_specs, out_specs=out_specs, out_shape=out_shape, scratch_shapes=scratch, compiler_params=_cparams(1)
    )(*inputs, *[a for _, a in comm])
    return res[:n_out], res[n_out:]


def rms_matmul_nt(x, g, wts, tt, name, comm=None, swiglu=False):
    T, D = x.shape
    N = wts[0].shape[0]
    nw = len(wts)
    n_out = 3 if swiglu else nw
    nc = _tile(N, MXU_COLS_V7X if swiglu else 1408, 128)

    def body(*refs):
        x_ref, g_ref = refs[0], refs[1]
        w_refs, o_refs = refs[2 : 2 + nw], refs[2 + nw : 2 + nw + n_out]
        xv = x_ref[...]
        h = (xv * _rstd(xv) * g_ref[...]).astype(BF16)
        if swiglu:
            silu_ref, dact_ref, act_ref = o_refs
            for j in range(N // nc):
                cols = slice(j * nc, (j + 1) * nc)
                gate = lax.dot_general(h, w_refs[0][cols, :], NT, preferred_element_type=F32)
                up = lax.dot_general(h, w_refs[1][cols, :], NT, preferred_element_type=F32)
                s = _sigmoid(gate)
                t = gate * s
                silu_ref[:, cols] = t.astype(BF16)
                act_ref[:, cols] = (t * up).astype(BF16)
                dact_ref[:, cols] = (up * (s + t - t * s)).astype(BF16)
            return
        for w_ref, o_ref in zip(w_refs, o_refs):
            for j in range(N // nc):
                cols = slice(j * nc, (j + 1) * nc)
                o_ref[:, cols] = lax.dot_general(h, w_ref[cols, :], NT, preferred_element_type=F32).astype(BF16)

    return _token_call(
        body,
        name,
        T // tt,
        [pl.BlockSpec((tt, D), lambda i: (i, 0)), _resident((1, D))] + [_resident((N, D))] * nw,
        [x, g, *wts],
        [pl.BlockSpec((tt, N), lambda i: (i, 0))] * n_out,
        [jax.ShapeDtypeStruct((T, N), BF16)] * n_out,
        [],
        comm,
    )


def matmul_rms_res(x, a, w, g, scale, tt, name, target=None):
    T, D = x.shape
    K = w.shape[0]
    loss = target is not None

    def body(*refs):
        if loss:
            x_ref, a_ref, w_ref, g_ref, t_ref, dy_ref, m_ref, sq_ref = refs

            @pl.when(pl.program_id(0) == 0)
            def _():
                sq_ref[...] = jnp.zeros_like(sq_ref)

        else:
            x_ref, a_ref, w_ref, g_ref, y_ref, m_ref = refs
        m = jnp.dot(a_ref[...], w_ref[...], preferred_element_type=F32)
        m_ref[...] = m.astype(BF16)
        y = x_ref[...] + scale * (m * _rstd(m) * g_ref[...])
        if loss:
            err = y - t_ref[...]
            dy_ref[...] = err * (1.0 / D)
            sq_ref[...] += _sum8(err * err)
        else:
            y_ref[...] = y

    row = pl.BlockSpec((tt, D), lambda i: (i, 0))
    outs, _ = _token_call(
        body,
        name,
        T // tt,
        [row, pl.BlockSpec((tt, K), lambda i: (i, 0)), _resident((K, D)), _resident((1, D))] + ([row] if loss else []),
        [x, a, w, g] + ([target] if loss else []),
        [row, row] + ([pl.BlockSpec((8, D), lambda i: (0, 0))] if loss else []),
        [jax.ShapeDtypeStruct((T, D), F32), jax.ShapeDtypeStruct((T, D), BF16)] + ([jax.ShapeDtypeStruct((8, D), F32)] if loss else []),
        [],
        None,
    )
    return outs


def rms_bwd_matmul_nt(dxo, m, g, w, scale, gate_up, tt, name, comm=None):
    T, D = dxo.shape
    K = w.shape[0]
    act = gate_up is not None
    ec = _tile(K, MXU_COLS_V7X if act else 1024, 128)

    def body(*refs):
        if act:
            dxo_ref, m_ref, g_ref, w_ref, silu_ref, dact_ref, dm_ref, dg_ref, dgate_ref, dup_ref = refs
        else:
            dxo_ref, m_ref, g_ref, w_ref, dm_ref, dg_ref, da_ref = refs

        @pl.when(pl.program_id(0) == 0)
        def _():
            dg_ref[...] = jnp.zeros_like(dg_ref)

        dy = scale * dxo_ref[...]
        mv = m_ref[...].astype(F32)
        r = _rstd(mv)
        mh = mv * r
        dg_ref[...] += _sum8(dy * mh)
        dmh = dy * g_ref[...]
        dm = (r * (dmh - mh * jnp.mean(dmh * mh, axis=-1, keepdims=True))).astype(BF16)
        dm_ref[...] = dm
        for j in range(K // ec):
            cols = slice(j * ec, (j + 1) * ec)
            da = lax.dot_general(dm, w_ref[cols, :], NT, preferred_element_type=F32)
            if act:
                dup_ref[:, cols] = (da * silu_ref[:, cols].astype(F32)).astype(BF16)
                dgate_ref[:, cols] = (da * dact_ref[:, cols].astype(F32)).astype(BF16)
            else:
                da_ref[:, cols] = da.astype(BF16)

    row = pl.BlockSpec((tt, D), lambda i: (i, 0))
    col = pl.BlockSpec((tt, K), lambda i: (i, 0))
    n_col = 2 if act else 1
    return _token_call(
        body,
        name,
        T // tt,
        [row, row, _resident((1, D)), _resident((K, D))] + ([col, col] if act else []),
        [dxo, m, g, w, *(gate_up or ())],
        [row, pl.BlockSpec((8, D), lambda i: (0, 0))] + [col] * n_col,
        [jax.ShapeDtypeStruct((T, D), BF16), jax.ShapeDtypeStruct((8, D), F32)] + [jax.ShapeDtypeStruct((T, K), BF16)] * n_col,
        [],
        comm,
    )


def matmul_rms_bwd(dxo, x, g, pairs, tt, name, comm=None):
    T, D = x.shape
    N = pairs[0][0].shape[1]
    npairs = len(pairs)

    def body(*refs):
        dxo_ref, x_ref, g_ref = refs[:3]
        dz_refs = refs[3 : 3 + npairs]
        wt_refs = refs[3 + npairs : 3 + 2 * npairs]
        dx_ref, h_ref, dg_ref = refs[3 + 2 * npairs :]

        @pl.when(pl.program_id(0) == 0)
        def _():
            dg_ref[...] = jnp.zeros_like(dg_ref)

        dh = None
        for dz_ref, wt_ref in zip(dz_refs, wt_refs):
            part = jnp.dot(dz_ref[...], wt_ref[...], preferred_element_type=F32)
            dh = part if dh is None else dh + part
        xv = x_ref[...]
        gv = g_ref[...]
        r = _rstd(xv)
        xh = xv * r
        h_ref[...] = (xh * gv).astype(BF16)
        dg_ref[...] += _sum8(dh * xh)
        dxh = dh * gv
        dx_ref[...] = dxo_ref[...] + r * (dxh - xh * jnp.mean(dxh * xh, axis=-1, keepdims=True))

    row = pl.BlockSpec((tt, D), lambda i: (i, 0))
    return _token_call(
        body,
        name,
        T // tt,
        [row, row, _resident((1, D))] + [pl.BlockSpec((tt, N), lambda i: (i, 0))] * npairs + [_resident((N, D))] * npairs,
        [dxo, x, g, *[p[0] for p in pairs], *[p[1] for p in pairs]],
        [row, row, pl.BlockSpec((8, D), lambda i: (0, 0))],
        [jax.ShapeDtypeStruct((T, D), F32), jax.ShapeDtypeStruct((T, D), BF16), jax.ShapeDtypeStruct((8, D), F32)],
        [],
        comm,
    )


def ffn_bwd(dxo, m, g_post, w_down, silu, dact, x, g_pre, w_gate_t, w_up_t, scale, tt, name, comm=None):
    T, D = x.shape
    F = w_down.shape[0]
    ec = _tile(F, MXU_COLS_V7X, 128)

    def body(dxo_ref, m_ref, gpost_ref, wd_ref, silu_ref, dact_ref, x_ref, gpre_ref, wg_ref, wu_ref, dm_ref, dgpost_ref, dgate_ref, dup_ref, dx_ref, h_ref, dgpre_ref):
        @pl.when(pl.program_id(0) == 0)
        def _():
            dgpost_ref[...] = jnp.zeros_like(dgpost_ref)
            dgpre_ref[...] = jnp.zeros_like(dgpre_ref)

        dxo_v = dxo_ref[...]
        dy = scale * dxo_v
        mv = m_ref[...].astype(F32)
        r = _rstd(mv)
        mh = mv * r
        dgpost_ref[...] += _sum8(dy * mh)
        dmh = dy * gpost_ref[...]
        dm = (r * (dmh - mh * jnp.mean(dmh * mh, axis=-1, keepdims=True))).astype(BF16)
        dm_ref[...] = dm
        for j in range(F // ec):
            cols = slice(j * ec, (j + 1) * ec)
            da = lax.dot_general(dm, wd_ref[cols, :], NT, preferred_element_type=F32)
            dup_ref[:, cols] = (da * silu_ref[:, cols].astype(F32)).astype(BF16)
            dgate_ref[:, cols] = (da * dact_ref[:, cols].astype(F32)).astype(BF16)
        dh = jnp.dot(dgate_ref[...], wg_ref[...], preferred_element_type=F32) + jnp.dot(dup_ref[...], wu_ref[...], preferred_element_type=F32)
        xv = x_ref[...]
        gv = gpre_ref[...]
        rx = _rstd(xv)
        xh = xv * rx
        h_ref[...] = (xh * gv).astype(BF16)
        dgpre_ref[...] += _sum8(dh * xh)
        dxh = dh * gv
        dx_ref[...] = dxo_v + rx * (dxh - xh * jnp.mean(dxh * xh, axis=-1, keepdims=True))

    row = pl.BlockSpec((tt, D), lambda i: (i, 0))
    col = pl.BlockSpec((tt, F), lambda i: (i, 0))
    vec = _resident((1, D))
    wgt = _resident((F, D))
    acc8 = pl.BlockSpec((8, D), lambda i: (0, 0))
    return _token_call(
        body,
        name,
        T // tt,
        [row, row, vec, wgt, col, col, row, vec, wgt, wgt],
        [dxo, m, g_post, w_down, silu, dact, x, g_pre, w_gate_t, w_up_t],
        [row, acc8, col, col, row, row, acc8],
        [
            jax.ShapeDtypeStruct((T, D), BF16),
            jax.ShapeDtypeStruct((8, D), F32),
            jax.ShapeDtypeStruct((T, F), BF16),
            jax.ShapeDtypeStruct((T, F), BF16),
            jax.ShapeDtypeStruct((T, D), F32),
            jax.ShapeDtypeStruct((T, D), BF16),
            jax.ShapeDtypeStruct((8, D), F32),
        ],
        [],
        comm,
    )


def wgrad_tn(a, b, tk, name, comm=None):
    T, M = a.shape
    N = b.shape[1]
    assert M * N * 4 <= WGRAD_ACC_BYTES, (M, N)
    nt = T // tk

    def body(a_ref, b_ref, o_ref, acc_ref):
        k = pl.program_id(0)

        @pl.when(k == 0)
        def _():
            acc_ref[...] = jnp.zeros_like(acc_ref)

        acc_ref[...] += lax.dot_general(a_ref[...], b_ref[...], TN, preferred_element_type=F32)

        @pl.when(k == nt - 1)
        def _():
            o_ref[...] = acc_ref[...].astype(BF16)

    (out,), got = _token_call(
        body,
        name,
        nt,
        [pl.BlockSpec((tk, M), lambda k: (k, 0)), pl.BlockSpec((tk, N), lambda k: (k, 0))],
        [a, b],
        [pl.BlockSpec((M, N), lambda k: (0, 0))],
        [jax.ShapeDtypeStruct((M, N), BF16)],
        [pltpu.VMEM((M, N), F32)],
        comm,
    )
    return (out, got) if comm else out


def _sgu_recompute(z_ref, lng_ref, lnb_ref, S):
    z = z_ref[...].astype(F32)
    zu, zv = z[:, :S], z[:, S:]
    pu, pv = _gelu_parts(zu), _gelu_parts(zv)
    u = zu * pu[2]
    v = zv * pv[2]
    vc = v - jnp.mean(v, axis=-1, keepdims=True)
    rstd = lax.rsqrt(jnp.mean(vc * vc, axis=-1, keepdims=True) + EPS)
    vhat = vc * rstd
    vln = (vhat * lng_ref[...] + lnb_ref[...]).astype(BF16)
    return zu, zv, pu, pv, u, vhat, rstd, vln


def sgu_mix_fwd(zpre, lng, lnb, ws, bias_full, tt, name, comm=None):
    T, S2 = zpre.shape
    S = S2 // 2
    dg = S // GROUPS

    def body(z_ref, lng_ref, lnb_ref, ws_ref, bias_ref, o_ref):
        _, _, _, _, u, _, _, vln = _sgu_recompute(z_ref, lng_ref, lnb_ref, S)
        for c in range(tt // CHUNK):
            rows = slice(c * CHUNK, (c + 1) * CHUNK)
            for gi in range(GROUPS):
                cols = slice(gi * dg, (gi + 1) * dg)
                mixed = jnp.dot(ws_ref[gi], vln[rows, cols], preferred_element_type=F32) + bias_ref[:, cols]
                o_ref[rows, cols] = (u[rows, cols] * mixed).astype(BF16)

    vec = pl.BlockSpec((1, S), lambda i: (0, 0))
    return _token_call(
        body,
        name,
        T // tt,
        [
            pl.BlockSpec((tt, S2), lambda i: (i, 0)),
            vec,
            vec,
            pl.BlockSpec((GROUPS, CHUNK, CHUNK), lambda i: (0, 0, 0)),
            pl.BlockSpec((CHUNK, S), lambda i: (0, 0)),
        ],
        [zpre, lng, lnb, ws, bias_full],
        [pl.BlockSpec((tt, S), lambda i: (i, 0))],
        [jax.ShapeDtypeStruct((T, S), BF16)],
        [],
        comm,
    )


def sgu_mix_bwd(dgated, zpre, lng, lnb, ws, ws_t, bias_full, tt, name, comm=None):
    T, S2 = zpre.shape
    S = S2 // 2
    dg = S // GROUPS

    def body(dgt_ref, z_ref, lng_ref, lnb_ref, ws_ref, wst_ref, bias_ref, dz_ref, dws_ref, dbias_ref, dlng_ref, dlnb_ref, du_ref, dvln_ref):
        @pl.when(pl.program_id(0) == 0)
        def _():
            dws_ref[...] = jnp.zeros_like(dws_ref)
            dbias_ref[...] = jnp.zeros_like(dbias_ref)
            dlng_ref[...] = jnp.zeros_like(dlng_ref)
            dlnb_ref[...] = jnp.zeros_like(dlnb_ref)

        zu, zv, pu, pv, u, vhat, rstd, vln = _sgu_recompute(z_ref, lng_ref, lnb_ref, S)
        for c in range(tt // CHUNK):
            rows = slice(c * CHUNK, (c + 1) * CHUNK)
            for gi in range(GROUPS):
                cols = slice(gi * dg, (gi + 1) * dg)
                v_cg = vln[rows, cols]
                mixed = jnp.dot(ws_ref[gi], v_cg, preferred_element_type=F32) + bias_ref[:, cols]
                dgt = dgt_ref[rows, cols].astype(F32)
                du_ref[rows, cols] = dgt * mixed
                dmx = dgt * u[rows, cols]
                dbias_ref[:, cols] += dmx
                dmx16 = dmx.astype(BF16)
                dws_ref[gi] += lax.dot_general(dmx16, v_cg, NT, preferred_element_type=F32)
                dvln_ref[rows, cols] = jnp.dot(wst_ref[gi], dmx16, preferred_element_type=F32)

        dvln = dvln_ref[...]
        dlng_ref[...] += _sum8(dvln * vhat)
        dlnb_ref[...] += _sum8(dvln)
        dvhat = dvln * lng_ref[...]
        dv = rstd * (dvhat - jnp.mean(dvhat, axis=-1, keepdims=True) - vhat * jnp.mean(dvhat * vhat, axis=-1, keepdims=True))
        dz_ref[:, :S] = (du_ref[...] * _gelu_grad(zu, pu)).astype(BF16)
        dz_ref[:, S:] = (dv * _gelu_grad(zv, pv)).astype(BF16)

    vec = pl.BlockSpec((1, S), lambda i: (0, 0))
    wsp = pl.BlockSpec((GROUPS, CHUNK, CHUNK), lambda i: (0, 0, 0))
    full = pl.BlockSpec((CHUNK, S), lambda i: (0, 0))
    acc8 = pl.BlockSpec((8, S), lambda i: (0, 0))
    return _token_call(
        body,
        name,
        T // tt,
        [pl.BlockSpec((tt, S), lambda i: (i, 0)), pl.BlockSpec((tt, S2), lambda i: (i, 0)), vec, vec, wsp, wsp, full],
        [dgated, zpre, lng, lnb, ws, ws_t, bias_full],
        [pl.BlockSpec((tt, S2), lambda i: (i, 0)), wsp, full, acc8, acc8],
        [
            jax.ShapeDtypeStruct((T, S2), BF16),
            jax.ShapeDtypeStruct((GROUPS, CHUNK, CHUNK), F32),
            jax.ShapeDtypeStruct((CHUNK, S), F32),
            jax.ShapeDtypeStruct((8, S), F32),
            jax.ShapeDtypeStruct((8, S), F32),
        ],
        [pltpu.VMEM((tt, S), F32), pltpu.VMEM((tt, S), F32)],
        comm,
    )


def _shifted_planes(sh, tt):
    n = tt + HALO - SUBLANES
    for s in range(1, SUBLANES):
        sh[s, 0:n, :] = sh[0, pl.ds(s, n), :]


def conv_mid_fwd(p, wdw, bdw, lng, lnb, tt, name, comm=None):
    T, C2 = p.shape
    C = C2 // 2

    def body(p_ref, w_ref, b_ref, lng_ref, lnb_ref, yc_ref, ys_ref, sh):
        @pl.when(pl.program_id(0) == 0)
        def _():
            sh[0, 0:HALO, :] = jnp.zeros((HALO, C), F32)

        @pl.when(pl.program_id(0) > 0)
        def _():
            sh[0, 0:HALO, :] = sh[0, tt : tt + HALO, :]

        pv = p_ref[...].astype(F32)
        sh[0, HALO:, :] = pv[:, :C] * _sigmoid(pv[:, C:])
        _shifted_planes(sh, tt)
        for lb in range(C // LANES):
            lanes = slice(lb * LANES, (lb + 1) * LANES)
            wk = [jnp.broadcast_to(w_ref[k : k + 1, lanes], (SUBLANES, LANES)) for k in range(CONV_W)]
            bias = jnp.broadcast_to(b_ref[:, lanes], (SUBLANES, LANES))

            def rows(i, carry):
                r0 = pl.multiple_of(i * CONV_ROWS, CONV_ROWS)
                accs = [[bias, jnp.zeros((SUBLANES, LANES), F32)] for _ in range(CONV_ROWS // SUBLANES)]
                for k in range(CONV_W):
                    o = HALO - (CONV_W - 1) + k
                    for j, acc in enumerate(accs):
                        blk = sh[o % SUBLANES, pl.ds(r0 + (o // SUBLANES + j) * SUBLANES, SUBLANES), lanes]
                        acc[k % 2] = acc[k % 2] + wk[k] * blk
                for j, acc in enumerate(accs):
                    yc_ref[pl.ds(r0 + j * SUBLANES, SUBLANES), lanes] = acc[0] + acc[1]
                return carry

            lax.fori_loop(0, tt // CONV_ROWS, rows, 0)
        acc = yc_ref[...]
        yc = acc - jnp.mean(acc, axis=-1, keepdims=True)
        yn = yc * lax.rsqrt(jnp.mean(yc * yc, axis=-1, keepdims=True) + EPS) * lng_ref[...] + lnb_ref[...]
        ys_ref[...] = (yn * _sigmoid(yn)).astype(BF16)

    vec = pl.BlockSpec((1, C), lambda i: (0, 0))
    row = pl.BlockSpec((tt, C), lambda i: (i, 0))
    return _token_call(
        body,
        name,
        T // tt,
        [pl.BlockSpec((tt, C2), lambda i: (i, 0)), pl.BlockSpec((HALO, C), lambda i: (0, 0)), vec, vec, vec],
        [p, wdw, bdw, lng, lnb],
        [row, row],
        [jax.ShapeDtypeStruct((T, C), F32), jax.ShapeDtypeStruct((T, C), BF16)],
        [pltpu.VMEM((SUBLANES, tt + HALO, C), F32)],
        comm,
    )


def conv_mid_bwd(dys, yc, p, wdw, lng, lnb, tt, name, comm=None):
    T, C2 = p.shape
    C = C2 // 2
    n = T // tt

    def body(dys_ref, yc_ref, p_ref, w_ref, lng_ref, lnb_ref, dp_ref, dw_ref, db_ref, dlng_ref, dlnb_ref, sh, y_s, dy_s):
        @pl.when(pl.program_id(0) == 0)
        def _():
            sh[0, tt : tt + HALO, :] = jnp.zeros((HALO, C), F32)
            dw_ref[...] = jnp.zeros_like(dw_ref)
            db_ref[...] = jnp.zeros_like(db_ref)
            dlng_ref[...] = jnp.zeros_like(dlng_ref)
            dlnb_ref[...] = jnp.zeros_like(dlnb_ref)

        @pl.when(pl.program_id(0) > 0)
        def _():
            sh[0, tt : tt + HALO, :] = sh[0, 0:HALO, :]

        ycv = yc_ref[...]
        ycc = ycv - jnp.mean(ycv, axis=-1, keepdims=True)
        rstd = lax.rsqrt(jnp.mean(ycc * ycc, axis=-1, keepdims=True) + EPS)
        yhat = ycc * rstd
        lng_v = lng_ref[...]
        yn = yhat * lng_v + lnb_ref[...]
        sg = _sigmoid(yn)
        dyn = dys_ref[...].astype(F32) * (sg * (1.0 + yn * (1.0 - sg)))
        dlng_ref[...] += _sum8(dyn * yhat)
        dlnb_ref[...] += _sum8(dyn)
        dyh = dyn * lng_v
        dyc = rstd * (dyh - jnp.mean(dyh, axis=-1, keepdims=True) - yhat * jnp.mean(dyh * yhat, axis=-1, keepdims=True))
        db_ref[...] += _sum8(dyc)
        sh[0, 0:tt, :] = dyc
        _shifted_planes(sh, tt)

        pv = p_ref[...].astype(F32)
        a = pv[:, :C]
        sgate = _sigmoid(pv[:, C:])
        y_s[...] = a * sgate
        for lb in range(C // LANES):
            lanes = slice(lb * LANES, (lb + 1) * LANES)
            for k0, k1 in ((0, CONV_W // 2), (CONV_W // 2, CONV_W)):
                wk = [jnp.broadcast_to(w_ref[k : k + 1, lanes], (SUBLANES, LANES)) for k in range(k0, k1)]

                def rows(i, dw_acc, k0=k0, k1=k1, wk=wk, lanes=lanes):
                    r0 = pl.multiple_of(i * CONV_ROWS, CONV_ROWS)
                    dw_acc = list(dw_acc)
                    zero = jnp.zeros((SUBLANES, LANES), F32)
                    ybs, accs = [], []
                    for j in range(CONV_ROWS // SUBLANES):
                        at = pl.ds(r0 + j * SUBLANES, SUBLANES)
                        ybs.append(y_s[at, lanes])
                        accs.append([zero if k0 == 0 else dy_s[at, lanes], zero])
                    for n_k, k in enumerate(range(k0, k1)):
                        o = CONV_W - 1 - k
                        for j, acc in enumerate(accs):
                            blk = sh[o % SUBLANES, pl.ds(r0 + (o // SUBLANES + j) * SUBLANES, SUBLANES), lanes]
                            acc[n_k % 2] = acc[n_k % 2] + wk[n_k] * blk
                            dw_acc[n_k] = dw_acc[n_k] + ybs[j] * blk
                    for j, acc in enumerate(accs):
                        dy_s[pl.ds(r0 + j * SUBLANES, SUBLANES), lanes] = acc[0] + acc[1]
                    return tuple(dw_acc)

                dw_acc = lax.fori_loop(0, tt // CONV_ROWS, rows, tuple(jnp.zeros((SUBLANES, LANES), F32) for _ in range(k0, k1)))
                for n_k, k in enumerate(range(k0, k1)):
                    dw_ref[SUBLANES * k : SUBLANES * (k + 1), lanes] += dw_acc[n_k]
        dy = dy_s[...]
        dp_ref[:, :C] = (dy * sgate).astype(BF16)
        dp_ref[:, C:] = (dy * a * sgate * (1.0 - sgate)).astype(BF16)

    vec = pl.BlockSpec((1, C), lambda i: (0, 0))
    row = pl.BlockSpec((tt, C), lambda i: (n - 1 - i, 0))
    row2 = pl.BlockSpec((tt, C2), lambda i: (n - 1 - i, 0))
    acc8 = pl.BlockSpec((8, C), lambda i: (0, 0))
    return _token_call(
        body,
        name,
        n,
        [row, row, row2, pl.BlockSpec((HALO, C), lambda i: (0, 0)), vec, vec],
        [dys, yc, p, wdw, lng, lnb],
        [row2, pl.BlockSpec((HALO * 8, C), lambda i: (0, 0)), acc8, acc8, acc8],
        [
            jax.ShapeDtypeStruct((T, C2), BF16),
            jax.ShapeDtypeStruct((HALO * 8, C), F32),
            jax.ShapeDtypeStruct((8, C), F32),
            jax.ShapeDtypeStruct((8, C), F32),
            jax.ShapeDtypeStruct((8, C), F32),
        ],
        [pltpu.VMEM((SUBLANES, tt + HALO, C), F32), pltpu.VMEM((tt, C), F32), pltpu.VMEM((tt, C), F32)],
        comm,
    )


def sum_slots(slots, name):
    _, r, w = slots.shape

    def body(s_ref, o_ref):
        total = s_ref[0].astype(F32)
        for p in range(1, N_DEV):
            total = total + s_ref[p].astype(F32)
        o_ref[...] = total

    return _pcall(body, name=name, out_shape=jax.ShapeDtypeStruct((r, w), F32), compiler_params=pltpu.CompilerParams(vmem_limit_bytes=VMEM_LIMIT_V7X))(slots)


def adamw(w, g, m, v, name):
    R, C = w.shape
    tr = _tile(R, 512, 8)
    c1 = 1.0 - ADAM_B1**ADAM_STEP
    c2 = 1.0 - ADAM_B2**ADAM_STEP

    def body(w_ref, g_ref, m_ref, v_ref, d_ref, mo_ref, vo_ref):
        gv = g_ref[...]
        m2 = ADAM_B1 * m_ref[...] + (1.0 - ADAM_B1) * gv
        v2 = ADAM_B2 * v_ref[...] + (1.0 - ADAM_B2) * (gv * gv)
        mo_ref[...] = m2
        vo_ref[...] = v2
        d_ref[...] = -ADAM_LR * ((m2 / c1) / (jnp.sqrt(v2 / c2) + ADAM_EPS) + ADAM_WD * w_ref[...])

    blk = pl.BlockSpec((tr, C), lambda i: (i, 0))
    return _pcall(
        body,
        name=name,
        grid=(R // tr,),
        in_specs=[blk] * 4,
        out_specs=[blk] * 3,
        out_shape=[jax.ShapeDtypeStruct((R, C), F32)] * 3,
        compiler_params=_cparams(1),
    )(w, g, m, v)


def _sublayers(depth):
    out = []
    for layer in range(depth):
        out.append(("ffn", layer, 0))
        out.append(("sgu" if layer % N_MIXERS == 0 else "conv", layer, layer // N_MIXERS))
        out.append(("ffn", layer, 1))
    return out


def _row_blocks(sub, shards):
    kind, layer, idx = sub
    if kind == "ffn":
        blk = {"gate_t": shards["ff_w_gate"][layer, idx].T, "up_t": shards["ff_w_up"][layer, idx].T, "down": shards["ff_w_down"][layer, idx]}
    elif kind == "sgu":
        blk = {"in_t": shards["sgu_w_in"][idx].T, "out": shards["sgu_w_out"][idx]}
    else:
        blk = {"pw1_t": shards["conv_w_pw1"][idx].T, "pw2": shards["conv_w_pw2"][idx]}
    return {nm: arr.astype(BF16) for nm, arr in blk.items()}


def _pad8(a):
    return jnp.pad(a, ((0, (-a.shape[0]) % 8), (0, 0)))


def kernel(x, norm_g, ff_w_gate, ff_w_up, ff_w_down, sgu_w_in, sgu_ln_g, sgu_ln_b, sgu_w_spatial, sgu_b_spatial, sgu_w_out, conv_w_pw1, conv_w_dw, conv_b_dw, conv_ln_g, conv_ln_b, conv_w_pw2, loss_target, m_norm_g, m_ff_w_gate, m_ff_w_up, m_ff_w_down, m_sgu_w_in, m_sgu_ln_g, m_sgu_ln_b, m_sgu_w_spatial, m_sgu_b_spatial, m_sgu_w_out, m_conv_w_pw1, m_conv_w_dw, m_conv_b_dw, m_conv_ln_g, m_conv_ln_b, m_conv_w_pw2, v_norm_g, v_ff_w_gate, v_ff_w_up, v_ff_w_down, v_sgu_w_in, v_sgu_ln_g, v_sgu_ln_b, v_sgu_w_spatial, v_sgu_b_spatial, v_sgu_w_out, v_conv_w_pw1, v_conv_w_dw, v_conv_b_dw, v_conv_ln_g, v_conv_ln_b, v_conv_w_pw2):
    names = ["norm_g", "ff_w_gate", "ff_w_up", "ff_w_down", "sgu_w_in", "sgu_ln_g", "sgu_ln_b", "sgu_w_spatial", "sgu_b_spatial", "sgu_w_out", "conv_w_pw1", "conv_w_dw", "conv_b_dw", "conv_ln_g", "conv_ln_b", "conv_w_pw2"]
    weights = dict(zip(names, [norm_g, ff_w_gate, ff_w_up, ff_w_down, sgu_w_in, sgu_ln_g, sgu_ln_b, sgu_w_spatial, sgu_b_spatial, sgu_w_out, conv_w_pw1, conv_w_dw, conv_b_dw, conv_ln_g, conv_ln_b, conv_w_pw2]))
    moments_m = dict(zip(names, [m_norm_g, m_ff_w_gate, m_ff_w_up, m_ff_w_down, m_sgu_w_in, m_sgu_ln_g, m_sgu_ln_b, m_sgu_w_spatial, m_sgu_b_spatial, m_sgu_w_out, m_conv_w_pw1, m_conv_w_dw, m_conv_b_dw, m_conv_ln_g, m_conv_ln_b, m_conv_w_pw2]))
    moments_v = dict(zip(names, [v_norm_g, v_ff_w_gate, v_ff_w_up, v_ff_w_down, v_sgu_w_in, v_sgu_ln_g, v_sgu_ln_b, v_sgu_w_spatial, v_sgu_b_spatial, v_sgu_w_out, v_conv_w_pw1, v_conv_w_dw, v_conv_b_dw, v_conv_ln_g, v_conv_ln_b, v_conv_w_pw2]))

    _, T, D = x.shape
    depth = norm_g.shape[0]
    n_conv = conv_w_dw.shape[0]
    n_sgu = sgu_w_in.shape[0]
    S = sgu_ln_g.shape[1]
    lanes = norm_g.shape[2]
    subs = _sublayers(depth)
    n_sub = len(subs)

    cx, cy, cc = (lax.axis_index(a) for a in MESH_AXES)
    my_block = 4 * cx + 2 * cy + cc

    blocks = [_row_blocks(sub, weights) for sub in subs]
    dw_pad = jnp.pad(conv_w_dw, ((0, 0), (0, HALO - CONV_W), (0, 0)))
    small_parts = [_pad8(p) for p in (norm_g.reshape(-1, lanes), dw_pad.reshape(-1, lanes), conv_b_dw, conv_ln_g, conv_ln_b)]
    small_rows = [p.shape[0] for p in small_parts]
    small = jnp.concatenate(small_parts, axis=0)

    def gathered(block_names, outs):
        return {nm: o.reshape(N_DEV * o.shape[1], o.shape[2]) for nm, o in zip(block_names, outs)}

    first_names = [nm for nm in blocks[0] if nm != "down"]
    first = gather_two_level([blocks[0][nm] for nm in first_names] + [small], name="gather_first")
    W = [None] * n_sub
    W[0] = gathered(first_names, first[:-1])
    gsmall = first[-1]

    small_full = jnp.transpose(gsmall, (1, 0, 2)).reshape(gsmall.shape[1], N_DEV * lanes)
    so = [0]
    for r in small_rows:
        so.append(so[-1] + r)
    norm_full = small_full[so[0] : so[0] + depth * norm_g.shape[1]].reshape(depth, -1, D)
    dw_full = small_full[so[1] : so[1] + n_conv * HALO].reshape(n_conv, HALO, D)
    bdw_full, clng_full, clnb_full = (small_full[so[k] : so[k] + n_conv] for k in (2, 3, 4))

    causal = jnp.tril(jnp.ones((CHUNK, CHUNK), dtype=bool))
    ws_all = jnp.where(causal[None, None], sgu_w_spatial, 0.0).astype(BF16)
    wst_all = jnp.swapaxes(ws_all, -1, -2)
    bias_full_all = jnp.repeat(jnp.swapaxes(sgu_b_spatial, -1, -2), S // GROUPS, axis=-1)

    tt = _tile(T, 512, CHUNK)
    tt_mix = _tile(T, 256, CHUNK)

    tk = _tile(T, 1024, CHUNK)

    def vec(v):
        return v.reshape(1, -1)

    def split_first(comm):
        return (comm[:1], comm[1:]) if comm else (None, None)

    def norms(kind, layer, idx):
        pre = 4 * idx if kind == "ffn" else 2
        return vec(norm_full[layer, pre]), vec(norm_full[layer, pre + 1])

    xs = x[0]
    saved = []
    for si, (kind, layer, idx) in enumerate(subs):
        w = W[si]
        g_pre, g_post = norms(kind, layer, idx)
        nxt = [("gather", a) for a in blocks[si + 1].values()] if si + 1 < n_sub else None
        nxt_first, nxt_rest = split_first(nxt)
        if kind == "ffn":
            own = [("gather", blocks[0]["down"])] if si == 0 else []
            (silu, dact, a), got = rms_matmul_nt(xs, g_pre, [w["gate_t"], w["up_t"]], tt, "ffn_in", own + (nxt or []), swiglu=True)
            if own:
                w.update(gathered(["down"], got[:1]))
                got = got[1:]
            if si == n_sub - 1:
                x_new, o, sq = matmul_rms_res(xs, a, w["down"], g_post, FFN_SCALE, tt, "ffn_out_loss", loss_target[0])
            else:
                x_new, o = matmul_rms_res(xs, a, w["down"], g_post, FFN_SCALE, tk, "ffn_out")
            saved.append((xs, silu, dact, o, a))
        elif kind == "sgu":
            (zpre,), got = rms_matmul_nt(xs, g_pre, [w["in_t"]], tt, "sgu_in", nxt_first)
            (gated,), got_rest = sgu_mix_fwd(zpre, vec(sgu_ln_g[idx]), vec(sgu_ln_b[idx]), ws_all[idx], bias_full_all[idx], tt_mix, "sgu_mix", nxt_rest)
            got = list(got) + list(got_rest)
            x_new, mm = matmul_rms_res(xs, gated, w["out"], g_post, 1.0, tk, "sgu_out")
            saved.append((xs, zpre, gated, mm))
        else:
            (p,), got = rms_matmul_nt(xs, g_pre, [w["pw1_t"]], tt, "conv_pw1", nxt_first)
            (yc, ys), got_rest = conv_mid_fwd(p, dw_full[idx], vec(bdw_full[idx]), vec(clng_full[idx]), vec(clnb_full[idx]), tt, "conv_mid", nxt_rest)
            got = list(got) + list(got_rest)
            x_new, mm = matmul_rms_res(xs, ys, w["pw2"], g_post, 1.0, tk, "conv_pw2")
            saved.append((xs, p, yc, ys, mm))
        if nxt:
            W[si + 1] = gathered(blocks[si + 1].keys(), got)
        xs = x_new

    dx = xs
    loss = lax.psum(0.5 * jnp.sum(sq) / D, MESH_AXES)

    d_norm = [[None] * norm_full.shape[1] for _ in range(depth)]
    d_sgu = [None] * n_sgu
    d_conv = [None] * n_conv
    slots = [None] * n_sub
    pending = None

    def scatter_of(p):
        return [("scatter", dwm.reshape(N_DEV, dwm.shape[0] // N_DEV, D)) for dwm in p[1].values()] if p else None

    def pack_small():
        def sum8(v):
            return jnp.sum(v, axis=0)

        g_norm = jnp.stack([jnp.stack([sum8(d) for d in row]) for row in d_norm])
        g_dw = jnp.stack([jnp.sum(d[0].reshape(HALO, 8, D), axis=1) for d in d_conv])
        g_bdw, g_clng, g_clnb = (jnp.stack([sum8(d[k]) for d in d_conv]) for k in (1, 2, 3))
        g_slng, g_slnb = (jnp.stack([sum8(d[k]) for d in d_sgu]) for k in (2, 3))
        g_bsp = jnp.stack([jnp.sum(d[1].reshape(CHUNK, GROUPS, S // GROUPS), axis=-1).T for d in d_sgu])
        g_wsp = jnp.stack([jnp.where(causal[None], d[0], 0.0) for d in d_sgu])
        parts = [g_norm, g_dw, g_bdw, g_clng, g_clnb, g_slng, g_slnb, g_bsp, g_wsp]
        return parts, [p.size // D for p in parts], jnp.concatenate([_pad8(p.reshape(-1, D)) for p in parts], axis=0)

    for si in reversed(range(n_sub)):
        kind, layer, idx = subs[si]
        w = W[si]
        g_pre, g_post = norms(kind, layer, idx)
        pre = 4 * idx if kind == "ffn" else 2
        sc_first, sc_rest = split_first(scatter_of(pending))
        if kind == "ffn":
            xs_in, silu, dact, o, a = saved[si]
            (do, dg_post, dgate, dup, dx, h, dg_pre), got = ffn_bwd(
                dx, o, g_post, w["down"], silu, dact, xs_in, g_pre, w["gate_t"], w["up_t"], FFN_SCALE, tt_mix, "ffn_bwd", scatter_of(pending)
            )
            if si == 0:
                d_norm[layer][pre], d_norm[layer][pre + 1] = dg_pre, dg_post
                sparts, srows, sgrad = pack_small()
                dw_down, got_small = wgrad_tn(a, do, tk, "wgrad_ffn_out", [("gather", sgrad)])
                dw_up, got_down = wgrad_tn(dup, h, tk, "wgrad_ffn_in", scatter_of((si, {"down": dw_down})))
                dw_gate, got_up = wgrad_tn(dgate, h, tk, "wgrad_ffn_in", scatter_of((si, {"up_t": dw_up})))
            else:
                dw_down = wgrad_tn(a, do, tk, name="wgrad_ffn_out")
                dw_up = wgrad_tn(dup, h, tk, name="wgrad_ffn_in")
                dw_gate = wgrad_tn(dgate, h, tk, name="wgrad_ffn_in")
            dws_now = {"gate_t": dw_gate, "up_t": dw_up, "down": dw_down}
        elif kind == "sgu":
            xs_in, zpre, gated, mm = saved[si]
            (dm, dg_post, dgated), got = rms_bwd_matmul_nt(dx, mm, g_post, w["out"], 1.0, None, tt, "sgu_out_bwd", sc_first)
            (dzpre, dws, dbias, dlng, dlnb), got_rest = sgu_mix_bwd(dgated, zpre, vec(sgu_ln_g[idx]), vec(sgu_ln_b[idx]), ws_all[idx], wst_all[idx], bias_full_all[idx], tt_mix, "sgu_mix_bwd", sc_rest)
            got = list(got) + list(got_rest)
            (dx, h, dg_pre), _ = matmul_rms_bwd(dx, xs_in, g_pre, [(dzpre, w["in_t"])], tt, "sgu_in_bwd")
            dws_now = {"in_t": wgrad_tn(dzpre, h, tt, name="wgrad_sgu_in"), "out": wgrad_tn(gated, dm, tk, name="wgrad_sgu_out")}
            d_sgu[idx] = (dws, dbias, dlng, dlnb)
        else:
            xs_in, p, yc, ys, mm = saved[si]
            (dm, dg_post, dys), got = rms_bwd_matmul_nt(dx, mm, g_post, w["pw2"], 1.0, None, tt, "conv_pw2_bwd", sc_first)
            (dp, dwdw, dbdw, dlng, dlnb), got_rest = conv_mid_bwd(dys, yc, p, dw_full[idx], vec(clng_full[idx]), vec(clnb_full[idx]), tt, "conv_mid_bwd", sc_rest)
            got = list(got) + list(got_rest)
            (dx, h, dg_pre), _ = matmul_rms_bwd(dx, xs_in, g_pre, [(dp, w["pw1_t"])], tt, "conv_pw1_bwd")
            dws_now = {"pw1_t": wgrad_tn(dp, h, tk, name="wgrad_conv_pw1"), "pw2": wgrad_tn(ys, dm, tk, name="wgrad_conv_pw2")}
            d_conv[idx] = (dwdw, dbdw, dlng, dlnb)
        d_norm[layer][pre], d_norm[layer][pre + 1] = dg_pre, dg_post
        if pending:
            slots[pending[0]] = dict(zip(pending[1].keys(), got))
        pending = (si, dws_now)
    grad_x = dx[None]

    last = exchange(scatter_of((0, {"gate_t": pending[1]["gate_t"]})), name="scatter_last")
    slots[0] = {"gate_t": last[0], "up_t": got_up[0], "down": got_down[0]}
    stotal = sum_slots(got_small[0], name="sum_slots")

    gs = {(si, nm): sum_slots(s, name="sum_slots") for si in range(n_sub) for nm, s in slots[si].items()}
    grads = {}
    ffn_si = {(layer, idx): si for si, (kind, layer, idx) in enumerate(subs) if kind == "ffn"}
    sgu_si = {idx: si for si, (kind, layer, idx) in enumerate(subs) if kind == "sgu"}
    conv_si = {idx: si for si, (kind, layer, idx) in enumerate(subs) if kind == "conv"}
    grads["ff_w_gate"] = jnp.stack([jnp.stack([gs[(ffn_si[(l, f)], "gate_t")].T for f in range(2)]) for l in range(depth)])
    grads["ff_w_up"] = jnp.stack([jnp.stack([gs[(ffn_si[(l, f)], "up_t")].T for f in range(2)]) for l in range(depth)])
    grads["ff_w_down"] = jnp.stack([jnp.stack([gs[(ffn_si[(l, f)], "down")] for f in range(2)]) for l in range(depth)])
    grads["sgu_w_in"] = jnp.stack([gs[(sgu_si[j], "in_t")].T for j in range(n_sgu)])
    grads["sgu_w_out"] = jnp.stack([gs[(sgu_si[j], "out")] for j in range(n_sgu)])
    grads["conv_w_pw1"] = jnp.stack([gs[(conv_si[j], "pw1_t")].T for j in range(n_conv)])
    grads["conv_w_pw2"] = jnp.stack([gs[(conv_si[j], "pw2")] for j in range(n_conv)])

    so = [0]
    for r in srows:
        so.append(so[-1] + r + (-r) % 8)
    sp = [stotal[so[k] : so[k] + srows[k]].reshape(sparts[k].shape) for k in range(len(sparts))]

    def my_lanes(v):
        return lax.dynamic_slice_in_dim(v, my_block * lanes, lanes, axis=-1)

    grads["norm_g"] = my_lanes(sp[0])
    grads["conv_w_dw"] = my_lanes(sp[1])[:, :CONV_W]
    grads["conv_b_dw"] = my_lanes(sp[2])
    grads["conv_ln_g"] = my_lanes(sp[3])
    grads["conv_ln_b"] = my_lanes(sp[4])
    grads["sgu_ln_g"], grads["sgu_ln_b"], grads["sgu_b_spatial"], grads["sgu_w_spatial"] = sp[5], sp[6], sp[7], sp[8]

    deltas, new_m, new_v = {}, {}, {}
    for nm in names:
        w = weights[nm]
        two_d = (-1, w.shape[-1])
        d, m2, v2 = adamw(w.reshape(two_d), grads[nm].reshape(two_d), moments_m[nm].reshape(two_d), moments_v[nm].reshape(two_d), name="adamw")
        deltas[nm], new_m[nm], new_v[nm] = d.reshape(w.shape), m2.reshape(w.shape), v2.reshape(w.shape)

    return (loss, grad_x, *[grads[n] for n in names], *[deltas[n] for n in names], *[new_m[n] for n in names], *[new_v[n] for n in names])
```

```python
import jax
import jax.numpy as jnp
from jax import lax
from jax.experimental import pallas as pl
from jax.experimental.pallas import tpu as pltpu

F32 = jnp.float32
BF16 = jnp.bfloat16

EPS = 1e-6
FFN_SCALE = 0.5
N_MIXERS = 2
CHUNK = 128
GROUPS = 8
CONV_W = 31
HALO = 32
GELU_C0 = 0.7978845608028654
GELU_C1 = 0.044715
ADAM_LR, ADAM_B1, ADAM_B2, ADAM_EPS, ADAM_WD, ADAM_STEP = 0.001, 0.9, 0.999, 1e-08, 0.01, 10

MESH_AXES = ("x", "y", "c")
N_DEV = 8
VMEM_LIMIT_V7X = 56 * 1024 * 1024
WGRAD_ACC_BYTES = 16 * 1024 * 1024
MXU_COLS_V7X = 256
SUBLANES, LANES = 8, 128
CONV_ROWS = 32

NT = (((1,), (1,)), ((), ()))
TN = (((0,), (0,)), ((), ()))

HBM_SPEC = pl.BlockSpec(memory_space=pl.ANY)
MESH_ID = pl.DeviceIdType.MESH


def _pcall(body, **kw):
    return pl.pallas_call(body, **kw)


def _cparams(n_axes):
    return pltpu.CompilerParams(dimension_semantics=("arbitrary",) * n_axes, vmem_limit_bytes=VMEM_LIMIT_V7X)


def _tile(n, pref, align):
    if n <= pref:
        return n
    t = (pref // align) * align
    while t > align and n % t:
        t -= align
    assert n % t == 0, (n, pref, align)
    return t


def _rstd(x):
    return lax.rsqrt(jnp.mean(x * x, axis=-1, keepdims=True) + EPS)


def _sum8(v):
    t, d = v.shape
    return v.reshape(t // 8, 8, d).sum(axis=0)


def _sigmoid(x):
    return jax.nn.sigmoid(x)


def _gelu_parts(z):
    zz = z * z
    t = jnp.tanh(z * (GELU_C0 + (GELU_C0 * GELU_C1) * zz))
    return zz, t, 0.5 * t + 0.5


def _gelu_grad(z, parts):
    zz, t, cdf = parts
    return cdf + z * (1.0 - t * t) * ((0.5 * GELU_C0) + (1.5 * GELU_C0 * GELU_C1) * zz)


def _resident(shape):
    return pl.BlockSpec(shape, lambda i: (0,) * len(shape), pipeline_mode=pl.Buffered(1))


def _exchange_io(comm):
    n = len(comm)
    out_shape = [jax.ShapeDtypeStruct(((N_DEV,) + a.shape) if kind == "gather" else a.shape, a.dtype) for kind, a in comm]
    scratch = [pltpu.SemaphoreType.DMA((n, N_DEV - 1)), pltpu.SemaphoreType.DMA((n, N_DEV - 1)), pltpu.SemaphoreType.DMA((n,))]
    return [HBM_SPEC] * n, [HBM_SPEC] * n, out_shape, scratch


def _exchange_copies(kinds, in_refs, out_refs, send_sems, recv_sems, local_sems, with_arrivals=True):
    x, y, c = (lax.axis_index(a) for a in MESH_AXES)
    me = 4 * x + 2 * y + c
    local, sends, arrivals = [], [], []
    for a, kind in enumerate(kinds):
        src, dst = in_refs[a], out_refs[a]
        gather = kind == "gather"
        local.append(pltpu.make_async_copy(src if gather else src.at[me], dst.at[me], local_sems.at[a]))
        for k in range(N_DEV - 1):
            mask = k + 1
            px = 1 - x if mask & 4 else x
            py = 1 - y if mask & 2 else y
            pc = 1 - c if mask & 1 else c
            peer = 4 * px + 2 * py + pc
            block = src if gather else src.at[peer]
            for into, lst in ((me, sends), (peer, arrivals)):
                if lst is arrivals and not with_arrivals:
                    continue
                lst.append(
                    pltpu.make_async_remote_copy(
                        src_ref=block, dst_ref=dst.at[into], send_sem=send_sems.at[a, k], recv_sem=recv_sems.at[a, k], device_id=(px, py, pc), device_id_type=MESH_ID
                    )
                )
    return local, sends, arrivals


def _exchange_start(copies):
    local, sends, _ = copies
    for cp in local + sends:
        cp.start()


def _exchange_finish(copies):
    local, sends, arrivals = copies
    for cp in arrivals:
        cp.wait_recv()
    for cp in sends:
        cp.wait_send()
    for cp in local:
        cp.wait()


def exchange(comm, name):
    kinds = [k for k, _ in comm]
    n = len(comm)
    in_specs, out_specs, out_shape, scratch = _exchange_io(comm)

    def body(*refs):
        copies = _exchange_copies(kinds, refs[:n], refs[n : 2 * n], *refs[2 * n :])
        _exchange_start(copies)
        _exchange_finish(copies)

    return _pcall(body, name=name, in_specs=in_specs, out_specs=out_specs, out_shape=out_shape, scratch_shapes=scratch)(*[a for _, a in comm])


def gather_two_level(packs, name):
    n = len(packs)

    def body(*refs):
        in_refs, out_refs = refs[:n], refs[n : 2 * n]
        send_sems, recv_sems, local_sems = refs[2 * n :]
        x, y, c = (lax.axis_index(a) for a in MESH_AXES)
        chips = [(1 - x, y), (x, 1 - y), (1 - x, 1 - y)]
        me, sibling = (x, y, c), (x, y, 1 - c)

        def slot(a, px, py, pc):
            return out_refs[a].at[4 * px + 2 * py + pc]

        def copy(a, k, block, to, from_input=False):
            return pltpu.make_async_remote_copy(
                src_ref=in_refs[a] if from_input else slot(a, *block),
                dst_ref=slot(a, *block),
                send_sem=send_sems.at[a, k],
                recv_sem=recv_sems.at[a, k],
                device_id=to,
                device_id_type=MESH_ID,
            )

        mine = [pltpu.make_async_copy(in_refs[a], slot(a, *me), local_sems.at[a]) for a in range(n)]
        for cp in mine:
            cp.start()
        first = []
        for a in range(n):
            first.append(copy(a, 0, me, sibling, from_input=True))
            first += [copy(a, 1 + j, me, (*chip, c), from_input=True) for j, chip in enumerate(chips)]
        for cp in first:
            cp.start()
        passed = []
        for j, chip in enumerate(chips):
            for a in range(n):
                copy(a, 1 + j, (*chip, c), me).wait_recv()
                fwd = copy(a, 4 + j, (*chip, c), sibling)
                fwd.start()
                passed.append(fwd)
        for a in range(n):
            copy(a, 0, sibling, me).wait_recv()
            for j, chip in enumerate(chips):
                copy(a, 4 + j, (*chip, 1 - c), me).wait_recv()
        for cp in first + passed:
            cp.wait_send()
        for cp in mine:
            cp.wait()

    return _pcall(
        body,
        name=name,
        in_specs=[HBM_SPEC] * n,
        out_specs=[HBM_SPEC] * n,
        out_shape=[jax.ShapeDtypeStruct((N_DEV,) + p.shape, p.dtype) for p in packs],
        scratch_shapes=[pltpu.SemaphoreType.DMA((n, N_DEV - 1)), pltpu.SemaphoreType.DMA((n, N_DEV - 1)), pltpu.SemaphoreType.DMA((n,))],
    )(*packs)


def _token_call(body, name, n_tiles, in_specs, inputs, out_specs, out_shape, scratch, comm):
    n_in, n_out, n_scr = len(inputs), len(out_shape), len(scratch)
    comm = comm or []
    nc = len(comm)
    full_body = body
    if nc:
        kinds = [k for k, _ in comm]
        c_in, c_out, c_shape, c_scr = _exchange_io(comm)
        in_specs, out_specs, out_shape, scratch = in_specs + c_in, out_specs + c_out, out_shape + c_shape, scratch + c_scr

        def full_body(*refs):
            ins, cins = refs[:n_in], refs[n_in : n_in + nc]
            o0 = n_in + nc
            outs, couts = refs[o0 : o0 + n_out], refs[o0 + n_out : o0 + n_out + nc]
            s0 = o0 + n_out + nc
            scr, sems = refs[s0 : s0 + n_scr], refs[s0 + n_scr :]

            @pl.when(pl.program_id(0) == 0)
            def _():
                _exchange_start(_exchange_copies(kinds, cins, couts, *sems, with_arrivals=False))

            body(*ins, *outs, *scr)

            @pl.when(pl.program_id(0) == n_tiles - 1)
            def _():
                _exchange_finish(_exchange_copies(kinds, cins, couts, *sems))

    res = _pcall(
        full_body, name=name, grid=(n_tiles,), in_specs=in_specs, out_specs=out_specs, out_shape=out_shape, scratch_shapes=scratch, compiler_params=_cparams(1)
    )(*inputs, *[a for _, a in comm])
    return res[:n_out], res[n_out:]


def rms_matmul_nt(x, g, wts, tt, name, comm=None, swiglu=False):
    T, D = x.shape
    N = wts[0].shape[0]
    nw = len(wts)
    n_out = 3 if swiglu else nw
    nc = _tile(N, MXU_COLS_V7X if swiglu else 1408, 128)

    def body(*refs):
        x_ref, g_ref = refs[0], refs[1]
        w_refs, o_refs = refs[2 : 2 + nw], refs[2 + nw : 2 + nw + n_out]
        xv = x_ref[...]
        h = (xv * _rstd(xv) * g_ref[...]).astype(BF16)
        if swiglu:
            silu_ref, dact_ref, act_ref = o_refs
            for j in range(N // nc):
                cols = slice(j * nc, (j + 1) * nc)
                gate = lax.dot_general(h, w_refs[0][cols, :], NT, preferred_element_type=F32)
                up = lax.dot_general(h, w_refs[1][cols, :], NT, preferred_element_type=F32)
                s = _sigmoid(gate)
                t = gate * s
                silu_ref[:, cols] = t.astype(BF16)
                act_ref[:, cols] = (t * up).astype(BF16)
                dact_ref[:, cols] = (up * (s + t - t * s)).astype(BF16)
            return
        for w_ref, o_ref in zip(w_refs, o_refs):
            for j in range(N // nc):
                cols = slice(j * nc, (j + 1) * nc)
                o_ref[:, cols] = lax.dot_general(h, w_ref[cols, :], NT, preferred_element_type=F32).astype(BF16)

    return _token_call(
        body,
        name,
        T // tt,
        [pl.BlockSpec((tt, D), lambda i: (i, 0)), _resident((1, D))] + [_resident((N, D))] * nw,
        [x, g, *wts],
        [pl.BlockSpec((tt, N), lambda i: (i, 0))] * n_out,
        [jax.ShapeDtypeStruct((T, N), BF16)] * n_out,
        [],
        comm,
    )


def matmul_rms_res(x, a, w, g, scale, tt, name, target=None):
    T, D = x.shape
    K = w.shape[0]
    loss = target is not None

    def body(*refs):
        if loss:
            x_ref, a_ref, w_ref, g_ref, t_ref, dy_ref, m_ref, sq_ref = refs

            @pl.when(pl.program_id(0) == 0)
            def _():
                sq_ref[...] = jnp.zeros_like(sq_ref)

        else:
            x_ref, a_ref, w_ref, g_ref, y_ref, m_ref = refs
        m = jnp.dot(a_ref[...], w_ref[...], preferred_element_type=F32)
        m_ref[...] = m.astype(BF16)
        y = x_ref[...] + scale * (m * _rstd(m) * g_ref[...])
        if loss:
            err = y - t_ref[...]
            dy_ref[...] = err * (1.0 / D)
            sq_ref[...] += _sum8(err * err)
        else:
            y_ref[...] = y

    row = pl.BlockSpec((tt, D), lambda i: (i, 0))
    outs, _ = _token_call(
        body,
        name,
        T // tt,
        [row, pl.BlockSpec((tt, K), lambda i: (i, 0)), _resident((K, D)), _resident((1, D))] + ([row] if loss else []),
        [x, a, w, g] + ([target] if loss else []),
        [row, row] + ([pl.BlockSpec((8, D), lambda i: (0, 0))] if loss else []),
        [jax.ShapeDtypeStruct((T, D), F32), jax.ShapeDtypeStruct((T, D), BF16)] + ([jax.ShapeDtypeStruct((8, D), F32)] if loss else []),
        [],
        None,
    )
    return outs


def rms_bwd_matmul_nt(dxo, m, g, w, scale, gate_up, tt, name, comm=None):
    T, D = dxo.shape
    K = w.shape[0]
    act = gate_up is not None
    ec = _tile(K, MXU_COLS_V7X if act else 1024, 128)

    def body(*refs):
        if act:
            dxo_ref, m_ref, g_ref, w_ref, silu_ref, dact_ref, dm_ref, dg_ref, dgate_ref, dup_ref = refs
        else:
            dxo_ref, m_ref, g_ref, w_ref, dm_ref, dg_ref, da_ref = refs

        @pl.when(pl.program_id(0) == 0)
        def _():
            dg_ref[...] = jnp.zeros_like(dg_ref)

        dy = scale * dxo_ref[...]
        mv = m_ref[...].astype(F32)
        r = _rstd(mv)
        mh = mv * r
        dg_ref[...] += _sum8(dy * mh)
        dmh = dy * g_ref[...]
        dm = (r * (dmh - mh * jnp.mean(dmh * mh, axis=-1, keepdims=True))).astype(BF16)
        dm_ref[...] = dm
        for j in range(K // ec):
            cols = slice(j * ec, (j + 1) * ec)
            da = lax.dot_general(dm, w_ref[cols, :], NT, preferred_element_type=F32)
            if act:
                dup_ref[:, cols] = (da * silu_ref[:, cols].astype(F32)).astype(BF16)
                dgate_ref[:, cols] = (da * dact_ref[:, cols].astype(F32)).astype(BF16)
            else:
                da_ref[:, cols] = da.astype(BF16)

    row = pl.BlockSpec((tt, D), lambda i: (i, 0))
    col = pl.BlockSpec((tt, K), lambda i: (i, 0))
    n_col = 2 if act else 1
    return _token_call(
        body,
        name,
        T // tt,
        [row, row, _resident((1, D)), _resident((K, D))] + ([col, col] if act else []),
        [dxo, m, g, w, *(gate_up or ())],
        [row, pl.BlockSpec((8, D), lambda i: (0, 0))] + [col] * n_col,
        [jax.ShapeDtypeStruct((T, D), BF16), jax.ShapeDtypeStruct((8, D), F32)] + [jax.ShapeDtypeStruct((T, K), BF16)] * n_col,
        [],
        comm,
    )


def matmul_rms_bwd(dxo, x, g, pairs, tt, name, comm=None):
    T, D = x.shape
    N = pairs[0][0].shape[1]
    npairs = len(pairs)

    def body(*refs):
        dxo_ref, x_ref, g_ref = refs[:3]
        dz_refs = refs[3 : 3 + npairs]
        wt_refs = refs[3 + npairs : 3 + 2 * npairs]
        dx_ref, h_ref, dg_ref = refs[3 + 2 * npairs :]

        @pl.when(pl.program_id(0) == 0)
        def _():
            dg_ref[...] = jnp.zeros_like(dg_ref)

        dh = None
        for dz_ref, wt_ref in zip(dz_refs, wt_refs):
            part = jnp.dot(dz_ref[...], wt_ref[...], preferred_element_type=F32)
            dh = part if dh is None else dh + part
        xv = x_ref[...]
        gv = g_ref[...]
        r = _rstd(xv)
        xh = xv * r
        h_ref[...] = (xh * gv).astype(BF16)
        dg_ref[...] += _sum8(dh * xh)
        dxh = dh * gv
        dx_ref[...] = dxo_ref[...] + r * (dxh - xh * jnp.mean(dxh * xh, axis=-1, keepdims=True))

    row = pl.BlockSpec((tt, D), lambda i: (i, 0))
    return _token_call(
        body,
        name,
        T // tt,
        [row, row, _resident((1, D))] + [pl.BlockSpec((tt, N), lambda i: (i, 0))] * npairs + [_resident((N, D))] * npairs,
        [dxo, x, g, *[p[0] for p in pairs], *[p[1] for p in pairs]],
        [row, row, pl.BlockSpec((8, D), lambda i: (0, 0))],
        [jax.ShapeDtypeStruct((T, D), F32), jax.ShapeDtypeStruct((T, D), BF16), jax.ShapeDtypeStruct((8, D), F32)],
        [],
        comm,
    )


def ffn_bwd(dxo, m, g_post, w_down, silu, dact, x, g_pre, w_gate_t, w_up_t, scale, tt, name, comm=None):
    T, D = x.shape
    F = w_down.shape[0]
    ec = _tile(F, MXU_COLS_V7X, 128)

    def body(dxo_ref, m_ref, gpost_ref, wd_ref, silu_ref, dact_ref, x_ref, gpre_ref, wg_ref, wu_ref, dm_ref, dgpost_ref, dgate_ref, dup_ref, dx_ref, h_ref, dgpre_ref):
        @pl.when(pl.program_id(0) == 0)
        def _():
            dgpost_ref[...] = jnp.zeros_like(dgpost_ref)
            dgpre_ref[...] = jnp.zeros_like(dgpre_ref)

        dxo_v = dxo_ref[...]
        dy = scale * dxo_v
        mv = m_ref[...].astype(F32)
        r = _rstd(mv)
        mh = mv * r
        dgpost_ref[...] += _sum8(dy * mh)
        dmh = dy * gpost_ref[...]
        dm = (r * (dmh - mh * jnp.mean(dmh * mh, axis=-1, keepdims=True))).astype(BF16)
        dm_ref[...] = dm
        for j in range(F // ec):
            cols = slice(j * ec, (j + 1) * ec)
            da = lax.dot_general(dm, wd_ref[cols, :], NT, preferred_element_type=F32)
            dup_ref[:, cols] = (da * silu_ref[:, cols].astype(F32)).astype(BF16)
            dgate_ref[:, cols] = (da * dact_ref[:, cols].astype(F32)).astype(BF16)
        dh = jnp.dot(dgate_ref[...], wg_ref[...], preferred_element_type=F32) + jnp.dot(dup_ref[...], wu_ref[...], preferred_element_type=F32)
        xv = x_ref[...]
        gv = gpre_ref[...]
        rx = _rstd(xv)
        xh = xv * rx
        h_ref[...] = (xh * gv).astype(BF16)
        dgpre_ref[...] += _sum8(dh * xh)
        dxh = dh * gv
        dx_ref[...] = dxo_v + rx * (dxh - xh * jnp.mean(dxh * xh, axis=-1, keepdims=True))

    row = pl.BlockSpec((tt, D), lambda i: (i, 0))
    col = pl.BlockSpec((tt, F), lambda i: (i, 0))
    vec = _resident((1, D))
    wgt = _resident((F, D))
    acc8 = pl.BlockSpec((8, D), lambda i: (0, 0))
    return _token_call(
        body,
        name,
        T // tt,
        [row, row, vec, wgt, col, col, row, vec, wgt, wgt],
        [dxo, m, g_post, w_down, silu, dact, x, g_pre, w_gate_t, w_up_t],
        [row, acc8, col, col, row, row, acc8],
        [
            jax.ShapeDtypeStruct((T, D), BF16),
            jax.ShapeDtypeStruct((8, D), F32),
            jax.ShapeDtypeStruct((T, F), BF16),
            jax.ShapeDtypeStruct((T, F), BF16),
            jax.ShapeDtypeStruct((T, D), F32),
            jax.ShapeDtypeStruct((T, D), BF16),
            jax.ShapeDtypeStruct((8, D), F32),
        ],
        [],
        comm,
    )


def wgrad_tn(a, b, tk, name, comm=None):
    T, M = a.shape
    N = b.shape[1]
    assert M * N * 4 <= WGRAD_ACC_BYTES, (M, N)
    nt = T // tk

    def body(a_ref, b_ref, o_ref, acc_ref):
        k = pl.program_id(0)

        @pl.when(k == 0)
        def _():
            acc_ref[...] = jnp.zeros_like(acc_ref)

        acc_ref[...] += lax.dot_general(a_ref[...], b_ref[...], TN, preferred_element_type=F32)

        @pl.when(k == nt - 1)
        def _():
            o_ref[...] = acc_ref[...].astype(BF16)

    (out,), got = _token_call(
        body,
        name,
        nt,
        [pl.BlockSpec((tk, M), lambda k: (k, 0)), pl.BlockSpec((tk, N), lambda k: (k, 0))],
        [a, b],
        [pl.BlockSpec((M, N), lambda k: (0, 0))],
        [jax.ShapeDtypeStruct((M, N), BF16)],
        [pltpu.VMEM((M, N), F32)],
        comm,
    )
    return (out, got) if comm else out


def _sgu_recompute(z_ref, lng_ref, lnb_ref, S):
    z = z_ref[...].astype(F32)
    zu, zv = z[:, :S], z[:, S:]
    pu, pv = _gelu_parts(zu), _gelu_parts(zv)
    u = zu * pu[2]
    v = zv * pv[2]
    vc = v - jnp.mean(v, axis=-1, keepdims=True)
    rstd = lax.rsqrt(jnp.mean(vc * vc, axis=-1, keepdims=True) + EPS)
    vhat = vc * rstd
    vln = (vhat * lng_ref[...] + lnb_ref[...]).astype(BF16)
    return zu, zv, pu, pv, u, vhat, rstd, vln


def sgu_mix_fwd(zpre, lng, lnb, ws, bias_full, tt, name, comm=None):
    T, S2 = zpre.shape
    S = S2 // 2
    dg = S // GROUPS

    def body(z_ref, lng_ref, lnb_ref, ws_ref, bias_ref, o_ref):
        _, _, _, _, u, _, _, vln = _sgu_recompute(z_ref, lng_ref, lnb_ref, S)
        for c in range(tt // CHUNK):
            rows = slice(c * CHUNK, (c + 1) * CHUNK)
            for gi in range(GROUPS):
                cols = slice(gi * dg, (gi + 1) * dg)
                mixed = jnp.dot(ws_ref[gi], vln[rows, cols], preferred_element_type=F32) + bias_ref[:, cols]
                o_ref[rows, cols] = (u[rows, cols] * mixed).astype(BF16)

    vec = pl.BlockSpec((1, S), lambda i: (0, 0))
    return _token_call(
        body,
        name,
        T // tt,
        [
            pl.BlockSpec((tt, S2), lambda i: (i, 0)),
            vec,
            vec,
            pl.BlockSpec((GROUPS, CHUNK, CHUNK), lambda i: (0, 0, 0)),
            pl.BlockSpec((CHUNK, S), lambda i: (0, 0)),
        ],
        [zpre, lng, lnb, ws, bias_full],
        [pl.BlockSpec((tt, S), lambda i: (i, 0))],
        [jax.ShapeDtypeStruct((T, S), BF16)],
        [],
        comm,
    )


def sgu_mix_bwd(dxo, m, g_post, w_out, zpre, lng, lnb, ws, ws_t, bias_full, tt, name, comm=None):
    T, S2 = zpre.shape
    D = dxo.shape[1]
    S = S2 // 2
    dg = S // GROUPS

    def body(dxo_ref, m_ref, gpost_ref, wout_ref, z_ref, lng_ref, lnb_ref, ws_ref, wst_ref, bias_ref, dm_ref, dgpost_ref, dz_ref, dws_ref, dbias_ref, dlng_ref, dlnb_ref, du_ref, dvln_ref, dgt_ref):
        @pl.when(pl.program_id(0) == 0)
        def _():
            dgpost_ref[...] = jnp.zeros_like(dgpost_ref)
            dws_ref[...] = jnp.zeros_like(dws_ref)
            dbias_ref[...] = jnp.zeros_like(dbias_ref)
            dlng_ref[...] = jnp.zeros_like(dlng_ref)
            dlnb_ref[...] = jnp.zeros_like(dlnb_ref)

        dy = dxo_ref[...]
        mv = m_ref[...].astype(F32)
        r = _rstd(mv)
        mh = mv * r
        dgpost_ref[...] += _sum8(dy * mh)
        dmh = dy * gpost_ref[...]
        dm = (r * (dmh - mh * jnp.mean(dmh * mh, axis=-1, keepdims=True))).astype(BF16)
        dm_ref[...] = dm
        dgt_ref[...] = lax.dot_general(dm, wout_ref[...], NT, preferred_element_type=F32)

        zu, zv, pu, pv, u, vhat, rstd, vln = _sgu_recompute(z_ref, lng_ref, lnb_ref, S)
        for c in range(tt // CHUNK):
            rows = slice(c * CHUNK, (c + 1) * CHUNK)
            for gi in range(GROUPS):
                cols = slice(gi * dg, (gi + 1) * dg)
                v_cg = vln[rows, cols]
                mixed = jnp.dot(ws_ref[gi], v_cg, preferred_element_type=F32) + bias_ref[:, cols]
                dgt = dgt_ref[rows, cols]
                du_ref[rows, cols] = dgt * mixed
                dmx = dgt * u[rows, cols]
                dbias_ref[:, cols] += dmx
                dmx16 = dmx.astype(BF16)
                dws_ref[gi] += lax.dot_general(dmx16, v_cg, NT, preferred_element_type=F32)
                dvln_ref[rows, cols] = jnp.dot(wst_ref[gi], dmx16, preferred_element_type=F32)

        dvln = dvln_ref[...]
        dlng_ref[...] += _sum8(dvln * vhat)
        dlnb_ref[...] += _sum8(dvln)
        dvhat = dvln * lng_ref[...]
        dv = rstd * (dvhat - jnp.mean(dvhat, axis=-1, keepdims=True) - vhat * jnp.mean(dvhat * vhat, axis=-1, keepdims=True))
        dz_ref[:, :S] = (du_ref[...] * _gelu_grad(zu, pu)).astype(BF16)
        dz_ref[:, S:] = (dv * _gelu_grad(zv, pv)).astype(BF16)

    vec = pl.BlockSpec((1, S), lambda i: (0, 0))
    wsp = pl.BlockSpec((GROUPS, CHUNK, CHUNK), lambda i: (0, 0, 0))
    full = pl.BlockSpec((CHUNK, S), lambda i: (0, 0))
    acc8 = pl.BlockSpec((8, S), lambda i: (0, 0))
    row = pl.BlockSpec((tt, D), lambda i: (i, 0))
    return _token_call(
        body,
        name,
        T // tt,
        [row, row, _resident((1, D)), _resident((S, D)), pl.BlockSpec((tt, S2), lambda i: (i, 0)), vec, vec, wsp, wsp, full],
        [dxo, m, g_post, w_out, zpre, lng, lnb, ws, ws_t, bias_full],
        [row, pl.BlockSpec((8, D), lambda i: (0, 0)), pl.BlockSpec((tt, S2), lambda i: (i, 0)), wsp, full, acc8, acc8],
        [
            jax.ShapeDtypeStruct((T, D), BF16),
            jax.ShapeDtypeStruct((8, D), F32),
            jax.ShapeDtypeStruct((T, S2), BF16),
            jax.ShapeDtypeStruct((GROUPS, CHUNK, CHUNK), F32),
            jax.ShapeDtypeStruct((CHUNK, S), F32),
            jax.ShapeDtypeStruct((8, S), F32),
            jax.ShapeDtypeStruct((8, S), F32),
        ],
        [pltpu.VMEM((tt, S), F32), pltpu.VMEM((tt, S), F32), pltpu.VMEM((tt, S), F32)],
        comm,
    )


def _shifted_planes(sh, tt):
    n = tt + HALO - SUBLANES
    for s in range(1, SUBLANES):
        sh[s, 0:n, :] = sh[0, pl.ds(s, n), :]


def conv_mid_fwd(p, wdw, bdw, lng, lnb, tt, name, comm=None):
    T, C2 = p.shape
    C = C2 // 2

    def body(p_ref, w_ref, b_ref, lng_ref, lnb_ref, yc_ref, ys_ref, sh):
        @pl.when(pl.program_id(0) == 0)
        def _():
            sh[0, 0:HALO, :] = jnp.zeros((HALO, C), F32)

        @pl.when(pl.program_id(0) > 0)
        def _():
            sh[0, 0:HALO, :] = sh[0, tt : tt + HALO, :]

        pv = p_ref[...].astype(F32)
        sh[0, HALO:, :] = pv[:, :C] * _sigmoid(pv[:, C:])
        _shifted_planes(sh, tt)
        for lb in range(C // LANES):
            lanes = slice(lb * LANES, (lb + 1) * LANES)
            wk = [jnp.broadcast_to(w_ref[k : k + 1, lanes], (SUBLANES, LANES)) for k in range(CONV_W)]
            bias = jnp.broadcast_to(b_ref[:, lanes], (SUBLANES, LANES))

            def rows(i, carry):
                r0 = pl.multiple_of(i * CONV_ROWS, CONV_ROWS)
                accs = [[bias, jnp.zeros((SUBLANES, LANES), F32)] for _ in range(CONV_ROWS // SUBLANES)]
                for k in range(CONV_W):
                    o = HALO - (CONV_W - 1) + k
                    for j, acc in enumerate(accs):
                        blk = sh[o % SUBLANES, pl.ds(r0 + (o // SUBLANES + j) * SUBLANES, SUBLANES), lanes]
                        acc[k % 2] = acc[k % 2] + wk[k] * blk
                for j, acc in enumerate(accs):
                    yc_ref[pl.ds(r0 + j * SUBLANES, SUBLANES), lanes] = acc[0] + acc[1]
                return carry

            lax.fori_loop(0, tt // CONV_ROWS, rows, 0)
        acc = yc_ref[...]
        yc = acc - jnp.mean(acc, axis=-1, keepdims=True)
        yn = yc * lax.rsqrt(jnp.mean(yc * yc, axis=-1, keepdims=True) + EPS) * lng_ref[...] + lnb_ref[...]
        ys_ref[...] = (yn * _sigmoid(yn)).astype(BF16)

    vec = pl.BlockSpec((1, C), lambda i: (0, 0))
    row = pl.BlockSpec((tt, C), lambda i: (i, 0))
    return _token_call(
        body,
        name,
        T // tt,
        [pl.BlockSpec((tt, C2), lambda i: (i, 0)), pl.BlockSpec((HALO, C), lambda i: (0, 0)), vec, vec, vec],
        [p, wdw, bdw, lng, lnb],
        [row, row],
        [jax.ShapeDtypeStruct((T, C), F32), jax.ShapeDtypeStruct((T, C), BF16)],
        [pltpu.VMEM((SUBLANES, tt + HALO, C), F32)],
        comm,
    )


def conv_mid_bwd(dys, yc, p, wdw, lng, lnb, tt, name, comm=None):
    T, C2 = p.shape
    C = C2 // 2
    n = T // tt

    def body(dys_ref, yc_ref, p_ref, w_ref, lng_ref, lnb_ref, dp_ref, dw_ref, db_ref, dlng_ref, dlnb_ref, sh, y_s, dy_s):
        @pl.when(pl.program_id(0) == 0)
        def _():
            sh[0, tt : tt + HALO, :] = jnp.zeros((HALO, C), F32)
            dw_ref[...] = jnp.zeros_like(dw_ref)
            db_ref[...] = jnp.zeros_like(db_ref)
            dlng_ref[...] = jnp.zeros_like(dlng_ref)
            dlnb_ref[...] = jnp.zeros_like(dlnb_ref)

        @pl.when(pl.program_id(0) > 0)
        def _():
            sh[0, tt : tt + HALO, :] = sh[0, 0:HALO, :]

        ycv = yc_ref[...]
        ycc = ycv - jnp.mean(ycv, axis=-1, keepdims=True)
        rstd = lax.rsqrt(jnp.mean(ycc * ycc, axis=-1, keepdims=True) + EPS)
        yhat = ycc * rstd
        lng_v = lng_ref[...]
        yn = yhat * lng_v + lnb_ref[...]
        sg = _sigmoid(yn)
        dyn = dys_ref[...].astype(F32) * (sg * (1.0 + yn * (1.0 - sg)))
        dlng_ref[...] += _sum8(dyn * yhat)
        dlnb_ref[...] += _sum8(dyn)
        dyh = dyn * lng_v
        dyc = rstd * (dyh - jnp.mean(dyh, axis=-1, keepdims=True) - yhat * jnp.mean(dyh * yhat, axis=-1, keepdims=True))
        db_ref[...] += _sum8(dyc)
        sh[0, 0:tt, :] = dyc
        _shifted_planes(sh, tt)

        pv = p_ref[...].astype(F32)
        a = pv[:, :C]
        sgate = _sigmoid(pv[:, C:])
        y_s[...] = a * sgate
        for lb in range(C // LANES):
            lanes = slice(lb * LANES, (lb + 1) * LANES)
            for k0, k1 in ((0, CONV_W // 2), (CONV_W // 2, CONV_W)):
                wk = [jnp.broadcast_to(w_ref[k : k + 1, lanes], (SUBLANES, LANES)) for k in range(k0, k1)]

                def rows(i, dw_acc, k0=k0, k1=k1, wk=wk, lanes=lanes):
                    r0 = pl.multiple_of(i * CONV_ROWS, CONV_ROWS)
                    dw_acc = list(dw_acc)
                    zero = jnp.zeros((SUBLANES, LANES), F32)
                    ybs, accs = [], []
                    for j in range(CONV_ROWS // SUBLANES):
                        at = pl.ds(r0 + j * SUBLANES, SUBLANES)
                        ybs.append(y_s[at, lanes])
                        accs.append([zero if k0 == 0 else dy_s[at, lanes], zero])
                    for n_k, k in enumerate(range(k0, k1)):
                        o = CONV_W - 1 - k
                        for j, acc in enumerate(accs):
                            blk = sh[o % SUBLANES, pl.ds(r0 + (o // SUBLANES + j) * SUBLANES, SUBLANES), lanes]
                            acc[n_k % 2] = acc[n_k % 2] + wk[n_k] * blk
                            dw_acc[n_k] = dw_acc[n_k] + ybs[j] * blk
                    for j, acc in enumerate(accs):
                        dy_s[pl.ds(r0 + j * SUBLANES, SUBLANES), lanes] = acc[0] + acc[1]
                    return tuple(dw_acc)

                dw_acc = lax.fori_loop(0, tt // CONV_ROWS, rows, tuple(jnp.zeros((SUBLANES, LANES), F32) for _ in range(k0, k1)))
                for n_k, k in enumerate(range(k0, k1)):
                    dw_ref[SUBLANES * k : SUBLANES * (k + 1), lanes] += dw_acc[n_k]
        dy = dy_s[...]
        dp_ref[:, :C] = (dy * sgate).astype(BF16)
        dp_ref[:, C:] = (dy * a * sgate * (1.0 - sgate)).astype(BF16)

    vec = pl.BlockSpec((1, C), lambda i: (0, 0))
    row = pl.BlockSpec((tt, C), lambda i: (n - 1 - i, 0))
    row2 = pl.BlockSpec((tt, C2), lambda i: (n - 1 - i, 0))
    acc8 = pl.BlockSpec((8, C), lambda i: (0, 0))
    return _token_call(
        body,
        name,
        n,
        [row, row, row2, pl.BlockSpec((HALO, C), lambda i: (0, 0)), vec, vec],
        [dys, yc, p, wdw, lng, lnb],
        [row2, pl.BlockSpec((HALO * 8, C), lambda i: (0, 0)), acc8, acc8, acc8],
        [
            jax.ShapeDtypeStruct((T, C2), BF16),
            jax.ShapeDtypeStruct((HALO * 8, C), F32),
            jax.ShapeDtypeStruct((8, C), F32),
            jax.ShapeDtypeStruct((8, C), F32),
            jax.ShapeDtypeStruct((8, C), F32),
        ],
        [pltpu.VMEM((SUBLANES, tt + HALO, C), F32), pltpu.VMEM((tt, C), F32), pltpu.VMEM((tt, C), F32)],
        comm,
    )


def sum_slots(slots, name):
    _, r, w = slots.shape

    def body(s_ref, o_ref):
        total = s_ref[0].astype(F32)
        for p in range(1, N_DEV):
            total = total + s_ref[p].astype(F32)
        o_ref[...] = total

    return _pcall(body, name=name, out_shape=jax.ShapeDtypeStruct((r, w), F32), compiler_params=pltpu.CompilerParams(vmem_limit_bytes=VMEM_LIMIT_V7X))(slots)


def adamw(w, g, m, v, name):
    R, C = w.shape
    tr = _tile(R, 512, 8)
    c1 = 1.0 - ADAM_B1**ADAM_STEP
    c2 = 1.0 - ADAM_B2**ADAM_STEP

    def body(w_ref, g_ref, m_ref, v_ref, d_ref, mo_ref, vo_ref):
        gv = g_ref[...]
        m2 = ADAM_B1 * m_ref[...] + (1.0 - ADAM_B1) * gv
        v2 = ADAM_B2 * v_ref[...] + (1.0 - ADAM_B2) * (gv * gv)
        mo_ref[...] = m2
        vo_ref[...] = v2
        d_ref[...] = -ADAM_LR * ((m2 / c1) / (jnp.sqrt(v2 / c2) + ADAM_EPS) + ADAM_WD * w_ref[...])

    blk = pl.BlockSpec((tr, C), lambda i: (i, 0))
    return _pcall(
        body,
        name=name,
        grid=(R // tr,),
        in_specs=[blk] * 4,
        out_specs=[blk] * 3,
        out_shape=[jax.ShapeDtypeStruct((R, C), F32)] * 3,
        compiler_params=_cparams(1),
    )(w, g, m, v)


def _sublayers(depth):
    out = []
    for layer in range(depth):
        out.append(("ffn", layer, 0))
        out.append(("sgu" if layer % N_MIXERS == 0 else "conv", layer, layer // N_MIXERS))
        out.append(("ffn", layer, 1))
    return out


def _row_blocks(sub, shards):
    kind, layer, idx = sub
    if kind == "ffn":
        blk = {"gate_t": shards["ff_w_gate"][layer, idx].T, "up_t": shards["ff_w_up"][layer, idx].T, "down": shards["ff_w_down"][layer, idx]}
    elif kind == "sgu":
        blk = {"in_t": shards["sgu_w_in"][idx].T, "out": shards["sgu_w_out"][idx]}
    else:
        blk = {"pw1_t": shards["conv_w_pw1"][idx].T, "pw2": shards["conv_w_pw2"][idx]}
    return {nm: arr.astype(BF16) for nm, arr in blk.items()}


def _pad8(a):
    return jnp.pad(a, ((0, (-a.shape[0]) % 8), (0, 0)))


def kernel(x, norm_g, ff_w_gate, ff_w_up, ff_w_down, sgu_w_in, sgu_ln_g, sgu_ln_b, sgu_w_spatial, sgu_b_spatial, sgu_w_out, conv_w_pw1, conv_w_dw, conv_b_dw, conv_ln_g, conv_ln_b, conv_w_pw2, loss_target, m_norm_g, m_ff_w_gate, m_ff_w_up, m_ff_w_down, m_sgu_w_in, m_sgu_ln_g, m_sgu_ln_b, m_sgu_w_spatial, m_sgu_b_spatial, m_sgu_w_out, m_conv_w_pw1, m_conv_w_dw, m_conv_b_dw, m_conv_ln_g, m_conv_ln_b, m_conv_w_pw2, v_norm_g, v_ff_w_gate, v_ff_w_up, v_ff_w_down, v_sgu_w_in, v_sgu_ln_g, v_sgu_ln_b, v_sgu_w_spatial, v_sgu_b_spatial, v_sgu_w_out, v_conv_w_pw1, v_conv_w_dw, v_conv_b_dw, v_conv_ln_g, v_conv_ln_b, v_conv_w_pw2):
    names = ["norm_g", "ff_w_gate", "ff_w_up", "ff_w_down", "sgu_w_in", "sgu_ln_g", "sgu_ln_b", "sgu_w_spatial", "sgu_b_spatial", "sgu_w_out", "conv_w_pw1", "conv_w_dw", "conv_b_dw", "conv_ln_g", "conv_ln_b", "conv_w_pw2"]
    weights = dict(zip(names, [norm_g, ff_w_gate, ff_w_up, ff_w_down, sgu_w_in, sgu_ln_g, sgu_ln_b, sgu_w_spatial, sgu_b_spatial, sgu_w_out, conv_w_pw1, conv_w_dw, conv_b_dw, conv_ln_g, conv_ln_b, conv_w_pw2]))
    moments_m = dict(zip(names, [m_norm_g, m_ff_w_gate, m_ff_w_up, m_ff_w_down, m_sgu_w_in, m_sgu_ln_g, m_sgu_ln_b, m_sgu_w_spatial, m_sgu_b_spatial, m_sgu_w_out, m_conv_w_pw1, m_conv_w_dw, m_conv_b_dw, m_conv_ln_g, m_conv_ln_b, m_conv_w_pw2]))
    moments_v = dict(zip(names, [v_norm_g, v_ff_w_gate, v_ff_w_up, v_ff_w_down, v_sgu_w_in, v_sgu_ln_g, v_sgu_ln_b, v_sgu_w_spatial, v_sgu_b_spatial, v_sgu_w_out, v_conv_w_pw1, v_conv_w_dw, v_conv_b_dw, v_conv_ln_g, v_conv_ln_b, v_conv_w_pw2]))

    _, T, D = x.shape
    depth = norm_g.shape[0]
    n_conv = conv_w_dw.shape[0]
    n_sgu = sgu_w_in.shape[0]
    S = sgu_ln_g.shape[1]
    lanes = norm_g.shape[2]
    subs = _sublayers(depth)
    n_sub = len(subs)

    cx, cy, cc = (lax.axis_index(a) for a in MESH_AXES)
    my_block = 4 * cx + 2 * cy + cc

    blocks = [_row_blocks(sub, weights) for sub in subs]
    dw_pad = jnp.pad(conv_w_dw, ((0, 0), (0, HALO - CONV_W), (0, 0)))
    small_parts = [_pad8(p) for p in (norm_g.reshape(-1, lanes), dw_pad.reshape(-1, lanes), conv_b_dw, conv_ln_g, conv_ln_b)]
    small_rows = [p.shape[0] for p in small_parts]
    small = jnp.concatenate(small_parts, axis=0)

    def gathered(block_names, outs):
        return {nm: o.reshape(N_DEV * o.shape[1], o.shape[2]) for nm, o in zip(block_names, outs)}

    first_names = [nm for nm in blocks[0] if nm != "down"]
    first = gather_two_level([blocks[0][nm] for nm in first_names] + [small], name="gather_first")
    W = [None] * n_sub
    W[0] = gathered(first_names, first[:-1])
    gsmall = first[-1]

    small_full = jnp.transpose(gsmall, (1, 0, 2)).reshape(gsmall.shape[1], N_DEV * lanes)
    so = [0]
    for r in small_rows:
        so.append(so[-1] + r)
    norm_full = small_full[so[0] : so[0] + depth * norm_g.shape[1]].reshape(depth, -1, D)
    dw_full = small_full[so[1] : so[1] + n_conv * HALO].reshape(n_conv, HALO, D)
    bdw_full, clng_full, clnb_full = (small_full[so[k] : so[k] + n_conv] for k in (2, 3, 4))

    causal = jnp.tril(jnp.ones((CHUNK, CHUNK), dtype=bool))
    ws_all = jnp.where(causal[None, None], sgu_w_spatial, 0.0).astype(BF16)
    wst_all = jnp.swapaxes(ws_all, -1, -2)
    bias_full_all = jnp.repeat(jnp.swapaxes(sgu_b_spatial, -1, -2), S // GROUPS, axis=-1)

    tt = _tile(T, 512, CHUNK)
    tt_mix = _tile(T, 256, CHUNK)

    tk = _tile(T, 1024, CHUNK)

    def vec(v):
        return v.reshape(1, -1)

    def split_first(comm):
        return (comm[:1], comm[1:]) if comm else (None, None)

    def norms(kind, layer, idx):
        pre = 4 * idx if kind == "ffn" else 2
        return vec(norm_full[layer, pre]), vec(norm_full[layer, pre + 1])

    xs = x[0]
    saved = []
    for si, (kind, layer, idx) in enumerate(subs):
        w = W[si]
        g_pre, g_post = norms(kind, layer, idx)
        nxt = [("gather", a) for a in blocks[si + 1].values()] if si + 1 < n_sub else None
        nxt_first, nxt_rest = split_first(nxt)
        if kind == "ffn":
            own = [("gather", blocks[0]["down"])] if si == 0 else []
            (silu, dact, a), got = rms_matmul_nt(xs, g_pre, [w["gate_t"], w["up_t"]], tt, "ffn_in", own + (nxt or []), swiglu=True)
            if own:
                w.update(gathered(["down"], got[:1]))
                got = got[1:]
            if si == n_sub - 1:
                x_new, o, sq = matmul_rms_res(xs, a, w["down"], g_post, FFN_SCALE, tt, "ffn_out_loss", loss_target[0])
            else:
                x_new, o = matmul_rms_res(xs, a, w["down"], g_post, FFN_SCALE, tk, "ffn_out")
            saved.append((xs, silu, dact, o, a))
        elif kind == "sgu":
            (zpre,), got = rms_matmul_nt(xs, g_pre, [w["in_t"]], tt, "sgu_in", nxt_first)
            (gated,), got_rest = sgu_mix_fwd(zpre, vec(sgu_ln_g[idx]), vec(sgu_ln_b[idx]), ws_all[idx], bias_full_all[idx], tt_mix, "sgu_mix", nxt_rest)
            got = list(got) + list(got_rest)
            x_new, mm = matmul_rms_res(xs, gated, w["out"], g_post, 1.0, tk, "sgu_out")
            saved.append((xs, zpre, gated, mm))
        else:
            (p,), got = rms_matmul_nt(xs, g_pre, [w["pw1_t"]], tt, "conv_pw1", nxt_first)
            (yc, ys), got_rest = conv_mid_fwd(p, dw_full[idx], vec(bdw_full[idx]), vec(clng_full[idx]), vec(clnb_full[idx]), tt, "conv_mid", nxt_rest)
            got = list(got) + list(got_rest)
            x_new, mm = matmul_rms_res(xs, ys, w["pw2"], g_post, 1.0, tk, "conv_pw2")
            saved.append((xs, p, yc, ys, mm))
        if nxt:
            W[si + 1] = gathered(blocks[si + 1].keys(), got)
        xs = x_new

    dx = xs
    loss = lax.psum(0.5 * jnp.sum(sq) / D, MESH_AXES)

    d_norm = [[None] * norm_full.shape[1] for _ in range(depth)]
    d_sgu = [None] * n_sgu
    d_conv = [None] * n_conv
    slots = [None] * n_sub
    pending = None

    def scatter_of(p):
        return [("scatter", dwm.reshape(N_DEV, dwm.shape[0] // N_DEV, D)) for dwm in p[1].values()] if p else None

    def pack_small():
        def sum8(v):
            return jnp.sum(v, axis=0)

        g_norm = jnp.stack([jnp.stack([sum8(d) for d in row]) for row in d_norm])
        g_dw = jnp.stack([jnp.sum(d[0].reshape(HALO, 8, D), axis=1) for d in d_conv])
        g_bdw, g_clng, g_clnb = (jnp.stack([sum8(d[k]) for d in d_conv]) for k in (1, 2, 3))
        g_slng, g_slnb = (jnp.stack([sum8(d[k]) for d in d_sgu]) for k in (2, 3))
        g_bsp = jnp.stack([jnp.sum(d[1].reshape(CHUNK, GROUPS, S // GROUPS), axis=-1).T for d in d_sgu])
        g_wsp = jnp.stack([jnp.where(causal[None], d[0], 0.0) for d in d_sgu])
        parts = [g_norm, g_dw, g_bdw, g_clng, g_clnb, g_slng, g_slnb, g_bsp, g_wsp]
        return parts, [p.size // D for p in parts], jnp.concatenate([_pad8(p.reshape(-1, D)) for p in parts], axis=0)

    for si in reversed(range(n_sub)):
        kind, layer, idx = subs[si]
        w = W[si]
        g_pre, g_post = norms(kind, layer, idx)
        pre = 4 * idx if kind == "ffn" else 2
        sc_first, sc_rest = split_first(scatter_of(pending))
        if kind == "ffn":
            xs_in, silu, dact, o, a = saved[si]
            (do, dg_post, dgate, dup, dx, h, dg_pre), got = ffn_bwd(
                dx, o, g_post, w["down"], silu, dact, xs_in, g_pre, w["gate_t"], w["up_t"], FFN_SCALE, tt_mix, "ffn_bwd", scatter_of(pending)
            )
            if si == 0:
                d_norm[layer][pre], d_norm[layer][pre + 1] = dg_pre, dg_post
                sparts, srows, sgrad = pack_small()
                dw_down, got_small = wgrad_tn(a, do, tk, "wgrad_ffn_out", [("gather", sgrad)])
                dw_up, got_down = wgrad_tn(dup, h, tk, "wgrad_ffn_in", scatter_of((si, {"down": dw_down})))
                dw_gate, got_up = wgrad_tn(dgate, h, tk, "wgrad_ffn_in", scatter_of((si, {"up_t": dw_up})))
            else:
                dw_down = wgrad_tn(a, do, tk, name="wgrad_ffn_out")
                dw_up = wgrad_tn(dup, h, tk, name="wgrad_ffn_in")
                dw_gate = wgrad_tn(dgate, h, tk, name="wgrad_ffn_in")
            dws_now = {"gate_t": dw_gate, "up_t": dw_up, "down": dw_down}
        elif kind == "sgu":
            xs_in, zpre, gated, mm = saved[si]
            (dm, dg_post, dzpre, dws, dbias, dlng, dlnb), got = sgu_mix_bwd(
                dx, mm, g_post, w["out"], zpre, vec(sgu_ln_g[idx]), vec(sgu_ln_b[idx]), ws_all[idx], wst_all[idx], bias_full_all[idx], tt_mix, "sgu_mix_bwd", scatter_of(pending)
            )
            (dx, h, dg_pre), _ = matmul_rms_bwd(dx, xs_in, g_pre, [(dzpre, w["in_t"])], tt, "sgu_in_bwd")
            dws_now = {"in_t": wgrad_tn(dzpre, h, tt, name="wgrad_sgu_in"), "out": wgrad_tn(gated, dm, tk, name="wgrad_sgu_out")}
            d_sgu[idx] = (dws, dbias, dlng, dlnb)
        else:
            xs_in, p, yc, ys, mm = saved[si]
            (dm, dg_post, dys), got = rms_bwd_matmul_nt(dx, mm, g_post, w["pw2"], 1.0, None, tt, "conv_pw2_bwd", sc_first)
            (dp, dwdw, dbdw, dlng, dlnb), got_rest = conv_mid_bwd(dys, yc, p, dw_full[idx], vec(clng_full[idx]), vec(clnb_full[idx]), tt, "conv_mid_bwd", sc_rest)
            got = list(got) + list(got_rest)
            (dx, h, dg_pre), _ = matmul_rms_bwd(dx, xs_in, g_pre, [(dp, w["pw1_t"])], tt, "conv_pw1_bwd")
            dws_now = {"pw1_t": wgrad_tn(dp, h, tk, name="wgrad_conv_pw1"), "pw2": wgrad_tn(ys, dm, tk, name="wgrad_conv_pw2")}
            d_conv[idx] = (dwdw, dbdw, dlng, dlnb)
        d_norm[layer][pre], d_norm[layer][pre + 1] = dg_pre, dg_post
        if pending:
            slots[pending[0]] = dict(zip(pending[1].keys(), got))
        pending = (si, dws_now)
    grad_x = dx[None]

    last = exchange(scatter_of((0, {"gate_t": pending[1]["gate_t"]})), name="scatter_last")
    slots[0] = {"gate_t": last[0], "up_t": got_up[0], "down": got_down[0]}
    stotal = sum_slots(got_small[0], name="sum_slots")

    gs = {(si, nm): sum_slots(s, name="sum_slots") for si in range(n_sub) for nm, s in slots[si].items()}
    grads = {}
    ffn_si = {(layer, idx): si for si, (kind, layer, idx) in enumerate(subs) if kind == "ffn"}
    sgu_si = {idx: si for si, (kind, layer, idx) in enumerate(subs) if kind == "sgu"}
    conv_si = {idx: si for si, (kind, layer, idx) in enumerate(subs) if kind == "conv"}
    grads["ff_w_gate"] = jnp.stack([jnp.stack([gs[(ffn_si[(l, f)], "gate_t")].T for f in range(2)]) for l in range(depth)])
    grads["ff_w_up"] = jnp.stack([jnp.stack([gs[(ffn_si[(l, f)], "up_t")].T for f in range(2)]) for l in range(depth)])
    grads["ff_w_down"] = jnp.stack([jnp.stack([gs[(ffn_si[(l, f)], "down")] for f in range(2)]) for l in range(depth)])
    grads["sgu_w_in"] = jnp.stack([gs[(sgu_si[j], "in_t")].T for j in range(n_sgu)])
    grads["sgu_w_out"] = jnp.stack([gs[(sgu_si[j], "out")] for j in range(n_sgu)])
    grads["conv_w_pw1"] = jnp.stack([gs[(conv_si[j], "pw1_t")].T for j in range(n_conv)])
    grads["conv_w_pw2"] = jnp.stack([gs[(conv_si[j], "pw2")] for j in range(n_conv)])

    so = [0]
    for r in srows:
        so.append(so[-1] + r + (-r) % 8)
    sp = [stotal[so[k] : so[k] + srows[k]].reshape(sparts[k].shape) for k in range(len(sparts))]

    def my_lanes(v):
        return lax.dynamic_slice_in_dim(v, my_block * lanes, lanes, axis=-1)

    grads["norm_g"] = my_lanes(sp[0])
    grads["conv_w_dw"] = my_lanes(sp[1])[:, :CONV_W]
    grads["conv_b_dw"] = my_lanes(sp[2])
    grads["conv_ln_g"] = my_lanes(sp[3])
    grads["conv_ln_b"] = my_lanes(sp[4])
    grads["sgu_ln_g"], grads["sgu_ln_b"], grads["sgu_b_spatial"], grads["sgu_w_spatial"] = sp[5], sp[6], sp[7], sp[8]

    deltas, new_m, new_v = {}, {}, {}
    for nm in names:
        w = weights[nm]
        two_d = (-1, w.shape[-1])
        d, m2, v2 = adamw(w.reshape(two_d), grads[nm].reshape(two_d), moments_m[nm].reshape(two_d), moments_v[nm].reshape(two_d), name="adamw")
        deltas[nm], new_m[nm], new_v[nm] = d.reshape(w.shape), m2.reshape(w.shape), v2.reshape(w.shape)

    return (loss, grad_x, *[grads[n] for n in names], *[deltas[n] for n in names], *[new_m[n] for n in names], *[new_v[n] for n in names])
```

```python
import jax
import jax.numpy as jnp
from jax import lax
from jax.experimental import pallas as pl
from jax.experimental.pallas import tpu as pltpu

F32 = jnp.float32
BF16 = jnp.bfloat16

EPS = 1e-6
FFN_SCALE = 0.5
N_MIXERS = 2
CHUNK = 128
GROUPS = 8
CONV_W = 31
HALO = 32
GELU_C0 = 0.7978845608028654
GELU_C1 = 0.044715
ADAM_LR, ADAM_B1, ADAM_B2, ADAM_EPS, ADAM_WD, ADAM_STEP = 0.001, 0.9, 0.999, 1e-08, 0.01, 10

MESH_AXES = ("x", "y", "c")
N_DEV = 8
VMEM_LIMIT_V7X = 56 * 1024 * 1024
WGRAD_ACC_BYTES = 16 * 1024 * 1024
MXU_COLS_V7X = 256
SUBLANES, LANES = 8, 128
CONV_ROWS = 32

NT = (((1,), (1,)), ((), ()))
TN = (((0,), (0,)), ((), ()))

HBM_SPEC = pl.BlockSpec(memory_space=pl.ANY)
MESH_ID = pl.DeviceIdType.MESH


def _pcall(body, **kw):
    return pl.pallas_call(body, **kw)


def _cparams(n_axes):
    return pltpu.CompilerParams(dimension_semantics=("arbitrary",) * n_axes, vmem_limit_bytes=VMEM_LIMIT_V7X)


def _tile(n, pref, align):
    if n <= pref:
        return n
    t = (pref // align) * align
    while t > align and n % t:
        t -= align
    assert n % t == 0, (n, pref, align)
    return t


def _rstd(x):
    return lax.rsqrt(jnp.mean(x * x, axis=-1, keepdims=True) + EPS)


def _sum8(v):
    t, d = v.shape
    return v.reshape(t // 8, 8, d).sum(axis=0)


def _sigmoid(x):
    return jax.nn.sigmoid(x)


def _gelu_parts(z):
    zz = z * z
    t = jnp.tanh(z * (GELU_C0 + (GELU_C0 * GELU_C1) * zz))
    return zz, t, 0.5 * t + 0.5


def _gelu_grad(z, parts):
    zz, t, cdf = parts
    return cdf + z * (1.0 - t * t) * ((0.5 * GELU_C0) + (1.5 * GELU_C0 * GELU_C1) * zz)


def _resident(shape):
    return pl.BlockSpec(shape, lambda i: (0,) * len(shape), pipeline_mode=pl.Buffered(1))


def _exchange_io(comm):
    n = len(comm)
    out_shape = [jax.ShapeDtypeStruct(((N_DEV,) + a.shape) if kind == "gather" else a.shape, a.dtype) for kind, a in comm]
    scratch = [pltpu.SemaphoreType.DMA((n, N_DEV - 1)), pltpu.SemaphoreType.DMA((n, N_DEV - 1)), pltpu.SemaphoreType.DMA((n,))]
    return [HBM_SPEC] * n, [HBM_SPEC] * n, out_shape, scratch


def _exchange_copies(kinds, in_refs, out_refs, send_sems, recv_sems, local_sems, with_arrivals=True):
    x, y, c = (lax.axis_index(a) for a in MESH_AXES)
    me = 4 * x + 2 * y + c
    local, sends, arrivals = [], [], []
    for a, kind in enumerate(kinds):
        src, dst = in_refs[a], out_refs[a]
        gather = kind == "gather"
        local.append(pltpu.make_async_copy(src if gather else src.at[me], dst.at[me], local_sems.at[a]))
        for k in range(N_DEV - 1):
            mask = k + 1
            px = 1 - x if mask & 4 else x
            py = 1 - y if mask & 2 else y
            pc = 1 - c if mask & 1 else c
            peer = 4 * px + 2 * py + pc
            block = src if gather else src.at[peer]
            for into, lst in ((me, sends), (peer, arrivals)):
                if lst is arrivals and not with_arrivals:
                    continue
                lst.append(
                    pltpu.make_async_remote_copy(
                        src_ref=block, dst_ref=dst.at[into], send_sem=send_sems.at[a, k], recv_sem=recv_sems.at[a, k], device_id=(px, py, pc), device_id_type=MESH_ID
                    )
                )
    return local, sends, arrivals


def _exchange_start(copies):
    local, sends, _ = copies
    for cp in local + sends:
        cp.start()


def _exchange_finish(copies):
    local, sends, arrivals = copies
    for cp in arrivals:
        cp.wait_recv()
    for cp in sends:
        cp.wait_send()
    for cp in local:
        cp.wait()


def exchange(comm, name):
    kinds = [k for k, _ in comm]
    n = len(comm)
    in_specs, out_specs, out_shape, scratch = _exchange_io(comm)

    def body(*refs):
        copies = _exchange_copies(kinds, refs[:n], refs[n : 2 * n], *refs[2 * n :])
        _exchange_start(copies)
        _exchange_finish(copies)

    return _pcall(body, name=name, in_specs=in_specs, out_specs=out_specs, out_shape=out_shape, scratch_shapes=scratch)(*[a for _, a in comm])


def gather_two_level(packs, name):
    n = len(packs)

    def body(*refs):
        in_refs, out_refs = refs[:n], refs[n : 2 * n]
        send_sems, recv_sems, local_sems = refs[2 * n :]
        x, y, c = (lax.axis_index(a) for a in MESH_AXES)
        chips = [(1 - x, y), (x, 1 - y), (1 - x, 1 - y)]
        me, sibling = (x, y, c), (x, y, 1 - c)

        def slot(a, px, py, pc):
            return out_refs[a].at[4 * px + 2 * py + pc]

        def copy(a, k, block, to, from_input=False):
            return pltpu.make_async_remote_copy(
                src_ref=in_refs[a] if from_input else slot(a, *block),
                dst_ref=slot(a, *block),
                send_sem=send_sems.at[a, k],
                recv_sem=recv_sems.at[a, k],
                device_id=to,
                device_id_type=MESH_ID,
            )

        mine = [pltpu.make_async_copy(in_refs[a], slot(a, *me), local_sems.at[a]) for a in range(n)]
        for cp in mine:
            cp.start()
        first = []
        for a in range(n):
            first.append(copy(a, 0, me, sibling, from_input=True))
            first += [copy(a, 1 + j, me, (*chip, c), from_input=True) for j, chip in enumerate(chips)]
        for cp in first:
            cp.start()
        passed = []
        for j, chip in enumerate(chips):
            for a in range(n):
                copy(a, 1 + j, (*chip, c), me).wait_recv()
                fwd = copy(a, 4 + j, (*chip, c), sibling)
                fwd.start()
                passed.append(fwd)
        for a in range(n):
            copy(a, 0, sibling, me).wait_recv()
            for j, chip in enumerate(chips):
                copy(a, 4 + j, (*chip, 1 - c), me).wait_recv()
        for cp in first + passed:
            cp.wait_send()
        for cp in mine:
            cp.wait()

    return _pcall(
        body,
        name=name,
        in_specs=[HBM_SPEC] * n,
        out_specs=[HBM_SPEC] * n,
        out_shape=[jax.ShapeDtypeStruct((N_DEV,) + p.shape, p.dtype) for p in packs],
        scratch_shapes=[pltpu.SemaphoreType.DMA((n, N_DEV - 1)), pltpu.SemaphoreType.DMA((n, N_DEV - 1)), pltpu.SemaphoreType.DMA((n,))],
    )(*packs)


def _token_call(body, name, n_tiles, in_specs, inputs, out_specs, out_shape, scratch, comm):
    n_in, n_out, n_scr = len(inputs), len(out_shape), len(scratch)
    comm = comm or []
    nc = len(comm)
    full_body = body
    if nc:
        kinds = [k for k, _ in comm]
        c_in, c_out, c_shape, c_scr = _exchange_io(comm)
        in_specs, out_specs, out_shape, scratch = in_specs + c_in, out_specs + c_out, out_shape + c_shape, scratch + c_scr

        def full_body(*refs):
            ins, cins = refs[:n_in], refs[n_in : n_in + nc]
            o0 = n_in + nc
            outs, couts = refs[o0 : o0 + n_out], refs[o0 + n_out : o0 + n_out + nc]
            s0 = o0 + n_out + nc
            scr, sems = refs[s0 : s0 + n_scr], refs[s0 + n_scr :]

            @pl.when(pl.program_id(0) == 0)
            def _():
                _exchange_start(_exchange_copies(kinds, cins, couts, *sems, with_arrivals=False))

            body(*ins, *outs, *scr)

            @pl.when(pl.program_id(0) == n_tiles - 1)
            def _():
                _exchange_finish(_exchange_copies(kinds, cins, couts, *sems))

    res = _pcall(
        full_body, name=name, grid=(n_tiles,), in_specs=in_specs, out_specs=out_specs, out_shape=out_shape, scratch_shapes=scratch, compiler_params=_cparams(1)
    )(*inputs, *[a for _, a in comm])
    return res[:n_out], res[n_out:]


def rms_matmul_nt(x, g, wts, tt, name, comm=None, swiglu=False, gelu=False):
    T, D = x.shape
    N = wts[0].shape[0]
    nw = len(wts)
    n_out = 3 if swiglu else (2 if gelu else nw)
    nc = _tile(N, MXU_COLS_V7X if (swiglu or gelu) else 1408, 128)

    def body(*refs):
        x_ref, g_ref = refs[0], refs[1]
        w_refs, o_refs = refs[2 : 2 + nw], refs[2 + nw : 2 + nw + n_out]
        xv = x_ref[...]
        h = (xv * _rstd(xv) * g_ref[...]).astype(BF16)
        if gelu:
            z_ref, act_ref = o_refs
            for j in range(N // nc):
                cols = slice(j * nc, (j + 1) * nc)
                z = lax.dot_general(h, w_refs[0][cols, :], NT, preferred_element_type=F32)
                z_ref[:, cols] = z.astype(BF16)
                act_ref[:, cols] = (z * _gelu_parts(z)[2]).astype(BF16)
            return
        if swiglu:
            silu_ref, dact_ref, act_ref = o_refs
            for j in range(N // nc):
                cols = slice(j * nc, (j + 1) * nc)
                gate = lax.dot_general(h, w_refs[0][cols, :], NT, preferred_element_type=F32)
                up = lax.dot_general(h, w_refs[1][cols, :], NT, preferred_element_type=F32)
                s = _sigmoid(gate)
                t = gate * s
                silu_ref[:, cols] = t.astype(BF16)
                act_ref[:, cols] = (t * up).astype(BF16)
                dact_ref[:, cols] = (up * (s + t - t * s)).astype(BF16)
            return
        for w_ref, o_ref in zip(w_refs, o_refs):
            for j in range(N // nc):
                cols = slice(j * nc, (j + 1) * nc)
                o_ref[:, cols] = lax.dot_general(h, w_ref[cols, :], NT, preferred_element_type=F32).astype(BF16)

    return _token_call(
        body,
        name,
        T // tt,
        [pl.BlockSpec((tt, D), lambda i: (i, 0)), _resident((1, D))] + [_resident((N, D))] * nw,
        [x, g, *wts],
        [pl.BlockSpec((tt, N), lambda i: (i, 0))] * n_out,
        [jax.ShapeDtypeStruct((T, N), BF16)] * n_out,
        [],
        comm,
    )


def matmul_rms_res(x, a, w, g, scale, tt, name, target=None):
    T, D = x.shape
    K = w.shape[0]
    loss = target is not None

    def body(*refs):
        if loss:
            x_ref, a_ref, w_ref, g_ref, t_ref, dy_ref, m_ref, sq_ref = refs

            @pl.when(pl.program_id(0) == 0)
            def _():
                sq_ref[...] = jnp.zeros_like(sq_ref)

        else:
            x_ref, a_ref, w_ref, g_ref, y_ref, m_ref = refs
        m = jnp.dot(a_ref[...], w_ref[...], preferred_element_type=F32)
        m_ref[...] = m.astype(BF16)
        y = x_ref[...] + scale * (m * _rstd(m) * g_ref[...])
        if loss:
            err = y - t_ref[...]
            dy_ref[...] = err * (1.0 / D)
            sq_ref[...] += _sum8(err * err)
        else:
            y_ref[...] = y

    row = pl.BlockSpec((tt, D), lambda i: (i, 0))
    outs, _ = _token_call(
        body,
        name,
        T // tt,
        [row, pl.BlockSpec((tt, K), lambda i: (i, 0)), _resident((K, D)), _resident((1, D))] + ([row] if loss else []),
        [x, a, w, g] + ([target] if loss else []),
        [row, row] + ([pl.BlockSpec((8, D), lambda i: (0, 0))] if loss else []),
        [jax.ShapeDtypeStruct((T, D), F32), jax.ShapeDtypeStruct((T, D), BF16)] + ([jax.ShapeDtypeStruct((8, D), F32)] if loss else []),
        [],
        None,
    )
    return outs


def rms_bwd_matmul_nt(dxo, m, g, w, scale, gate_up, tt, name, comm=None):
    T, D = dxo.shape
    K = w.shape[0]
    act = gate_up is not None
    ec = _tile(K, MXU_COLS_V7X if act else 1024, 128)

    def body(*refs):
        if act:
            dxo_ref, m_ref, g_ref, w_ref, silu_ref, dact_ref, dm_ref, dg_ref, dgate_ref, dup_ref = refs
        else:
            dxo_ref, m_ref, g_ref, w_ref, dm_ref, dg_ref, da_ref = refs

        @pl.when(pl.program_id(0) == 0)
        def _():
            dg_ref[...] = jnp.zeros_like(dg_ref)

        dy = scale * dxo_ref[...]
        mv = m_ref[...].astype(F32)
        r = _rstd(mv)
        mh = mv * r
        dg_ref[...] += _sum8(dy * mh)
        dmh = dy * g_ref[...]
        dm = (r * (dmh - mh * jnp.mean(dmh * mh, axis=-1, keepdims=True))).astype(BF16)
        dm_ref[...] = dm
        for j in range(K // ec):
            cols = slice(j * ec, (j + 1) * ec)
            da = lax.dot_general(dm, w_ref[cols, :], NT, preferred_element_type=F32)
            if act:
                dup_ref[:, cols] = (da * silu_ref[:, cols].astype(F32)).astype(BF16)
                dgate_ref[:, cols] = (da * dact_ref[:, cols].astype(F32)).astype(BF16)
            else:
                da_ref[:, cols] = da.astype(BF16)

    row = pl.BlockSpec((tt, D), lambda i: (i, 0))
    col = pl.BlockSpec((tt, K), lambda i: (i, 0))
    n_col = 2 if act else 1
    return _token_call(
        body,
        name,
        T // tt,
        [row, row, _resident((1, D)), _resident((K, D))] + ([col, col] if act else []),
        [dxo, m, g, w, *(gate_up or ())],
        [row, pl.BlockSpec((8, D), lambda i: (0, 0))] + [col] * n_col,
        [jax.ShapeDtypeStruct((T, D), BF16), jax.ShapeDtypeStruct((8, D), F32)] + [jax.ShapeDtypeStruct((T, K), BF16)] * n_col,
        [],
        comm,
    )


def matmul_rms_bwd(dxo, x, g, pairs, tt, name, comm=None):
    T, D = x.shape
    N = pairs[0][0].shape[1]
    npairs = len(pairs)

    def body(*refs):
        dxo_ref, x_ref, g_ref = refs[:3]
        dz_refs = refs[3 : 3 + npairs]
        wt_refs = refs[3 + npairs : 3 + 2 * npairs]
        dx_ref, h_ref, dg_ref = refs[3 + 2 * npairs :]

        @pl.when(pl.program_id(0) == 0)
        def _():
            dg_ref[...] = jnp.zeros_like(dg_ref)

        dh = None
        for dz_ref, wt_ref in zip(dz_refs, wt_refs):
            part = jnp.dot(dz_ref[...], wt_ref[...], preferred_element_type=F32)
            dh = part if dh is None else dh + part
        xv = x_ref[...]
        gv = g_ref[...]
        r = _rstd(xv)
        xh = xv * r
        h_ref[...] = (xh * gv).astype(BF16)
        dg_ref[...] += _sum8(dh * xh)
        dxh = dh * gv
        dx_ref[...] = dxo_ref[...] + r * (dxh - xh * jnp.mean(dxh * xh, axis=-1, keepdims=True))

    row = pl.BlockSpec((tt, D), lambda i: (i, 0))
    return _token_call(
        body,
        name,
        T // tt,
        [row, row, _resident((1, D))] + [pl.BlockSpec((tt, N), lambda i: (i, 0))] * npairs + [_resident((N, D))] * npairs,
        [dxo, x, g, *[p[0] for p in pairs], *[p[1] for p in pairs]],
        [row, row, pl.BlockSpec((8, D), lambda i: (0, 0))],
        [jax.ShapeDtypeStruct((T, D), F32), jax.ShapeDtypeStruct((T, D), BF16), jax.ShapeDtypeStruct((8, D), F32)],
        [],
        comm,
    )


def ffn_bwd(dxo, m, g_post, w_down, silu, dact, x, g_pre, w_gate_t, w_up_t, scale, tt, name, comm=None):
    T, D = x.shape
    F = w_down.shape[0]
    ec = _tile(F, MXU_COLS_V7X, 128)

    def body(dxo_ref, m_ref, gpost_ref, wd_ref, silu_ref, dact_ref, x_ref, gpre_ref, wg_ref, wu_ref, dm_ref, dgpost_ref, dgate_ref, dup_ref, dx_ref, h_ref, dgpre_ref):
        @pl.when(pl.program_id(0) == 0)
        def _():
            dgpost_ref[...] = jnp.zeros_like(dgpost_ref)
            dgpre_ref[...] = jnp.zeros_like(dgpre_ref)

        dxo_v = dxo_ref[...]
        dy = scale * dxo_v
        mv = m_ref[...].astype(F32)
        r = _rstd(mv)
        mh = mv * r
        dgpost_ref[...] += _sum8(dy * mh)
        dmh = dy * gpost_ref[...]
        dm = (r * (dmh - mh * jnp.mean(dmh * mh, axis=-1, keepdims=True))).astype(BF16)
        dm_ref[...] = dm
        for j in range(F // ec):
            cols = slice(j * ec, (j + 1) * ec)
            da = lax.dot_general(dm, wd_ref[cols, :], NT, preferred_element_type=F32)
            dup_ref[:, cols] = (da * silu_ref[:, cols].astype(F32)).astype(BF16)
            dgate_ref[:, cols] = (da * dact_ref[:, cols].astype(F32)).astype(BF16)
        dh = jnp.dot(dgate_ref[...], wg_ref[...], preferred_element_type=F32) + jnp.dot(dup_ref[...], wu_ref[...], preferred_element_type=F32)
        xv = x_ref[...]
        gv = gpre_ref[...]
        rx = _rstd(xv)
        xh = xv * rx
        h_ref[...] = (xh * gv).astype(BF16)
        dgpre_ref[...] += _sum8(dh * xh)
        dxh = dh * gv
        dx_ref[...] = dxo_v + rx * (dxh - xh * jnp.mean(dxh * xh, axis=-1, keepdims=True))

    row = pl.BlockSpec((tt, D), lambda i: (i, 0))
    col = pl.BlockSpec((tt, F), lambda i: (i, 0))
    vec = _resident((1, D))
    wgt = _resident((F, D))
    acc8 = pl.BlockSpec((8, D), lambda i: (0, 0))
    return _token_call(
        body,
        name,
        T // tt,
        [row, row, vec, wgt, col, col, row, vec, wgt, wgt],
        [dxo, m, g_post, w_down, silu, dact, x, g_pre, w_gate_t, w_up_t],
        [row, acc8, col, col, row, row, acc8],
        [
            jax.ShapeDtypeStruct((T, D), BF16),
            jax.ShapeDtypeStruct((8, D), F32),
            jax.ShapeDtypeStruct((T, F), BF16),
            jax.ShapeDtypeStruct((T, F), BF16),
            jax.ShapeDtypeStruct((T, D), F32),
            jax.ShapeDtypeStruct((T, D), BF16),
            jax.ShapeDtypeStruct((8, D), F32),
        ],
        [],
        comm,
    )


def wgrad_tn(a, b, tk, name, comm=None):
    T, M = a.shape
    N = b.shape[1]
    assert M * N * 4 <= WGRAD_ACC_BYTES, (M, N)
    nt = T // tk

    def body(a_ref, b_ref, o_ref, acc_ref):
        k = pl.program_id(0)

        @pl.when(k == 0)
        def _():
            acc_ref[...] = jnp.zeros_like(acc_ref)

        acc_ref[...] += lax.dot_general(a_ref[...], b_ref[...], TN, preferred_element_type=F32)

        @pl.when(k == nt - 1)
        def _():
            o_ref[...] = acc_ref[...].astype(BF16)

    (out,), got = _token_call(
        body,
        name,
        nt,
        [pl.BlockSpec((tk, M), lambda k: (k, 0)), pl.BlockSpec((tk, N), lambda k: (k, 0))],
        [a, b],
        [pl.BlockSpec((M, N), lambda k: (0, 0))],
        [jax.ShapeDtypeStruct((M, N), BF16)],
        [pltpu.VMEM((M, N), F32)],
        comm,
    )
    return (out, got) if comm else out


def _sgu_recompute(z_ref, lng_ref, lnb_ref, S):
    z = z_ref[...].astype(F32)
    zu, zv = z[:, :S], z[:, S:]
    pu, pv = _gelu_parts(zu), _gelu_parts(zv)
    u = zu * pu[2]
    v = zv * pv[2]
    vc = v - jnp.mean(v, axis=-1, keepdims=True)
    rstd = lax.rsqrt(jnp.mean(vc * vc, axis=-1, keepdims=True) + EPS)
    vhat = vc * rstd
    vln = (vhat * lng_ref[...] + lnb_ref[...]).astype(BF16)
    return zu, zv, pu, pv, u, vhat, rstd, vln


def sgu_mix_fwd(zpre, lng, lnb, ws, bias_full, tt, name, comm=None):
    T, S2 = zpre.shape
    S = S2 // 2
    dg = S // GROUPS

    def body(z_ref, lng_ref, lnb_ref, ws_ref, bias_ref, o_ref):
        act = z_ref[...].astype(F32)
        u, v = act[:, :S], act[:, S:]
        vc = v - jnp.mean(v, axis=-1, keepdims=True)
        vln = (vc * lax.rsqrt(jnp.mean(vc * vc, axis=-1, keepdims=True) + EPS) * lng_ref[...] + lnb_ref[...]).astype(BF16)
        for c in range(tt // CHUNK):
            rows = slice(c * CHUNK, (c + 1) * CHUNK)
            for gi in range(GROUPS):
                cols = slice(gi * dg, (gi + 1) * dg)
                mixed = jnp.dot(ws_ref[gi], vln[rows, cols], preferred_element_type=F32) + bias_ref[:, cols]
                o_ref[rows, cols] = (u[rows, cols] * mixed).astype(BF16)

    vec = pl.BlockSpec((1, S), lambda i: (0, 0))
    return _token_call(
        body,
        name,
        T // tt,
        [
            pl.BlockSpec((tt, S2), lambda i: (i, 0)),
            vec,
            vec,
            pl.BlockSpec((GROUPS, CHUNK, CHUNK), lambda i: (0, 0, 0)),
            pl.BlockSpec((CHUNK, S), lambda i: (0, 0)),
        ],
        [zpre, lng, lnb, ws, bias_full],
        [pl.BlockSpec((tt, S), lambda i: (i, 0))],
        [jax.ShapeDtypeStruct((T, S), BF16)],
        [],
        comm,
    )


def sgu_mix_bwd(dxo, m, g_post, w_out, zpre, lng, lnb, ws, ws_t, bias_full, tt, name, comm=None):
    T, S2 = zpre.shape
    D = dxo.shape[1]
    S = S2 // 2
    dg = S // GROUPS

    def body(dxo_ref, m_ref, gpost_ref, wout_ref, z_ref, lng_ref, lnb_ref, ws_ref, wst_ref, bias_ref, dm_ref, dgpost_ref, dz_ref, dws_ref, dbias_ref, dlng_ref, dlnb_ref, du_ref, dvln_ref, dgt_ref):
        @pl.when(pl.program_id(0) == 0)
        def _():
            dgpost_ref[...] = jnp.zeros_like(dgpost_ref)
            dws_ref[...] = jnp.zeros_like(dws_ref)
            dbias_ref[...] = jnp.zeros_like(dbias_ref)
            dlng_ref[...] = jnp.zeros_like(dlng_ref)
            dlnb_ref[...] = jnp.zeros_like(dlnb_ref)

        dy = dxo_ref[...]
        mv = m_ref[...].astype(F32)
        r = _rstd(mv)
        mh = mv * r
        dgpost_ref[...] += _sum8(dy * mh)
        dmh = dy * gpost_ref[...]
        dm = (r * (dmh - mh * jnp.mean(dmh * mh, axis=-1, keepdims=True))).astype(BF16)
        dm_ref[...] = dm
        dgt_ref[...] = lax.dot_general(dm, wout_ref[...], NT, preferred_element_type=F32)

        zu, zv, pu, pv, u, vhat, rstd, vln = _sgu_recompute(z_ref, lng_ref, lnb_ref, S)
        for c in range(tt // CHUNK):
            rows = slice(c * CHUNK, (c + 1) * CHUNK)
            for gi in range(GROUPS):
                cols = slice(gi * dg, (gi + 1) * dg)
                v_cg = vln[rows, cols]
                mixed = jnp.dot(ws_ref[gi], v_cg, preferred_element_type=F32) + bias_ref[:, cols]
                dgt = dgt_ref[rows, cols]
                du_ref[rows, cols] = dgt * mixed
                dmx = dgt * u[rows, cols]
                dbias_ref[:, cols] += dmx
                dmx16 = dmx.astype(BF16)
                dws_ref[gi] += lax.dot_general(dmx16, v_cg, NT, preferred_element_type=F32)
                dvln_ref[rows, cols] = jnp.dot(wst_ref[gi], dmx16, preferred_element_type=F32)

        dvln = dvln_ref[...]
        dlng_ref[...] += _sum8(dvln * vhat)
        dlnb_ref[...] += _sum8(dvln)
        dvhat = dvln * lng_ref[...]
        dv = rstd * (dvhat - jnp.mean(dvhat, axis=-1, keepdims=True) - vhat * jnp.mean(dvhat * vhat, axis=-1, keepdims=True))
        dz_ref[:, :S] = (du_ref[...] * _gelu_grad(zu, pu)).astype(BF16)
        dz_ref[:, S:] = (dv * _gelu_grad(zv, pv)).astype(BF16)

    vec = pl.BlockSpec((1, S), lambda i: (0, 0))
    wsp = pl.BlockSpec((GROUPS, CHUNK, CHUNK), lambda i: (0, 0, 0))
    full = pl.BlockSpec((CHUNK, S), lambda i: (0, 0))
    acc8 = pl.BlockSpec((8, S), lambda i: (0, 0))
    row = pl.BlockSpec((tt, D), lambda i: (i, 0))
    return _token_call(
        body,
        name,
        T // tt,
        [row, row, _resident((1, D)), _resident((S, D)), pl.BlockSpec((tt, S2), lambda i: (i, 0)), vec, vec, wsp, wsp, full],
        [dxo, m, g_post, w_out, zpre, lng, lnb, ws, ws_t, bias_full],
        [row, pl.BlockSpec((8, D), lambda i: (0, 0)), pl.BlockSpec((tt, S2), lambda i: (i, 0)), wsp, full, acc8, acc8],
        [
            jax.ShapeDtypeStruct((T, D), BF16),
            jax.ShapeDtypeStruct((8, D), F32),
            jax.ShapeDtypeStruct((T, S2), BF16),
            jax.ShapeDtypeStruct((GROUPS, CHUNK, CHUNK), F32),
            jax.ShapeDtypeStruct((CHUNK, S), F32),
            jax.ShapeDtypeStruct((8, S), F32),
            jax.ShapeDtypeStruct((8, S), F32),
        ],
        [pltpu.VMEM((tt, S), F32), pltpu.VMEM((tt, S), F32), pltpu.VMEM((tt, S), F32)],
        comm,
    )


def _shifted_planes(sh, tt):
    n = tt + HALO - SUBLANES
    for s in range(1, SUBLANES):
        sh[s, 0:n, :] = sh[0, pl.ds(s, n), :]


def conv_mid_fwd(p, wdw, bdw, lng, lnb, tt, name, comm=None):
    T, C2 = p.shape
    C = C2 // 2

    def body(p_ref, w_ref, b_ref, lng_ref, lnb_ref, yc_ref, ys_ref, sh):
        @pl.when(pl.program_id(0) == 0)
        def _():
            sh[0, 0:HALO, :] = jnp.zeros((HALO, C), F32)

        @pl.when(pl.program_id(0) > 0)
        def _():
            sh[0, 0:HALO, :] = sh[0, tt : tt + HALO, :]

        pv = p_ref[...].astype(F32)
        sh[0, HALO:, :] = pv[:, :C] * _sigmoid(pv[:, C:])
        _shifted_planes(sh, tt)
        for lb in range(C // LANES):
            lanes = slice(lb * LANES, (lb + 1) * LANES)
            wk = [jnp.broadcast_to(w_ref[k : k + 1, lanes], (SUBLANES, LANES)) for k in range(CONV_W)]
            bias = jnp.broadcast_to(b_ref[:, lanes], (SUBLANES, LANES))

            def rows(i, carry):
                r0 = pl.multiple_of(i * CONV_ROWS, CONV_ROWS)
                accs = [[bias, jnp.zeros((SUBLANES, LANES), F32)] for _ in range(CONV_ROWS // SUBLANES)]
                for k in range(CONV_W):
                    o = HALO - (CONV_W - 1) + k
                    for j, acc in enumerate(accs):
                        blk = sh[o % SUBLANES, pl.ds(r0 + (o // SUBLANES + j) * SUBLANES, SUBLANES), lanes]
                        acc[k % 2] = acc[k % 2] + wk[k] * blk
                for j, acc in enumerate(accs):
                    yc_ref[pl.ds(r0 + j * SUBLANES, SUBLANES), lanes] = acc[0] + acc[1]
                return carry

            lax.fori_loop(0, tt // CONV_ROWS, rows, 0)
        acc = yc_ref[...]
        yc = acc - jnp.mean(acc, axis=-1, keepdims=True)
        yn = yc * lax.rsqrt(jnp.mean(yc * yc, axis=-1, keepdims=True) + EPS) * lng_ref[...] + lnb_ref[...]
        ys_ref[...] = (yn * _sigmoid(yn)).astype(BF16)

    vec = pl.BlockSpec((1, C), lambda i: (0, 0))
    row = pl.BlockSpec((tt, C), lambda i: (i, 0))
    return _token_call(
        body,
        name,
        T // tt,
        [pl.BlockSpec((tt, C2), lambda i: (i, 0)), pl.BlockSpec((HALO, C), lambda i: (0, 0)), vec, vec, vec],
        [p, wdw, bdw, lng, lnb],
        [row, row],
        [jax.ShapeDtypeStruct((T, C), F32), jax.ShapeDtypeStruct((T, C), BF16)],
        [pltpu.VMEM((SUBLANES, tt + HALO, C), F32)],
        comm,
    )


def conv_mid_bwd(dys, yc, p, wdw, lng, lnb, tt, name, comm=None):
    T, C2 = p.shape
    C = C2 // 2
    n = T // tt

    def body(dys_ref, yc_ref, p_ref, w_ref, lng_ref, lnb_ref, dp_ref, dw_ref, db_ref, dlng_ref, dlnb_ref, sh, y_s, dy_s):
        @pl.when(pl.program_id(0) == 0)
        def _():
            sh[0, tt : tt + HALO, :] = jnp.zeros((HALO, C), F32)
            dw_ref[...] = jnp.zeros_like(dw_ref)
            db_ref[...] = jnp.zeros_like(db_ref)
            dlng_ref[...] = jnp.zeros_like(dlng_ref)
            dlnb_ref[...] = jnp.zeros_like(dlnb_ref)

        @pl.when(pl.program_id(0) > 0)
        def _():
            sh[0, tt : tt + HALO, :] = sh[0, 0:HALO, :]

        ycv = yc_ref[...]
        ycc = ycv - jnp.mean(ycv, axis=-1, keepdims=True)
        rstd = lax.rsqrt(jnp.mean(ycc * ycc, axis=-1, keepdims=True) + EPS)
        yhat = ycc * rstd
        lng_v = lng_ref[...]
        yn = yhat * lng_v + lnb_ref[...]
        sg = _sigmoid(yn)
        dyn = dys_ref[...].astype(F32) * (sg * (1.0 + yn * (1.0 - sg)))
        dlng_ref[...] += _sum8(dyn * yhat)
        dlnb_ref[...] += _sum8(dyn)
        dyh = dyn * lng_v
        dyc = rstd * (dyh - jnp.mean(dyh, axis=-1, keepdims=True) - yhat * jnp.mean(dyh * yhat, axis=-1, keepdims=True))
        db_ref[...] += _sum8(dyc)
        sh[0, 0:tt, :] = dyc
        _shifted_planes(sh, tt)

        pv = p_ref[...].astype(F32)
        a = pv[:, :C]
        sgate = _sigmoid(pv[:, C:])
        y_s[...] = a * sgate
        for lb in range(C // LANES):
            lanes = slice(lb * LANES, (lb + 1) * LANES)
            for k0, k1 in ((0, CONV_W // 2), (CONV_W // 2, CONV_W)):
                wk = [jnp.broadcast_to(w_ref[k : k + 1, lanes], (SUBLANES, LANES)) for k in range(k0, k1)]

                def rows(i, dw_acc, k0=k0, k1=k1, wk=wk, lanes=lanes):
                    r0 = pl.multiple_of(i * CONV_ROWS, CONV_ROWS)
                    dw_acc = list(dw_acc)
                    zero = jnp.zeros((SUBLANES, LANES), F32)
                    ybs, accs = [], []
                    for j in range(CONV_ROWS // SUBLANES):
                        at = pl.ds(r0 + j * SUBLANES, SUBLANES)
                        ybs.append(y_s[at, lanes])
                        accs.append([zero if k0 == 0 else dy_s[at, lanes], zero])
                    for n_k, k in enumerate(range(k0, k1)):
                        o = CONV_W - 1 - k
                        for j, acc in enumerate(accs):
                            blk = sh[o % SUBLANES, pl.ds(r0 + (o // SUBLANES + j) * SUBLANES, SUBLANES), lanes]
                            acc[n_k % 2] = acc[n_k % 2] + wk[n_k] * blk
                            dw_acc[n_k] = dw_acc[n_k] + ybs[j] * blk
                    for j, acc in enumerate(accs):
                        dy_s[pl.ds(r0 + j * SUBLANES, SUBLANES), lanes] = acc[0] + acc[1]
                    return tuple(dw_acc)

                dw_acc = lax.fori_loop(0, tt // CONV_ROWS, rows, tuple(jnp.zeros((SUBLANES, LANES), F32) for _ in range(k0, k1)))
                for n_k, k in enumerate(range(k0, k1)):
                    dw_ref[SUBLANES * k : SUBLANES * (k + 1), lanes] += dw_acc[n_k]
        dy = dy_s[...]
        dp_ref[:, :C] = (dy * sgate).astype(BF16)
        dp_ref[:, C:] = (dy * a * sgate * (1.0 - sgate)).astype(BF16)

    vec = pl.BlockSpec((1, C), lambda i: (0, 0))
    row = pl.BlockSpec((tt, C), lambda i: (n - 1 - i, 0))
    row2 = pl.BlockSpec((tt, C2), lambda i: (n - 1 - i, 0))
    acc8 = pl.BlockSpec((8, C), lambda i: (0, 0))
    return _token_call(
        body,
        name,
        n,
        [row, row, row2, pl.BlockSpec((HALO, C), lambda i: (0, 0)), vec, vec],
        [dys, yc, p, wdw, lng, lnb],
        [row2, pl.BlockSpec((HALO * 8, C), lambda i: (0, 0)), acc8, acc8, acc8],
        [
            jax.ShapeDtypeStruct((T, C2), BF16),
            jax.ShapeDtypeStruct((HALO * 8, C), F32),
            jax.ShapeDtypeStruct((8, C), F32),
            jax.ShapeDtypeStruct((8, C), F32),
            jax.ShapeDtypeStruct((8, C), F32),
        ],
        [pltpu.VMEM((SUBLANES, tt + HALO, C), F32), pltpu.VMEM((tt, C), F32), pltpu.VMEM((tt, C), F32)],
        comm,
    )


def sum_slots(slots, name):
    _, r, w = slots.shape

    def body(s_ref, o_ref):
        total = s_ref[0].astype(F32)
        for p in range(1, N_DEV):
            total = total + s_ref[p].astype(F32)
        o_ref[...] = total

    return _pcall(body, name=name, out_shape=jax.ShapeDtypeStruct((r, w), F32), compiler_params=pltpu.CompilerParams(vmem_limit_bytes=VMEM_LIMIT_V7X))(slots)


def adamw(w, g, m, v, name):
    R, C = w.shape
    tr = _tile(R, 512, 8)
    c1 = 1.0 - ADAM_B1**ADAM_STEP
    c2 = 1.0 - ADAM_B2**ADAM_STEP

    def body(w_ref, g_ref, m_ref, v_ref, d_ref, mo_ref, vo_ref):
        gv = g_ref[...]
        m2 = ADAM_B1 * m_ref[...] + (1.0 - ADAM_B1) * gv
        v2 = ADAM_B2 * v_ref[...] + (1.0 - ADAM_B2) * (gv * gv)
        mo_ref[...] = m2
        vo_ref[...] = v2
        d_ref[...] = -ADAM_LR * ((m2 / c1) / (jnp.sqrt(v2 / c2) + ADAM_EPS) + ADAM_WD * w_ref[...])

    blk = pl.BlockSpec((tr, C), lambda i: (i, 0))
    return _pcall(
        body,
        name=name,
        grid=(R // tr,),
        in_specs=[blk] * 4,
        out_specs=[blk] * 3,
        out_shape=[jax.ShapeDtypeStruct((R, C), F32)] * 3,
        compiler_params=_cparams(1),
    )(w, g, m, v)


def _sublayers(depth):
    out = []
    for layer in range(depth):
        out.append(("ffn", layer, 0))
        out.append(("sgu" if layer % N_MIXERS == 0 else "conv", layer, layer // N_MIXERS))
        out.append(("ffn", layer, 1))
    return out


def _row_blocks(sub, shards):
    kind, layer, idx = sub
    if kind == "ffn":
        blk = {"gate_t": shards["ff_w_gate"][layer, idx].T, "up_t": shards["ff_w_up"][layer, idx].T, "down": shards["ff_w_down"][layer, idx]}
    elif kind == "sgu":
        blk = {"in_t": shards["sgu_w_in"][idx].T, "out": shards["sgu_w_out"][idx]}
    else:
        blk = {"pw1_t": shards["conv_w_pw1"][idx].T, "pw2": shards["conv_w_pw2"][idx]}
    return {nm: arr.astype(BF16) for nm, arr in blk.items()}


def _pad8(a):
    return jnp.pad(a, ((0, (-a.shape[0]) % 8), (0, 0)))


def kernel(x, norm_g, ff_w_gate, ff_w_up, ff_w_down, sgu_w_in, sgu_ln_g, sgu_ln_b, sgu_w_spatial, sgu_b_spatial, sgu_w_out, conv_w_pw1, conv_w_dw, conv_b_dw, conv_ln_g, conv_ln_b, conv_w_pw2, loss_target, m_norm_g, m_ff_w_gate, m_ff_w_up, m_ff_w_down, m_sgu_w_in, m_sgu_ln_g, m_sgu_ln_b, m_sgu_w_spatial, m_sgu_b_spatial, m_sgu_w_out, m_conv_w_pw1, m_conv_w_dw, m_conv_b_dw, m_conv_ln_g, m_conv_ln_b, m_conv_w_pw2, v_norm_g, v_ff_w_gate, v_ff_w_up, v_ff_w_down, v_sgu_w_in, v_sgu_ln_g, v_sgu_ln_b, v_sgu_w_spatial, v_sgu_b_spatial, v_sgu_w_out, v_conv_w_pw1, v_conv_w_dw, v_conv_b_dw, v_conv_ln_g, v_conv_ln_b, v_conv_w_pw2):
    names = ["norm_g", "ff_w_gate", "ff_w_up", "ff_w_down", "sgu_w_in", "sgu_ln_g", "sgu_ln_b", "sgu_w_spatial", "sgu_b_spatial", "sgu_w_out", "conv_w_pw1", "conv_w_dw", "conv_b_dw", "conv_ln_g", "conv_ln_b", "conv_w_pw2"]
    weights = dict(zip(names, [norm_g, ff_w_gate, ff_w_up, ff_w_down, sgu_w_in, sgu_ln_g, sgu_ln_b, sgu_w_spatial, sgu_b_spatial, sgu_w_out, conv_w_pw1, conv_w_dw, conv_b_dw, conv_ln_g, conv_ln_b, conv_w_pw2]))
    moments_m = dict(zip(names, [m_norm_g, m_ff_w_gate, m_ff_w_up, m_ff_w_down, m_sgu_w_in, m_sgu_ln_g, m_sgu_ln_b, m_sgu_w_spatial, m_sgu_b_spatial, m_sgu_w_out, m_conv_w_pw1, m_conv_w_dw, m_conv_b_dw, m_conv_ln_g, m_conv_ln_b, m_conv_w_pw2]))
    moments_v = dict(zip(names, [v_norm_g, v_ff_w_gate, v_ff_w_up, v_ff_w_down, v_sgu_w_in, v_sgu_ln_g, v_sgu_ln_b, v_sgu_w_spatial, v_sgu_b_spatial, v_sgu_w_out, v_conv_w_pw1, v_conv_w_dw, v_conv_b_dw, v_conv_ln_g, v_conv_ln_b, v_conv_w_pw2]))

    _, T, D = x.shape
    depth = norm_g.shape[0]
    n_conv = conv_w_dw.shape[0]
    n_sgu = sgu_w_in.shape[0]
    S = sgu_ln_g.shape[1]
    lanes = norm_g.shape[2]
    subs = _sublayers(depth)
    n_sub = len(subs)

    cx, cy, cc = (lax.axis_index(a) for a in MESH_AXES)
    my_block = 4 * cx + 2 * cy + cc

    blocks = [_row_blocks(sub, weights) for sub in subs]
    dw_pad = jnp.pad(conv_w_dw, ((0, 0), (0, HALO - CONV_W), (0, 0)))
    small_parts = [_pad8(p) for p in (norm_g.reshape(-1, lanes), dw_pad.reshape(-1, lanes), conv_b_dw, conv_ln_g, conv_ln_b)]
    small_rows = [p.shape[0] for p in small_parts]
    small = jnp.concatenate(small_parts, axis=0)

    def gathered(block_names, outs):
        return {nm: o.reshape(N_DEV * o.shape[1], o.shape[2]) for nm, o in zip(block_names, outs)}

    first_names = [nm for nm in blocks[0] if nm != "down"]
    first = gather_two_level([blocks[0][nm] for nm in first_names] + [small], name="gather_first")
    W = [None] * n_sub
    W[0] = gathered(first_names, first[:-1])
    gsmall = first[-1]

    small_full = jnp.transpose(gsmall, (1, 0, 2)).reshape(gsmall.shape[1], N_DEV * lanes)
    so = [0]
    for r in small_rows:
        so.append(so[-1] + r)
    norm_full = small_full[so[0] : so[0] + depth * norm_g.shape[1]].reshape(depth, -1, D)
    dw_full = small_full[so[1] : so[1] + n_conv * HALO].reshape(n_conv, HALO, D)
    bdw_full, clng_full, clnb_full = (small_full[so[k] : so[k] + n_conv] for k in (2, 3, 4))

    causal = jnp.tril(jnp.ones((CHUNK, CHUNK), dtype=bool))
    ws_all = jnp.where(causal[None, None], sgu_w_spatial, 0.0).astype(BF16)
    wst_all = jnp.swapaxes(ws_all, -1, -2)
    bias_full_all = jnp.repeat(jnp.swapaxes(sgu_b_spatial, -1, -2), S // GROUPS, axis=-1)

    tt = _tile(T, 512, CHUNK)
    tt_mix = _tile(T, 256, CHUNK)

    tk = _tile(T, 1024, CHUNK)

    def vec(v):
        return v.reshape(1, -1)

    def split_first(comm):
        return (comm[:1], comm[1:]) if comm else (None, None)

    def norms(kind, layer, idx):
        pre = 4 * idx if kind == "ffn" else 2
        return vec(norm_full[layer, pre]), vec(norm_full[layer, pre + 1])

    xs = x[0]
    saved = []
    for si, (kind, layer, idx) in enumerate(subs):
        w = W[si]
        g_pre, g_post = norms(kind, layer, idx)
        nxt = [("gather", a) for a in blocks[si + 1].values()] if si + 1 < n_sub else None
        nxt_first, nxt_rest = split_first(nxt)
        if kind == "ffn":
            own = [("gather", blocks[0]["down"])] if si == 0 else []
            (silu, dact, a), got = rms_matmul_nt(xs, g_pre, [w["gate_t"], w["up_t"]], tt, "ffn_in", own + (nxt or []), swiglu=True)
            if own:
                w.update(gathered(["down"], got[:1]))
                got = got[1:]
            if si == n_sub - 1:
                x_new, o, sq = matmul_rms_res(xs, a, w["down"], g_post, FFN_SCALE, tt, "ffn_out_loss", loss_target[0])
            else:
                x_new, o = matmul_rms_res(xs, a, w["down"], g_post, FFN_SCALE, tk, "ffn_out")
            saved.append((xs, silu, dact, o, a))
        elif kind == "sgu":
            (zpre, uv), got = rms_matmul_nt(xs, g_pre, [w["in_t"]], tt, "sgu_in", nxt_first, gelu=True)
            (gated,), got_rest = sgu_mix_fwd(uv, vec(sgu_ln_g[idx]), vec(sgu_ln_b[idx]), ws_all[idx], bias_full_all[idx], tt_mix, "sgu_mix", nxt_rest)
            got = list(got) + list(got_rest)
            x_new, mm = matmul_rms_res(xs, gated, w["out"], g_post, 1.0, tk, "sgu_out")
            saved.append((xs, zpre, gated, mm))
        else:
            (p,), got = rms_matmul_nt(xs, g_pre, [w["pw1_t"]], tt, "conv_pw1", nxt_first)
            (yc, ys), got_rest = conv_mid_fwd(p, dw_full[idx], vec(bdw_full[idx]), vec(clng_full[idx]), vec(clnb_full[idx]), tt, "conv_mid", nxt_rest)
            got = list(got) + list(got_rest)
            x_new, mm = matmul_rms_res(xs, ys, w["pw2"], g_post, 1.0, tk, "conv_pw2")
            saved.append((xs, p, yc, ys, mm))
        if nxt:
            W[si + 1] = gathered(blocks[si + 1].keys(), got)
        xs = x_new

    dx = xs
    loss = lax.psum(0.5 * jnp.sum(sq) / D, MESH_AXES)

    d_norm = [[None] * norm_full.shape[1] for _ in range(depth)]
    d_sgu = [None] * n_sgu
    d_conv = [None] * n_conv
    slots = [None] * n_sub
    pending = None

    def scatter_of(p):
        return [("scatter", dwm.reshape(N_DEV, dwm.shape[0] // N_DEV, D)) for dwm in p[1].values()] if p else None

    def pack_small():
        def sum8(v):
            return jnp.sum(v, axis=0)

        g_norm = jnp.stack([jnp.stack([sum8(d) for d in row]) for row in d_norm])
        g_dw = jnp.stack([jnp.sum(d[0].reshape(HALO, 8, D), axis=1) for d in d_conv])
        g_bdw, g_clng, g_clnb = (jnp.stack([sum8(d[k]) for d in d_conv]) for k in (1, 2, 3))
        g_slng, g_slnb = (jnp.stack([sum8(d[k]) for d in d_sgu]) for k in (2, 3))
        g_bsp = jnp.stack([jnp.sum(d[1].reshape(CHUNK, GROUPS, S // GROUPS), axis=-1).T for d in d_sgu])
        g_wsp = jnp.stack([jnp.where(causal[None], d[0], 0.0) for d in d_sgu])
        parts = [g_norm, g_dw, g_bdw, g_clng, g_clnb, g_slng, g_slnb, g_bsp, g_wsp]
        return parts, [p.size // D for p in parts], jnp.concatenate([_pad8(p.reshape(-1, D)) for p in parts], axis=0)

    for si in reversed(range(n_sub)):
        kind, layer, idx = subs[si]
        w = W[si]
        g_pre, g_post = norms(kind, layer, idx)
        pre = 4 * idx if kind == "ffn" else 2
        sc_first, sc_rest = split_first(scatter_of(pending))
        if kind == "ffn":
            xs_in, silu, dact, o, a = saved[si]
            (do, dg_post, dgate, dup, dx, h, dg_pre), got = ffn_bwd(
                dx, o, g_post, w["down"], silu, dact, xs_in, g_pre, w["gate_t"], w["up_t"], FFN_SCALE, tt_mix, "ffn_bwd", scatter_of(pending)
            )
            if si == 0:
                d_norm[layer][pre], d_norm[layer][pre + 1] = dg_pre, dg_post
                sparts, srows, sgrad = pack_small()
                dw_down, got_small = wgrad_tn(a, do, tk, "wgrad_ffn_out", [("gather", sgrad)])
                dw_up, got_down = wgrad_tn(dup, h, tk, "wgrad_ffn_in", scatter_of((si, {"down": dw_down})))
                dw_gate, got_up = wgrad_tn(dgate, h, tk, "wgrad_ffn_in", scatter_of((si, {"up_t": dw_up})))
            else:
                dw_down = wgrad_tn(a, do, tk, name="wgrad_ffn_out")
                dw_up = wgrad_tn(dup, h, tk, name="wgrad_ffn_in")
                dw_gate = wgrad_tn(dgate, h, tk, name="wgrad_ffn_in")
            dws_now = {"gate_t": dw_gate, "up_t": dw_up, "down": dw_down}
        elif kind == "sgu":
            xs_in, zpre, gated, mm = saved[si]
            (dm, dg_post, dzpre, dws, dbias, dlng, dlnb), got = sgu_mix_bwd(
                dx, mm, g_post, w["out"], zpre, vec(sgu_ln_g[idx]), vec(sgu_ln_b[idx]), ws_all[idx], wst_all[idx], bias_full_all[idx], tt_mix, "sgu_mix_bwd", scatter_of(pending)
            )
            (dx, h, dg_pre), _ = matmul_rms_bwd(dx, xs_in, g_pre, [(dzpre, w["in_t"])], tt, "sgu_in_bwd")
            dws_now = {"in_t": wgrad_tn(dzpre, h, tt, name="wgrad_sgu_in"), "out": wgrad_tn(gated, dm, tk, name="wgrad_sgu_out")}
            d_sgu[idx] = (dws, dbias, dlng, dlnb)
        else:
            xs_in, p, yc, ys, mm = saved[si]
            (dm, dg_post, dys), got = rms_bwd_matmul_nt(dx, mm, g_post, w["pw2"], 1.0, None, tt, "conv_pw2_bwd", sc_first)
            (dp, dwdw, dbdw, dlng, dlnb), got_rest = conv_mid_bwd(dys, yc, p, dw_full[idx], vec(clng_full[idx]), vec(clnb_full[idx]), tt, "conv_mid_bwd", sc_rest)
            got = list(got) + list(got_rest)
            (dx, h, dg_pre), _ = matmul_rms_bwd(dx, xs_in, g_pre, [(dp, w["pw1_t"])], tt, "conv_pw1_bwd")
            dws_now = {"pw1_t": wgrad_tn(dp, h, tk, name="wgrad_conv_pw1"), "pw2": wgrad_tn(ys, dm, tk, name="wgrad_conv_pw2")}
            d_conv[idx] = (dwdw, dbdw, dlng, dlnb)
        d_norm[layer][pre], d_norm[layer][pre + 1] = dg_pre, dg_post
        if pending:
            slots[pending[0]] = dict(zip(pending[1].keys(), got))
        pending = (si, dws_now)
    grad_x = dx[None]

    last = exchange(scatter_of((0, {"gate_t": pending[1]["gate_t"]})), name="scatter_last")
    slots[0] = {"gate_t": last[0], "up_t": got_up[0], "down": got_down[0]}
    stotal = sum_slots(got_small[0], name="sum_slots")

    gs = {(si, nm): sum_slots(s, name="sum_slots") for si in range(n_sub) for nm, s in slots[si].items()}
    grads = {}
    ffn_si = {(layer, idx): si for si, (kind, layer, idx) in enumerate(subs) if kind == "ffn"}
    sgu_si = {idx: si for si, (kind, layer, idx) in enumerate(subs) if kind == "sgu"}
    conv_si = {idx: si for si, (kind, layer, idx) in enumerate(subs) if kind == "conv"}
    grads["ff_w_gate"] = jnp.stack([jnp.stack([gs[(ffn_si[(l, f)], "gate_t")].T for f in range(2)]) for l in range(depth)])
    grads["ff_w_up"] = jnp.stack([jnp.stack([gs[(ffn_si[(l, f)], "up_t")].T for f in range(2)]) for l in range(depth)])
    grads["ff_w_down"] = jnp.stack([jnp.stack([gs[(ffn_si[(l, f)], "down")] for f in range(2)]) for l in range(depth)])
    grads["sgu_w_in"] = jnp.stack([gs[(sgu_si[j], "in_t")].T for j in range(n_sgu)])
    grads["sgu_w_out"] = jnp.stack([gs[(sgu_si[j], "out")] for j in range(n_sgu)])
    grads["conv_w_pw1"] = jnp.stack([gs[(conv_si[j], "pw1_t")].T for j in range(n_conv)])
    grads["conv_w_pw2"] = jnp.stack([gs[(conv_si[j], "pw2")] for j in range(n_conv)])

    so = [0]
    for r in srows:
        so.append(so[-1] + r + (-r) % 8)
    sp = [stotal[so[k] : so[k] + srows[k]].reshape(sparts[k].shape) for k in range(len(sparts))]

    def my_lanes(v):
        return lax.dynamic_slice_in_dim(v, my_block * lanes, lanes, axis=-1)

    grads["norm_g"] = my_lanes(sp[0])
    grads["conv_w_dw"] = my_lanes(sp[1])[:, :CONV_W]
    grads["conv_b_dw"] = my_lanes(sp[2])
    grads["conv_ln_g"] = my_lanes(sp[3])
    grads["conv_ln_b"] = my_lanes(sp[4])
    grads["sgu_ln_g"], grads["sgu_ln_b"], grads["sgu_b_spatial"], grads["sgu_w_spatial"] = sp[5], sp[6], sp[7], sp[8]

    deltas, new_m, new_v = {}, {}, {}
    for nm in names:
        w = weights[nm]
        two_d = (-1, w.shape[-1])
        d, m2, v2 = adamw(w.reshape(two_d), grads[nm].reshape(two_d), moments_m[nm].reshape(two_d), moments_v[nm].reshape(two_d), name="adamw")
        deltas[nm], new_m[nm], new_v[nm] = d.reshape(w.shape), m2.reshape(w.shape), v2.reshape(w.shape)

    return (loss, grad_x, *[grads[n] for n in names], *[deltas[n] for n in names], *[new_m[n] for n in names], *[new_v[n] for n in names])
```
